```python
import math
import jax, jax.numpy as jnp
from jax import lax
import numpy as np

D_MODEL = 2048
BATCH = 1
SEQ = 8192
DEPTH = 4

HEAD_DIM = 128
MLA_HEADS = 8
MLA_Q_RANK = 512
MLA_KV_RANK = 512
MLA_NOPE_DIM = 128
MLA_ROPE_DIM = 64
MLA_V_DIM = 128
MLA_QK_DIM = MLA_NOPE_DIM + MLA_ROPE_DIM
DIL_HEADS = 8
DIL_PATTERNS = ((128, 1), (512, 4), (2048, 16))
ROPE_THETA = 500000.0
PARTIAL_ROPE_DIM = HEAD_DIM // 4
D_FF = 5632
Q_BLOCK = 128
RMS_EPS = 1e-6
NEG = -1e30
MLA_WIDTH = MLA_HEADS * MLA_V_DIM
DIL_WIDTH = DIL_HEADS * HEAD_DIM
D_MIX = MLA_WIDTH + DIL_WIDTH
IN_SIZES = (MLA_Q_RANK, MLA_KV_RANK, MLA_ROPE_DIM, DIL_WIDTH, DIL_WIDTH, DIL_WIDTH)
IN_COLS = sum(IN_SIZES)
IN_SPLITS = tuple(int(v) for v in np.cumsum(IN_SIZES)[:-1])

kernel_name = "hymba_mla_dilated_macaron_sandwich"


def rms_norm(x, g):
    xf = x.astype(jnp.float32)
    y = xf * lax.rsqrt(jnp.mean(xf * xf, axis=-1, keepdims=True) + RMS_EPS)
    return (y * g.astype(jnp.float32)).astype(x.dtype)


def swiglu(h, w_gate, w_up, w_down):
    return (jax.nn.silu(h @ w_gate) * (h @ w_up)) @ w_down


def rope_tables(positions, dim):
    inv = ROPE_THETA ** (-jnp.arange(0, dim, 2, dtype=jnp.float32) / dim)
    ang = positions.astype(jnp.float32)[..., None] * inv
    return jnp.cos(ang), jnp.sin(ang)


def apply_rope(x, cos, sin):
    xf = x.astype(jnp.float32)
    x1, x2 = jnp.split(xf, 2, axis=-1)
    out = jnp.concatenate([x1 * cos - x2 * sin, x2 * cos + x1 * sin], axis=-1)
    return out.astype(x.dtype)


def mla_attention(q_nope, q_rope, k_nope, k_rope, v):
    B, S, H, _ = q_nope.shape
    nb = S // Q_BLOCK
    scale = 1.0 / math.sqrt(MLA_QK_DIM)
    key_idx = jnp.arange(S)

    def to_blocks(t):
        return jnp.moveaxis(t.reshape(B, nb, Q_BLOCK, *t.shape[2:]), 1, 0)

    def one_block(args):
        qn_b, qr_b, i = args
        s = (jnp.einsum('bqhd,bkhd->bhqk', qn_b, k_nope).astype(jnp.float32)
             + jnp.einsum('bqhr,bkr->bhqk', qr_b, k_rope).astype(jnp.float32)) * scale
        q_idx = i * Q_BLOCK + jnp.arange(Q_BLOCK)
        mask = key_idx[None, :] <= q_idx[:, None]
        s = jnp.where(mask, s, NEG)
        p = jax.nn.softmax(s, axis=-1).astype(v.dtype)
        return jnp.einsum('bhqk,bkhd->bqhd', p, v)

    o = lax.map(one_block, (to_blocks(q_nope), to_blocks(q_rope), jnp.arange(nb)))
    return jnp.moveaxis(o, 0, 1).reshape(B, S, H, -1)


def dilated_window_attention(q, k, v, window, dilation):
    B, S, H, Dh = q.shape
    span = window // dilation
    blk = span
    seg = blk * dilation
    s_pad = -(-S // seg) * seg
    L = s_pad // dilation
    nb = L // blk
    scale = 1.0 / math.sqrt(Dh)

    def to_sub(t):
        t = jnp.pad(t, ((0, 0), (0, s_pad - S), (0, 0), (0, 0)))
        t = t.reshape(B, L, dilation, H, Dh).transpose(0, 2, 1, 3, 4)
        return t.reshape(B, dilation, nb, blk, H, Dh)

    def with_prev(t):
        prev = jnp.pad(t[:, :, :-1], ((0, 0), (0, 0), (1, 0), (0, 0), (0, 0), (0, 0)))
        return jnp.concatenate([prev, t], axis=3)

    qb = to_sub(q)
    kc = with_prev(to_sub(k))
    vc = with_prev(to_sub(v))
    s = jnp.einsum('brnqhd,brnkhd->brnhqk', qb, kc).astype(jnp.float32) * scale
    qi = jnp.arange(blk)[:, None]
    kj = jnp.arange(2 * blk)[None, :]
    dist = blk + qi - kj
    band = (dist >= 0) & (dist <= span)
    valid_prev = (jnp.arange(nb)[:, None, None] > 0) | (kj >= blk)[None]
    mask = band[None] & valid_prev
    s = jnp.where(mask[None, None, :, None], s, NEG)
    lse = jax.nn.logsumexp(s, axis=-1)
    p = jnp.exp(s - lse[..., None]).astype(v.dtype)
    o = jnp.einsum('brnhqk,brnkhd->brnqhd', p, vc)

    def from_sub(t):
        rest = t.shape[4:]
        t = t.reshape(B, dilation, L, *rest)
        t = jnp.moveaxis(t, 1, 2).reshape(B, s_pad, *rest)
        return t[:, :S]

    return from_sub(o), from_sub(jnp.moveaxis(lse, -1, -2))


def dilated_mixture(q, k, v):
    outs, lses = [], []
    for window, dilation in DIL_PATTERNS:
        o_p, lse_p = dilated_window_attention(q, k, v, window, dilation)
        outs.append(o_p)
        lses.append(lse_p)
    w = jax.nn.softmax(jnp.stack(lses, axis=0), axis=0)
    return jnp.einsum('pbsh,pbshd->bshd', w.astype(q.dtype), jnp.stack(outs, axis=0))


def setup_inputs(seed: int = 0) -> dict:
    key = jax.random.key(seed)
    ks = jax.random.split(key, 24)

    def w(k, shape, fan_in):
        return jax.random.normal(k, shape, jnp.float32) * (fan_in ** -0.5)

    def gain(k, n):
        return 1.0 + 0.1 * jax.random.normal(k, (DEPTH, n), jnp.float32)

    x = jax.random.normal(ks[0], (BATCH, SEQ, D_MODEL), jnp.float32)
    offset = jax.random.randint(ks[1], (BATCH, 1), 0, 1024, dtype=jnp.int32)
    positions = (offset + jnp.arange(SEQ, dtype=jnp.int32)[None, :]).astype(jnp.int32)
    return {
        "x": x,
        "positions": positions,
        "ffn1_pre_g": gain(ks[2], D_MODEL),
        "ffn1_post_g": gain(ks[3], D_MODEL),
        "ffn1_w_gate": w(ks[4], (DEPTH, D_MODEL, D_FF), D_MODEL),
        "ffn1_w_up": w(ks[5], (DEPTH, D_MODEL, D_FF), D_MODEL),
        "ffn1_w_down": w(ks[6], (DEPTH, D_FF, D_MODEL), D_FF),
        "mix_pre_g": gain(ks[7], D_MODEL),
        "mix_post_g": gain(ks[8], D_MODEL),
        "w_in": w(ks[9], (DEPTH, D_MODEL, IN_COLS), D_MODEL),
        "mla_q_norm_g": gain(ks[10], MLA_Q_RANK),
        "mla_w_uq": w(ks[11], (DEPTH, MLA_Q_RANK, MLA_HEADS * MLA_QK_DIM), MLA_Q_RANK),
        "mla_kv_norm_g": gain(ks[12], MLA_KV_RANK),
        "mla_w_ukv": w(ks[13], (DEPTH, MLA_KV_RANK, MLA_HEADS * (MLA_NOPE_DIM + MLA_V_DIM)), MLA_KV_RANK),
        "w_o": w(ks[14], (DEPTH, D_MIX, D_MODEL), D_MIX),
        "ffn2_pre_g": gain(ks[15], D_MODEL),
        "ffn2_post_g": gain(ks[16], D_MODEL),
        "ffn2_w_gate": w(ks[17], (DEPTH, D_MODEL, D_FF), D_MODEL),
        "ffn2_w_up": w(ks[18], (DEPTH, D_MODEL, D_FF), D_MODEL),
        "ffn2_w_down": w(ks[19], (DEPTH, D_FF, D_MODEL), D_FF),
    }


def reference(x, positions, ffn1_pre_g, ffn1_post_g, ffn1_w_gate, ffn1_w_up, ffn1_w_down,
              mix_pre_g, mix_post_g, w_in, mla_q_norm_g, mla_w_uq, mla_kv_norm_g, mla_w_ukv,
              w_o, ffn2_pre_g, ffn2_post_g, ffn2_w_gate, ffn2_w_up, ffn2_w_down):
    B, S, _ = x.shape
    cos_a, sin_a = rope_tables(positions, MLA_ROPE_DIM)
    cos_p, sin_p = rope_tables(positions, PARTIAL_ROPE_DIM)

    def partial_rope(t):
        return jnp.concatenate([apply_rope(t[..., :PARTIAL_ROPE_DIM], cos_p[:, :, None], sin_p[:, :, None]),
                                t[..., PARTIAL_ROPE_DIM:]], axis=-1)

    for l in range(DEPTH):
        h = rms_norm(x, ffn1_pre_g[l])
        x = x + 0.5 * rms_norm(swiglu(h, ffn1_w_gate[l], ffn1_w_up[l], ffn1_w_down[l]), ffn1_post_g[l])

        h = rms_norm(x, mix_pre_g[l])
        proj = h @ w_in[l]
        c_q, c_kv, k_rope, q_d, k_d, v_d = jnp.split(proj, IN_SPLITS, axis=-1)

        q_a = (rms_norm(c_q, mla_q_norm_g[l]) @ mla_w_uq[l]).reshape(B, S, MLA_HEADS, MLA_QK_DIM)
        q_nope, q_rope = q_a[..., :MLA_NOPE_DIM], q_a[..., MLA_NOPE_DIM:]
        q_rope = apply_rope(q_rope, cos_a[:, :, None], sin_a[:, :, None])
        k_rope = apply_rope(k_rope, cos_a, sin_a)
        kv = (rms_norm(c_kv, mla_kv_norm_g[l]) @ mla_w_ukv[l]).reshape(B, S, MLA_HEADS, MLA_NOPE_DIM + MLA_V_DIM)
        k_nope, v_a = kv[..., :MLA_NOPE_DIM], kv[..., MLA_NOPE_DIM:]
        o_a = mla_attention(q_nope, q_rope, k_nope, k_rope, v_a)

        q_b = partial_rope(q_d.reshape(B, S, DIL_HEADS, HEAD_DIM))
        k_b = partial_rope(k_d.reshape(B, S, DIL_HEADS, HEAD_DIM))
        v_b = v_d.reshape(B, S, DIL_HEADS, HEAD_DIM)
        o_b = dilated_mixture(q_b, k_b, v_b)

        o = jnp.concatenate([o_a.reshape(B, S, MLA_WIDTH), o_b.reshape(B, S, DIL_WIDTH)], axis=-1) @ w_o[l]
        x = x + rms_norm(o, mix_post_g[l])

        h = rms_norm(x, ffn2_pre_g[l])
        x = x + 0.5 * rms_norm(swiglu(h, ffn2_w_gate[l], ffn2_w_up[l], ffn2_w_down[l]), ffn2_post_g[l])
    return x
```

```python
import functools
import math

import jax
import jax.numpy as jnp
import numpy as np
from jax import lax
from jax.experimental import pallas as pl
from jax.experimental.pallas import tpu as pltpu

F32 = jnp.float32
BF16 = jnp.bfloat16

HEAD_DIM = 128
N_HEADS = 8
MLA_RANK = 512
MLA_NOPE = 128
MLA_ROPE = 64
MLA_QK = MLA_NOPE + MLA_ROPE
MLA_QK_PAD = 256
PART_ROPE = 32
DIL_PATTERNS = ((128, 1), (512, 4), (2048, 16))
DIL_SPAN = 128
ROPE_THETA = 500000.0
RMS_EPS = 1e-6
NEG = -1e30
LANES = 128
VMEM_LIMIT = 56 * 1024 * 1024


def _rms(xf, g):
    ms = jnp.mean(xf * xf, axis=-1, keepdims=True)
    return xf * lax.rsqrt(ms + RMS_EPS) * g


def _rope(y, c, a, b, half):
    return y * c + pltpu.roll(y, LANES - half, axis=1) * a + pltpu.roll(y, half, axis=1) * b


def _params(*sem):
    return pltpu.CompilerParams(dimension_semantics=sem, vmem_limit_bytes=VMEM_LIMIT)


def _ffn_kernel(x_ref, gpre_ref, gpost_ref, wg_ref, wu_ref, wd_ref, o_ref, nb_ref):
    j = pl.program_id(1)

    @pl.when(j == 0)
    def _():
        nb_ref[...] = _rms(x_ref[...], gpre_ref[0]).astype(BF16)
        o_ref[...] = jnp.zeros_like(o_ref)

    nb = nb_ref[...]
    g = jnp.dot(nb, wg_ref[0], preferred_element_type=F32)
    u = jnp.dot(nb, wu_ref[0], preferred_element_type=F32)
    h = (g * jax.nn.sigmoid(g) * u).astype(BF16)
    o_ref[...] += jnp.dot(h, wd_ref[0], preferred_element_type=F32)

    @pl.when(j == pl.num_programs(1) - 1)
    def _():
        o_ref[...] = x_ref[...] + 0.5 * _rms(o_ref[...], gpost_ref[0])


def _ffn(x, gpre, gpost, wg, wu, wd, layer, *, tm=512, tn=512):
    s, d = x.shape
    dff = wg.shape[-1]
    return pl.pallas_call(
        _ffn_kernel,
        grid=(s // tm, dff // tn),
        in_specs=[
            pl.BlockSpec((tm, d), lambda m, j: (m, 0)),
            pl.BlockSpec((1, 1, d), lambda m, j: (layer, 0, 0)),
            pl.BlockSpec((1, 1, d), lambda m, j: (layer, 0, 0)),
            pl.BlockSpec((1, d, tn), lambda m, j: (layer, 0, j)),
            pl.BlockSpec((1, d, tn), lambda m, j: (layer, 0, j)),
            pl.BlockSpec((1, tn, d), lambda m, j: (layer, j, 0)),
        ],
        out_specs=pl.BlockSpec((tm, d), lambda m, j: (m, 0)),
        out_shape=jax.ShapeDtypeStruct((s, d), F32),
        scratch_shapes=[pltpu.VMEM((tm, d), BF16)],
        compiler_params=_params("parallel", "arbitrary"),
        name="ffn",
    )(x, gpre, gpost, wg, wu, wd)


def _dproj_kernel(x_ref, g_ref, w_ref, c_ref, a_ref, b_ref, o_ref, nb_ref):
    j = pl.program_id(1)

    @pl.when(j == 0)
    def _():
        nb_ref[...] = _rms(x_ref[...], g_ref[0]).astype(BF16)

    y = jnp.dot(nb_ref[...], w_ref[0], preferred_element_type=F32)

    @pl.when(j < 2)
    def _():
        scale = jnp.where(j == 0, 1.0 / math.sqrt(HEAD_DIM), 1.0).astype(F32)
        c = c_ref[...] * scale
        a = a_ref[...] * scale
        b = b_ref[...] * scale
        for h in range(N_HEADS):
            sl = slice(h * HEAD_DIM, (h + 1) * HEAD_DIM)
            o_ref[0, :, sl] = _rope(y[:, sl], c, a, b, PART_ROPE // 2).astype(BF16)

    @pl.when(j == 2)
    def _():
        o_ref[0] = y.astype(BF16)


def _dproj(x, g, wd, tabs, layer, *, tm=512):
    s, d = x.shape
    width = N_HEADS * HEAD_DIM
    tab_spec = pl.BlockSpec((tm, LANES), lambda m, j: (m, 0))
    return pl.pallas_call(
        _dproj_kernel,
        grid=(s // tm, 3),
        in_specs=[
            pl.BlockSpec((tm, d), lambda m, j: (m, 0)),
            pl.BlockSpec((1, 1, d), lambda m, j: (layer, 0, 0)),
            pl.BlockSpec((1, d, width), lambda m, j: (layer, 0, j)),
            tab_spec, tab_spec, tab_spec,
        ],
        out_specs=pl.BlockSpec((1, tm, width), lambda m, j: (j, m, 0)),
        out_shape=jax.ShapeDtypeStruct((3, s, width), BF16),
        scratch_shapes=[pltpu.VMEM((tm, d), BF16)],
        compiler_params=_params("parallel", "arbitrary"),
        name="dproj",
    )(x, g, wd, *tabs)


def _mproj_kernel(x_ref, g_ref, wc_ref, wkr_ref, gq_ref, wuq_ref, gkv_ref, wukv_ref,
                  c_ref, a_ref, b_ref, q_ref, k_ref, v_ref):
    nb = _rms(x_ref[...], g_ref[0]).astype(BF16)
    lat = jnp.dot(nb, wc_ref[0], preferred_element_type=F32)
    cq = _rms(lat[:, :MLA_RANK], gq_ref[0]).astype(BF16)
    ckv = _rms(lat[:, MLA_RANK:], gkv_ref[0]).astype(BF16)
    c, a, b = c_ref[...], a_ref[...], b_ref[...]
    half = MLA_ROPE // 2
    kr = jnp.dot(nb, wkr_ref[0], preferred_element_type=F32)
    kr = _rope(kr, c, a, b, half).astype(BF16)
    q = jnp.dot(cq, wuq_ref[0], preferred_element_type=F32)
    kv = jnp.dot(ckv, wukv_ref[0], preferred_element_type=F32)
    scale = 1.0 / math.sqrt(MLA_QK)
    for h in range(N_HEADS):
        lo = h * MLA_QK_PAD
        q_ref[h, :, :MLA_NOPE] = (q[:, lo:lo + MLA_NOPE] * scale).astype(BF16)
        qr = _rope(q[:, lo + MLA_NOPE:lo + MLA_QK_PAD], c, a, b, half)
        q_ref[h, :, MLA_NOPE:] = (qr * scale).astype(BF16)
        k_ref[h, :, :MLA_NOPE] = kv[:, lo:lo + MLA_NOPE].astype(BF16)
        k_ref[h, :, MLA_NOPE:] = kr
        v_ref[h] = kv[:, lo + MLA_NOPE:lo + MLA_QK_PAD].astype(BF16)


def _mproj(x, g, wc, wkr, gq, wuq, gkv, wukv, tabs, layer, *, tm=512):
    s, d = x.shape
    tab_spec = pl.BlockSpec((tm, LANES), lambda m: (m, 0))

    def wspec(w):
        return pl.BlockSpec((1,) + w.shape[1:], lambda m: (layer, 0, 0))

    def gspec(gv):
        return pl.BlockSpec((1, 1, gv.shape[2]), lambda m: (layer, 0, 0))

    return pl.pallas_call(
        _mproj_kernel,
        grid=(s // tm,),
        in_specs=[
            pl.BlockSpec((tm, d), lambda m: (m, 0)),
            gspec(g), wspec(wc), wspec(wkr), gspec(gq), wspec(wuq), gspec(gkv), wspec(wukv),
            tab_spec, tab_spec, tab_spec,
        ],
        out_specs=[
            pl.BlockSpec((N_HEADS, tm, MLA_QK_PAD), lambda m: (0, m, 0)),
            pl.BlockSpec((N_HEADS, tm, MLA_QK_PAD), lambda m: (0, m, 0)),
            pl.BlockSpec((N_HEADS, tm, HEAD_DIM), lambda m: (0, m, 0)),
        ],
        out_shape=[
            jax.ShapeDtypeStruct((N_HEADS, s, MLA_QK_PAD), BF16),
            jax.ShapeDtypeStruct((N_HEADS, s, MLA_QK_PAD), BF16),
            jax.ShapeDtypeStruct((N_HEADS, s, HEAD_DIM), BF16),
        ],
        compiler_params=_params("parallel"),
        name="mproj",
    )(x, g, wc, wkr, gq, wuq, gkv, wukv, *tabs)


def _flash_kernel(qi_ref, kj_ref, q_ref, k_ref, v_ref, o_ref, m_ref, l_ref, acc_ref, *, t):
    p = pl.program_id(1)
    qi = qi_ref[p]
    kj = kj_ref[p]

    @pl.when(kj == 0)
    def _():
        m_ref[...] = jnp.full_like(m_ref, NEG)
        l_ref[...] = jnp.zeros_like(l_ref)
        acc_ref[...] = jnp.zeros_like(acc_ref)

    def step(diagonal):
        s = lax.dot_general(q_ref[0], k_ref[0], (((1,), (1,)), ((), ())),
                            preferred_element_type=F32)
        if diagonal:
            rows = lax.broadcasted_iota(jnp.int32, (t, t), 0)
            cols = lax.broadcasted_iota(jnp.int32, (t, t), 1)
            s = jnp.where(cols <= rows, s, NEG)
        m_prev = m_ref[...]
        m_new = jnp.maximum(m_prev, jnp.max(s, axis=-1, keepdims=True))
        alpha = jnp.exp(m_prev - m_new)
        pr = jnp.exp(s - m_new)
        l_ref[...] = alpha * l_ref[...] + jnp.sum(pr, axis=-1, keepdims=True)
        acc_ref[...] = alpha * acc_ref[...] + jnp.dot(pr.astype(BF16), v_ref[0],
                                                      preferred_element_type=F32)
        m_ref[...] = m_new

    @pl.when(kj < qi)
    def _():
        step(False)

    @pl.when(kj == qi)
    def _():
        step(True)
        o_ref[...] = (acc_ref[...] / l_ref[...]).astype(BF16)


def _flash(q, k, v, *, t=512):
    _, s, _ = q.shape
    nb = s // t
    pairs = [(i, j) for i in range(nb) for j in range(i + 1)]
    qi = jnp.asarray(np.array([p[0] for p in pairs], np.int32))
    kj = jnp.asarray(np.array([p[1] for p in pairs], np.int32))
    grid_spec = pltpu.PrefetchScalarGridSpec(
        num_scalar_prefetch=2,
        grid=(N_HEADS, len(pairs)),
        in_specs=[
            pl.BlockSpec((1, t, MLA_QK_PAD), lambda h, p, qi, kj: (h, qi[p], 0)),
            pl.BlockSpec((1, t, MLA_QK_PAD), lambda h, p, qi, kj: (h, kj[p], 0)),
            pl.BlockSpec((1, t, HEAD_DIM), lambda h, p, qi, kj: (h, kj[p], 0)),
        ],
        out_specs=pl.BlockSpec((t, HEAD_DIM), lambda h, p, qi, kj: (qi[p], h)),
        scratch_shapes=[
            pltpu.VMEM((t, 1), F32),
            pltpu.VMEM((t, 1), F32),
            pltpu.VMEM((t, HEAD_DIM), F32),
        ],
    )
    return pl.pallas_call(
        functools.partial(_flash_kernel, t=t),
        grid_spec=grid_spec,
        out_shape=jax.ShapeDtypeStruct((s, N_HEADS * HEAD_DIM), BF16),
        compiler_params=_params("parallel", "arbitrary"),
        name="mla_flash",
    )(qi, kj, q, k, v)


def _dil_kernel(*refs, tq, has_prev, is_last):
    q_ref, kc_ref, kp_ref, vc_ref, vp_ref = refs[:5]
    refs = refs[5:]
    if has_prev:
        op_ref, lp_ref = refs[:2]
        refs = refs[2:]
    if is_last:
        o_ref, kbuf, vbuf = refs
    else:
        o_ref, lse_ref, kbuf, vbuf = refs
    blk = DIL_SPAN
    n = pl.program_id(1)

    kbuf[:blk] = kp_ref[0]
    kbuf[blk:] = kc_ref[0]
    vbuf[:blk] = vp_ref[0]
    vbuf[blk:] = vc_ref[0]

    qi = lax.broadcasted_iota(jnp.int32, (blk, 2 * blk), 0)
    kj = lax.broadcasted_iota(jnp.int32, (blk, 2 * blk), 1)
    diff = kj - qi
    band = (diff >= 0) & (diff <= DIL_SPAN)
    lane = lax.broadcasted_iota(jnp.int32, (blk, LANES), 1)

    def body(j, carry):
        r0 = pl.multiple_of(j * blk, blk)
        first = jnp.logical_and(n == 0, j == 0)
        lo = jnp.where(first, blk, 0)
        valid = band & (kj >= lo)
        lse_all = jnp.zeros((blk, LANES), F32)
        for h in range(N_HEADS):
            sl = slice(h * HEAD_DIM, (h + 1) * HEAD_DIM)
            q = q_ref[0, pl.ds(r0, blk), sl]
            k = kbuf[pl.ds(r0, 2 * blk), sl]
            v = vbuf[pl.ds(r0, 2 * blk), sl]
            s = lax.dot_general(q, k, (((1,), (1,)), ((), ())), preferred_element_type=F32)
            s = jnp.where(valid, s, NEG)
            m = jnp.max(s, axis=-1, keepdims=True)
            pr = jnp.exp(s - m)
            l = jnp.sum(pr, axis=-1, keepdims=True)
            o = jnp.dot(pr.astype(BF16), v, preferred_element_type=F32) / l
            lse = m + jnp.log(l)
            if has_prev:
                lp = lp_ref[pl.ds(r0, blk), h:h + 1]
                mx = jnp.maximum(lp, lse)
                lnew = mx + jnp.log(jnp.exp(lp - mx) + jnp.exp(lse - mx))
                o = op_ref[pl.ds(r0, blk), sl] * jnp.exp(lp - lnew) + o * jnp.exp(lse - lnew)
                lse = lnew
            o_ref[pl.ds(r0, blk), sl] = o.astype(o_ref.dtype)
            if not is_last:
                lse_all = jnp.where(lane == h, lse, lse_all)
        if not is_last:
            lse_ref[pl.ds(r0, blk), :] = lse_all
        return carry

    lax.fori_loop(0, tq // blk, body, 0)


def _dilated(qkv, dilation, prev, is_last, *, tq=512):
    _, s, width = qkv.shape
    d = dilation
    length = s // d
    tq = min(tq, length)
    ratio = tq // DIL_SPAN
    has_prev = prev is not None
    qkv_v = qkv.reshape(3, length, d * width)

    def cur(which):
        return pl.BlockSpec((1, tq, width), lambda r, n: (which, n, r))

    def prv(which):
        return pl.BlockSpec((1, DIL_SPAN, width),
                            lambda r, n: (which, jnp.maximum(n * ratio - 1, 0), r))

    in_specs = [cur(0), cur(1), prv(1), cur(2), prv(2)]
    args = [qkv_v, qkv_v, qkv_v, qkv_v, qkv_v]
    o_spec = pl.BlockSpec((tq, width), lambda r, n: (n, r))
    l_spec = pl.BlockSpec((tq, LANES), lambda r, n: (n, r))
    if has_prev:
        in_specs += [o_spec, l_spec]
        args += [prev[0].reshape(length, d * width), prev[1].reshape(length, d * LANES)]
    if is_last:
        out_specs = o_spec
        out_shape = jax.ShapeDtypeStruct((length, d * width), BF16)
    else:
        out_specs = [o_spec, l_spec]
        out_shape = [jax.ShapeDtypeStruct((length, d * width), F32),
                     jax.ShapeDtypeStruct((length, d * LANES), F32)]
    out = pl.pallas_call(
        functools.partial(_dil_kernel, tq=tq, has_prev=has_prev, is_last=is_last),
        grid=(d, length // tq),
        in_specs=in_specs,
        out_specs=out_specs,
        out_shape=out_shape,
        scratch_shapes=[pltpu.VMEM((tq + DIL_SPAN, width), BF16),
                        pltpu.VMEM((tq + DIL_SPAN, width), BF16)],
        compiler_params=_params("parallel", "arbitrary"),
        name=f"dilated_d{d}",
    )(*args)
    if is_last:
        return out.reshape(s, width)
    return out[0].reshape(s, width), out[1].reshape(s, LANES)


def _oproj_kernel(x_ref, oa_ref, ob_ref, w_ref, g_ref, o_ref):
    half = oa_ref.shape[1]
    y = jnp.dot(oa_ref[...], w_ref[0, :half], preferred_element_type=F32)
    y = y + jnp.dot(ob_ref[...], w_ref[0, half:], preferred_element_type=F32)
    o_ref[...] = x_ref[...] + _rms(y, g_ref[0])


def _oproj(x, oa, ob, wo, g, layer, *, tm=512):
    s, d = x.shape
    width = oa.shape[1]
    return pl.pallas_call(
        _oproj_kernel,
        grid=(s // tm,),
        in_specs=[
            pl.BlockSpec((tm, d), lambda m: (m, 0)),
            pl.BlockSpec((tm, width), lambda m: (m, 0)),
            pl.BlockSpec((tm, width), lambda m: (m, 0)),
            pl.BlockSpec((1,) + wo.shape[1:], lambda m: (layer, 0, 0)),
            pl.BlockSpec((1, 1, d), lambda m: (layer, 0, 0)),
        ],
        out_specs=pl.BlockSpec((tm, d), lambda m: (m, 0)),
        out_shape=jax.ShapeDtypeStruct((s, d), F32),
        compiler_params=_params("parallel"),
        name="oproj",
    )(x, oa, ob, wo, g)


def _rope_tables(pos, dim):
    half = dim // 2
    inv = ROPE_THETA ** (-jnp.arange(0, dim, 2, dtype=F32) / dim)
    ang = pos.astype(F32)[:, None] * inv
    cos, sin = jnp.cos(ang), jnp.sin(ang)
    n = pos.shape[0]
    c = jnp.concatenate([cos, cos, jnp.ones((n, LANES - dim), F32)], axis=1)
    a = jnp.concatenate([-sin, jnp.zeros((n, LANES - half), F32)], axis=1)
    b = jnp.concatenate([jnp.zeros((n, half), F32), sin, jnp.zeros((n, LANES - dim), F32)], axis=1)
    return c, a, b


def kernel(x, positions, ffn1_pre_g, ffn1_post_g, ffn1_w_gate, ffn1_w_up, ffn1_w_down,
           mix_pre_g, mix_post_g, w_in, mla_q_norm_g, mla_w_uq, mla_kv_norm_g, mla_w_ukv,
           w_o, ffn2_pre_g, ffn2_post_g, ffn2_w_gate, ffn2_w_up, ffn2_w_down):
    batch, _, _ = x.shape
    depth = w_in.shape[0]
    bf = lambda w: w.astype(BF16)

    lat_cols = 2 * MLA_RANK
    wc = bf(w_in[:, :, :lat_cols])
    wkr = bf(jnp.pad(w_in[:, :, lat_cols:lat_cols + MLA_ROPE], ((0, 0), (0, 0), (0, LANES - MLA_ROPE))))
    wd = bf(w_in[:, :, lat_cols + MLA_ROPE:])
    wuq = mla_w_uq.reshape(depth, MLA_RANK, N_HEADS, MLA_QK)
    wuq = bf(jnp.pad(wuq, ((0, 0), (0, 0), (0, 0), (0, MLA_QK_PAD - MLA_QK)))
             ).reshape(depth, MLA_RANK, N_HEADS * MLA_QK_PAD)
    wukv = bf(mla_w_ukv)
    wo = bf(w_o)
    f1 = (bf(ffn1_w_gate), bf(ffn1_w_up), bf(ffn1_w_down))
    f2 = (bf(ffn2_w_gate), bf(ffn2_w_up), bf(ffn2_w_down))
    row = lambda g: g[:, None, :]
    ffn1_pre_g, ffn1_post_g, ffn2_pre_g, ffn2_post_g = map(
        row, (ffn1_pre_g, ffn1_post_g, ffn2_pre_g, ffn2_post_g))
    mix_pre_g, mix_post_g, mla_q_norm_g, mla_kv_norm_g = map(
        row, (mix_pre_g, mix_post_g, mla_q_norm_g, mla_kv_norm_g))

    outs = []
    for bi in range(batch):
        xs = x[bi]
        tabs_a = _rope_tables(positions[bi], MLA_ROPE)
        tabs_p = _rope_tables(positions[bi], PART_ROPE)
        for l in range(depth):
            xs = _ffn(xs, ffn1_pre_g, ffn1_post_g, *f1, l)
            qkv = _dproj(xs, mix_pre_g, wd, tabs_p, l)
            qm, km, vm = _mproj(xs, mix_pre_g, wc, wkr, mla_q_norm_g, wuq,
                                mla_kv_norm_g, wukv, tabs_a, l)
            oa = _flash(qm, km, vm)
            prev = None
            for idx, (_, dil) in enumerate(DIL_PATTERNS):
                prev = _dilated(qkv, dil, prev, idx == len(DIL_PATTERNS) - 1)
            xs = _oproj(xs, oa, prev, wo, mix_post_g, l)
            xs = _ffn(xs, ffn2_pre_g, ffn2_post_g, *f2, l)
        outs.append(xs)
    return jnp.stack(outs, axis=0)
```

```python
import functools
import math

import jax
import jax.numpy as jnp
from jax import lax
from jax.experimental import pallas as pl
from jax.experimental.pallas import tpu as pltpu

F32 = jnp.float32
BF16 = jnp.bfloat16

HEAD_DIM = 128
N_HEADS = 8
MLA_RANK = 512
MLA_NOPE = 128
MLA_ROPE = 64
MLA_QK = MLA_NOPE + MLA_ROPE
MLA_QK_PAD = 256
PART_ROPE = 32
DIL_PATTERNS = ((128, 1), (512, 4), (2048, 16))
DIL_SPAN = 128
DIL_TILE = 2048
ROPE_THETA = 500000.0
RMS_EPS = 1e-6
NEG = -1e30
LANES = 128
VMEM_LIMIT = 56 * 1024 * 1024


def _rms(xf, g):
    ms = jnp.mean(xf * xf, axis=-1, keepdims=True)
    return xf * lax.rsqrt(ms + RMS_EPS) * g


def _rope(y, c, a, b, half):
    return y * c + pltpu.roll(y, LANES - half, axis=1) * a + pltpu.roll(y, half, axis=1) * b


def _params(*sem):
    return pltpu.CompilerParams(dimension_semantics=sem, vmem_limit_bytes=VMEM_LIMIT)


def _ffn_kernel(x_ref, gpre_ref, gpost_ref, wg_ref, wu_ref, wd_ref, o_ref, nb_ref):
    j = pl.program_id(1)

    @pl.when(j == 0)
    def _():
        nb_ref[...] = _rms(x_ref[...], gpre_ref[0]).astype(BF16)
        o_ref[...] = jnp.zeros_like(o_ref)

    nb = nb_ref[...]
    g = jnp.dot(nb, wg_ref[0], preferred_element_type=F32)
    u = jnp.dot(nb, wu_ref[0], preferred_element_type=F32)
    h = (g * jax.nn.sigmoid(g) * u).astype(BF16)
    o_ref[...] += jnp.dot(h, wd_ref[0], preferred_element_type=F32)

    @pl.when(j == pl.num_programs(1) - 1)
    def _():
        o_ref[...] = x_ref[...] + 0.5 * _rms(o_ref[...], gpost_ref[0])


def _ffn(x, gpre, gpost, wg, wu, wd, layer, *, tm=512, tn=512):
    s, d = x.shape
    dff = wg.shape[-1]
    return pl.pallas_call(
        _ffn_kernel,
        grid=(s // tm, dff // tn),
        in_specs=[
            pl.BlockSpec((tm, d), lambda m, j: (m, 0)),
            pl.BlockSpec((1, 1, d), lambda m, j: (layer, 0, 0)),
            pl.BlockSpec((1, 1, d), lambda m, j: (layer, 0, 0)),
            pl.BlockSpec((1, d, tn), lambda m, j: (layer, 0, j)),
            pl.BlockSpec((1, d, tn), lambda m, j: (layer, 0, j)),
            pl.BlockSpec((1, tn, d), lambda m, j: (layer, j, 0)),
        ],
        out_specs=pl.BlockSpec((tm, d), lambda m, j: (m, 0)),
        out_shape=jax.ShapeDtypeStruct((s, d), F32),
        scratch_shapes=[pltpu.VMEM((tm, d), BF16)],
        compiler_params=_params("parallel", "arbitrary"),
        name="ffn",
    )(x, gpre, gpost, wg, wu, wd)


def _dproj_kernel(x_ref, g_ref, w_ref, c_ref, a_ref, b_ref, o_ref, nb_ref):
    j = pl.program_id(1)

    @pl.when(j == 0)
    def _():
        nb_ref[...] = _rms(x_ref[...], g_ref[0]).astype(BF16)

    y = jnp.dot(nb_ref[...], w_ref[0], preferred_element_type=F32)

    @pl.when(j < 2)
    def _():
        scale = jnp.where(j == 0, 1.0 / math.sqrt(HEAD_DIM), 1.0).astype(F32)
        c = c_ref[...] * scale
        a = a_ref[...] * scale
        b = b_ref[...] * scale
        for h in range(N_HEADS):
            sl = slice(h * HEAD_DIM, (h + 1) * HEAD_DIM)
            o_ref[0, :, sl] = _rope(y[:, sl], c, a, b, PART_ROPE // 2).astype(BF16)

    @pl.when(j == 2)
    def _():
        o_ref[0] = y.astype(BF16)


def _dproj(x, g, wd, tabs, layer, *, tm=512):
    s, d = x.shape
    width = N_HEADS * HEAD_DIM
    tab_spec = pl.BlockSpec((tm, LANES), lambda m, j: (m, 0))
    return pl.pallas_call(
        _dproj_kernel,
        grid=(s // tm, 3),
        in_specs=[
            pl.BlockSpec((tm, d), lambda m, j: (m, 0)),
            pl.BlockSpec((1, 1, d), lambda m, j: (layer, 0, 0)),
            pl.BlockSpec((1, d, width), lambda m, j: (layer, 0, j)),
            tab_spec, tab_spec, tab_spec,
        ],
        out_specs=pl.BlockSpec((1, tm, width), lambda m, j: (j, m, 0)),
        out_shape=jax.ShapeDtypeStruct((3, s, width), BF16),
        scratch_shapes=[pltpu.VMEM((tm, d), BF16)],
        compiler_params=_params("parallel", "arbitrary"),
        name="dproj",
    )(x, g, wd, *tabs)


def _mproj_kernel(x_ref, g_ref, wc_ref, wkr_ref, gq_ref, wuq_ref, gkv_ref, wukv_ref,
                  c_ref, a_ref, b_ref, q_ref, k_ref, v_ref):
    nb = _rms(x_ref[...], g_ref[0]).astype(BF16)
    lat = jnp.dot(nb, wc_ref[0], preferred_element_type=F32)
    cq = _rms(lat[:, :MLA_RANK], gq_ref[0]).astype(BF16)
    ckv = _rms(lat[:, MLA_RANK:], gkv_ref[0]).astype(BF16)
    c, a, b = c_ref[...], a_ref[...], b_ref[...]
    half = MLA_ROPE // 2
    kr = jnp.dot(nb, wkr_ref[0], preferred_element_type=F32)
    kr = _rope(kr, c, a, b, half).astype(BF16)
    q = jnp.dot(cq, wuq_ref[0], preferred_element_type=F32)
    kv = jnp.dot(ckv, wukv_ref[0], preferred_element_type=F32)
    scale = math.log2(math.e) / math.sqrt(MLA_QK)
    ones = jnp.ones((x_ref.shape[0], HEAD_DIM), BF16)
    for h in range(N_HEADS):
        lo = h * MLA_QK_PAD
        q_ref[h, :, :MLA_NOPE] = (q[:, lo:lo + MLA_NOPE] * scale).astype(BF16)
        qr = _rope(q[:, lo + MLA_NOPE:lo + MLA_QK_PAD], c, a, b, half)
        q_ref[h, :, MLA_NOPE:] = (qr * scale).astype(BF16)
        k_ref[h, :, :MLA_NOPE] = kv[:, lo:lo + MLA_NOPE].astype(BF16)
        k_ref[h, :, MLA_NOPE:] = kr
        v_ref[h, :, :HEAD_DIM] = kv[:, lo + MLA_NOPE:lo + MLA_QK_PAD].astype(BF16)
        v_ref[h, :, HEAD_DIM:] = ones


def _mproj(x, g, wc, wkr, gq, wuq, gkv, wukv, tabs, layer, *, tm=512):
    s, d = x.shape
    tab_spec = pl.BlockSpec((tm, LANES), lambda m: (m, 0))

    def wspec(w):
        return pl.BlockSpec((1,) + w.shape[1:], lambda m: (layer, 0, 0))

    def gspec(gv):
        return pl.BlockSpec((1, 1, gv.shape[2]), lambda m: (layer, 0, 0))

    return pl.pallas_call(
        _mproj_kernel,
        grid=(s // tm,),
        in_specs=[
            pl.BlockSpec((tm, d), lambda m: (m, 0)),
            gspec(g), wspec(wc), wspec(wkr), gspec(gq), wspec(wuq), gspec(gkv), wspec(wukv),
            tab_spec, tab_spec, tab_spec,
        ],
        out_specs=[
            pl.BlockSpec((N_HEADS, tm, MLA_QK_PAD), lambda m: (0, m, 0)),
            pl.BlockSpec((N_HEADS, tm, MLA_QK_PAD), lambda m: (0, m, 0)),
            pl.BlockSpec((N_HEADS, tm, 2 * HEAD_DIM), lambda m: (0, m, 0)),
        ],
        out_shape=[
            jax.ShapeDtypeStruct((N_HEADS, s, MLA_QK_PAD), BF16),
            jax.ShapeDtypeStruct((N_HEADS, s, MLA_QK_PAD), BF16),
            jax.ShapeDtypeStruct((N_HEADS, s, 2 * HEAD_DIM), BF16),
        ],
        compiler_params=_params("parallel"),
        name="mproj",
    )(x, g, wc, wkr, gq, wuq, gkv, wukv, *tabs)


def _flash_kernel(q_ref, k_ref, v_ref, o_ref, m_ref, acc_ref, s_ref, *, tq):
    half = tq // 2
    i = pl.program_id(1)
    m_ref[...] = jnp.full_like(m_ref, NEG)
    acc_ref[...] = jnp.zeros_like(acc_ref)

    def scores(j, slot):
        kc = k_ref[0, pl.ds(pl.multiple_of(j * tq, tq), tq), :]
        for st in range(2):
            q = q_ref[0, st * half:(st + 1) * half, :]
            s_ref[slot, st] = lax.dot_general(q, kc, (((1,), (1,)), ((), ())),
                                              preferred_element_type=F32)

    def accumulate(j, slot, diagonal):
        vc = v_ref[0, pl.ds(pl.multiple_of(j * tq, tq), tq), :]
        for st in range(2):
            s = s_ref[slot, st]
            if diagonal:
                rows = lax.broadcasted_iota(jnp.int32, s.shape, 0)
                cols = lax.broadcasted_iota(jnp.int32, s.shape, 1)
                s = jnp.where(cols <= rows + st * half, s, NEG)
            m_prev = m_ref[st]
            m_new = jnp.maximum(m_prev, jnp.max(s, axis=-1, keepdims=True))
            alpha = jnp.exp2(m_prev - m_new)
            pr = jnp.exp2(s - m_new).astype(BF16)
            acc_ref[st] = alpha * acc_ref[st] + jnp.dot(pr, vc, preferred_element_type=F32)
            m_ref[st] = m_new

    scores(0, 0)

    def body(j, carry):
        slot = lax.rem(j, 2)
        accumulate(j, slot, False)
        scores(j + 1, 1 - slot)
        return carry

    lax.fori_loop(0, i, body, 0)
    accumulate(i, lax.rem(i, 2), True)
    for st in range(2):
        acc = acc_ref[st]
        o_ref[st * half:(st + 1) * half, :] = (acc[:, :HEAD_DIM] / acc[:, HEAD_DIM:]).astype(BF16)


def _flash(q, k, v, *, tq=1024):
    _, s, _ = q.shape
    return pl.pallas_call(
        functools.partial(_flash_kernel, tq=tq),
        grid=(N_HEADS, s // tq),
        in_specs=[
            pl.BlockSpec((1, tq, MLA_QK_PAD), lambda h, i: (h, i, 0)),
            pl.BlockSpec((1, s, MLA_QK_PAD), lambda h, i: (h, 0, 0)),
            pl.BlockSpec((1, s, 2 * HEAD_DIM), lambda h, i: (h, 0, 0)),
        ],
        out_specs=pl.BlockSpec((tq, HEAD_DIM), lambda h, i: (i, h)),
        out_shape=jax.ShapeDtypeStruct((s, N_HEADS * HEAD_DIM), BF16),
        scratch_shapes=[
            pltpu.VMEM((2, tq // 2, 1), F32),
            pltpu.VMEM((2, tq // 2, 2 * HEAD_DIM), F32),
            pltpu.VMEM((2, 2, tq // 2, tq), F32),
        ],
        compiler_params=_params("parallel", "arbitrary"),
        name="mla_flash",
    )(q, k, v)


def _dil_kernel(q_ref, kc_ref, kp_ref, vc_ref, vp_ref, o_ref, qf, kk, vv, o_sc, l_sc):
    t = q_ref.shape[1]
    blk = DIL_SPAN
    n = pl.program_id(0)
    qf[...] = q_ref[0].astype(F32)
    kk[:t] = kp_ref[0].astype(F32)
    kk[t:] = kc_ref[0].astype(F32)
    vv[:t] = vp_ref[0].astype(F32)
    vv[t:] = vc_ref[0].astype(F32)

    qi = lax.broadcasted_iota(jnp.int32, (blk, 2 * blk), 0)
    kj = lax.broadcasted_iota(jnp.int32, (blk, 2 * blk), 1)
    diff = kj - qi
    band = (diff >= 0) & (diff <= DIL_SPAN)

    for idx, (window, d) in enumerate(DIL_PATTERNS):
        seg = blk * d
        shift = d.bit_length() - 1

        def rows(ref, start, count, d=d):
            if d == 1:
                return ref[pl.ds(start, count), :]
            return ref[pl.ds(start, count, stride=d), :]

        def body(it, carry, idx=idx, d=d, seg=seg, shift=shift, rows=rows):
            r = jnp.bitwise_and(it, d - 1)
            b = jnp.right_shift(it, shift)
            start = b * seg + r
            first = jnp.logical_and(n == 0, b == 0)
            valid = band & (kj >= jnp.where(first, blk, 0))
            q = rows(qf, start, blk).astype(BF16)
            k = rows(kk, t + start - seg, 2 * blk).astype(BF16)
            v = rows(vv, t + start - seg, 2 * blk).astype(BF16)
            s = lax.dot_general(q, k, (((1,), (1,)), ((), ())), preferred_element_type=F32)
            s = jnp.where(valid, s, NEG)
            m = jnp.max(s, axis=-1, keepdims=True)
            pr = jnp.exp(s - m)
            l = jnp.sum(pr, axis=-1, keepdims=True)
            o = jnp.dot(pr.astype(BF16), v, preferred_element_type=F32) / l
            lse = jnp.broadcast_to(m + jnp.log(l), (blk, LANES))
            if d == 1:
                o_sc[idx, pl.ds(start, blk), :] = o
                l_sc[idx, pl.ds(start, blk), :] = lse
            else:
                o_sc[idx, pl.ds(start, blk, stride=d), :] = o
                l_sc[idx, pl.ds(start, blk, stride=d), :] = lse
            return carry

        lax.fori_loop(0, t // blk, body, 0, unroll=4)

    lses = [l_sc[p] for p in range(len(DIL_PATTERNS))]
    mx = functools.reduce(jnp.maximum, lses)
    ws = [jnp.exp(l - mx) for l in lses]
    num = sum(w * o_sc[p] for p, w in enumerate(ws))
    o_ref[...] = (num / sum(ws)).astype(BF16)


def _dilated(qkv):
    _, s, width = qkv.shape
    t = DIL_TILE
    npat = len(DIL_PATTERNS)

    def cur(which):
        return pl.BlockSpec((1, t, HEAD_DIM), lambda n, h: (which, n, h))

    def prv(which):
        return pl.BlockSpec((1, t, HEAD_DIM), lambda n, h: (which, jnp.maximum(n - 1, 0), h))

    return pl.pallas_call(
        _dil_kernel,
        grid=(s // t, N_HEADS),
        in_specs=[cur(0), cur(1), prv(1), cur(2), prv(2)],
        out_specs=pl.BlockSpec((t, HEAD_DIM), lambda n, h: (n, h)),
        out_shape=jax.ShapeDtypeStruct((s, width), BF16),
        scratch_shapes=[
            pltpu.VMEM((t, HEAD_DIM), F32),
            pltpu.VMEM((2 * t, HEAD_DIM), F32),
            pltpu.VMEM((2 * t, HEAD_DIM), F32),
            pltpu.VMEM((npat, t, HEAD_DIM), F32),
            pltpu.VMEM((npat, t, LANES), F32),
        ],
        compiler_params=_params("parallel", "parallel"),
        name="dilated",
    )(qkv, qkv, qkv, qkv, qkv)


def _oproj_kernel(x_ref, oa_ref, ob_ref, w_ref, g_ref, o_ref):
    half = oa_ref.shape[1]
    y = jnp.dot(oa_ref[...], w_ref[0, :half], preferred_element_type=F32)
    y = y + jnp.dot(ob_ref[...], w_ref[0, half:], preferred_element_type=F32)
    o_ref[...] = x_ref[...] + _rms(y, g_ref[0])


def _oproj(x, oa, ob, wo, g, layer, *, tm=512):
    s, d = x.shape
    width = oa.shape[1]
    return pl.pallas_call(
        _oproj_kernel,
        grid=(s // tm,),
        in_specs=[
            pl.BlockSpec((tm, d), lambda m: (m, 0)),
            pl.BlockSpec((tm, width), lambda m: (m, 0)),
            pl.BlockSpec((tm, width), lambda m: (m, 0)),
            pl.BlockSpec((1,) + wo.shape[1:], lambda m: (layer, 0, 0)),
            pl.BlockSpec((1, 1, d), lambda m: (layer, 0, 0)),
        ],
        out_specs=pl.BlockSpec((tm, d), lambda m: (m, 0)),
        out_shape=jax.ShapeDtypeStruct((s, d), F32),
        compiler_params=_params("parallel"),
        name="oproj",
    )(x, oa, ob, wo, g)


def _rope_tables(pos, dim):
    half = dim // 2
    inv = ROPE_THETA ** (-jnp.arange(0, dim, 2, dtype=F32) / dim)
    ang = pos.astype(F32)[:, None] * inv
    cos, sin = jnp.cos(ang), jnp.sin(ang)
    n = pos.shape[0]
    c = jnp.concatenate([cos, cos, jnp.ones((n, LANES - dim), F32)], axis=1)
    a = jnp.concatenate([-sin, jnp.zeros((n, LANES - half), F32)], axis=1)
    b = jnp.concatenate([jnp.zeros((n, half), F32), sin, jnp.zeros((n, LANES - dim), F32)], axis=1)
    return c, a, b


def kernel(x, positions, ffn1_pre_g, ffn1_post_g, ffn1_w_gate, ffn1_w_up, ffn1_w_down,
           mix_pre_g, mix_post_g, w_in, mla_q_norm_g, mla_w_uq, mla_kv_norm_g, mla_w_ukv,
           w_o, ffn2_pre_g, ffn2_post_g, ffn2_w_gate, ffn2_w_up, ffn2_w_down):
    batch, _, _ = x.shape
    depth = w_in.shape[0]
    bf = lambda w: w.astype(BF16)

    lat_cols = 2 * MLA_RANK
    wc = bf(w_in[:, :, :lat_cols])
    wkr = bf(jnp.pad(w_in[:, :, lat_cols:lat_cols + MLA_ROPE], ((0, 0), (0, 0), (0, LANES - MLA_ROPE))))
    wd = bf(w_in[:, :, lat_cols + MLA_ROPE:])
    wuq = mla_w_uq.reshape(depth, MLA_RANK, N_HEADS, MLA_QK)
    wuq = bf(jnp.pad(wuq, ((0, 0), (0, 0), (0, 0), (0, MLA_QK_PAD - MLA_QK)))
             ).reshape(depth, MLA_RANK, N_HEADS * MLA_QK_PAD)
    wukv = bf(mla_w_ukv)
    wo = bf(w_o)
    f1 = (bf(ffn1_w_gate), bf(ffn1_w_up), bf(ffn1_w_down))
    f2 = (bf(ffn2_w_gate), bf(ffn2_w_up), bf(ffn2_w_down))
    row = lambda g: g[:, None, :]
    ffn1_pre_g, ffn1_post_g, ffn2_pre_g, ffn2_post_g = map(
        row, (ffn1_pre_g, ffn1_post_g, ffn2_pre_g, ffn2_post_g))
    mix_pre_g, mix_post_g, mla_q_norm_g, mla_kv_norm_g = map(
        row, (mix_pre_g, mix_post_g, mla_q_norm_g, mla_kv_norm_g))

    outs = []
    for bi in range(batch):
        xs = x[bi]
        tabs_a = _rope_tables(positions[bi], MLA_ROPE)
        tabs_p = _rope_tables(positions[bi], PART_ROPE)
        for l in range(depth):
            xs = _ffn(xs, ffn1_pre_g, ffn1_post_g, *f1, l)
            qkv = _dproj(xs, mix_pre_g, wd, tabs_p, l)
            qm, km, vm = _mproj(xs, mix_pre_g, wc, wkr, mla_q_norm_g, wuq,
                                mla_kv_norm_g, wukv, tabs_a, l)
            oa = _flash(qm, km, vm)
            ob = _dilated(qkv)
            xs = _oproj(xs, oa, ob, wo, mix_post_g, l)
            xs = _ffn(xs, ffn2_pre_g, ffn2_post_g, *f2, l)
        outs.append(xs)
    return jnp.stack(outs, axis=0)
```

```python
import functools
import math

import jax
import jax.numpy as jnp
from jax import lax
from jax.experimental import pallas as pl
from jax.experimental.pallas import tpu as pltpu

F32 = jnp.float32
BF16 = jnp.bfloat16

HEAD_DIM = 128
N_HEADS = 8
MLA_RANK = 512
MLA_NOPE = 128
MLA_ROPE = 64
MLA_QK = MLA_NOPE + MLA_ROPE
MLA_QK_PAD = 256
PART_ROPE = 32
DIL_PATTERNS = ((128, 1), (512, 4), (2048, 16))
DIL_SPAN = 128
DIL_TILE = 2048
ROPE_THETA = 500000.0
RMS_EPS = 1e-6
NEG = -1e30
LANES = 128
VMEM_LIMIT = 56 * 1024 * 1024


def _rms(xf, g):
    ms = jnp.mean(xf * xf, axis=-1, keepdims=True)
    return xf * lax.rsqrt(ms + RMS_EPS) * g


def _rope(y, c, a, b, half):
    return y * c + pltpu.roll(y, LANES - half, axis=1) * a + pltpu.roll(y, half, axis=1) * b


def _params(*sem):
    return pltpu.CompilerParams(dimension_semantics=sem, vmem_limit_bytes=VMEM_LIMIT)


def _ffn_kernel(x_ref, gpre_ref, gpost_ref, wg_ref, wu_ref, wd_ref, o_ref, nb_ref):
    j = pl.program_id(1)

    @pl.when(j == 0)
    def _():
        nb_ref[...] = _rms(x_ref[...], gpre_ref[0]).astype(BF16)
        o_ref[...] = jnp.zeros_like(o_ref)

    nb = nb_ref[...]
    g = jnp.dot(nb, wg_ref[0], preferred_element_type=F32)
    u = jnp.dot(nb, wu_ref[0], preferred_element_type=F32)
    h = (g * jax.nn.sigmoid(g) * u).astype(BF16)
    o_ref[...] += jnp.dot(h, wd_ref[0], preferred_element_type=F32)

    @pl.when(j == pl.num_programs(1) - 1)
    def _():
        o_ref[...] = x_ref[...] + 0.5 * _rms(o_ref[...], gpost_ref[0])


def _ffn(x, gpre, gpost, wg, wu, wd, layer, *, tm=512, tn=512):
    s, d = x.shape
    dff = wg.shape[-1]
    return pl.pallas_call(
        _ffn_kernel,
        grid=(s // tm, dff // tn),
        in_specs=[
            pl.BlockSpec((tm, d), lambda m, j: (m, 0)),
            pl.BlockSpec((1, 1, d), lambda m, j: (layer, 0, 0)),
            pl.BlockSpec((1, 1, d), lambda m, j: (layer, 0, 0)),
            pl.BlockSpec((1, d, tn), lambda m, j: (layer, 0, j)),
            pl.BlockSpec((1, d, tn), lambda m, j: (layer, 0, j)),
            pl.BlockSpec((1, tn, d), lambda m, j: (layer, j, 0)),
        ],
        out_specs=pl.BlockSpec((tm, d), lambda m, j: (m, 0)),
        out_shape=jax.ShapeDtypeStruct((s, d), F32),
        scratch_shapes=[pltpu.VMEM((tm, d), BF16)],
        compiler_params=_params("parallel", "arbitrary"),
        name="ffn",
    )(x, gpre, gpost, wg, wu, wd)


def _dproj_kernel(x_ref, g_ref, w_ref, c_ref, a_ref, b_ref, o_ref, nb_ref):
    j = pl.program_id(1)

    @pl.when(j == 0)
    def _():
        nb_ref[...] = _rms(x_ref[...], g_ref[0]).astype(BF16)

    y = jnp.dot(nb_ref[...], w_ref[0], preferred_element_type=F32)

    @pl.when(j < 2)
    def _():
        scale = jnp.where(j == 0, 1.0 / math.sqrt(HEAD_DIM), 1.0).astype(F32)
        c = c_ref[...] * scale
        a = a_ref[...] * scale
        b = b_ref[...] * scale
        for h in range(N_HEADS):
            sl = slice(h * HEAD_DIM, (h + 1) * HEAD_DIM)
            o_ref[0, :, sl] = _rope(y[:, sl], c, a, b, PART_ROPE // 2).astype(BF16)

    @pl.when(j == 2)
    def _():
        o_ref[0] = y.astype(BF16)


def _dproj(x, g, wd, tabs, layer, *, tm=512):
    s, d = x.shape
    width = N_HEADS * HEAD_DIM
    tab_spec = pl.BlockSpec((tm, LANES), lambda m, j: (m, 0))
    return pl.pallas_call(
        _dproj_kernel,
        grid=(s // tm, 3),
        in_specs=[
            pl.BlockSpec((tm, d), lambda m, j: (m, 0)),
            pl.BlockSpec((1, 1, d), lambda m, j: (layer, 0, 0)),
            pl.BlockSpec((1, d, width), lambda m, j: (layer, 0, j)),
            tab_spec, tab_spec, tab_spec,
        ],
        out_specs=pl.BlockSpec((1, tm, width), lambda m, j: (j, m, 0)),
        out_shape=jax.ShapeDtypeStruct((3, s, width), BF16),
        scratch_shapes=[pltpu.VMEM((tm, d), BF16)],
        compiler_params=_params("parallel", "arbitrary"),
        name="dproj",
    )(x, g, wd, *tabs)


def _mproj_kernel(x_ref, g_ref, wc_ref, wkr_ref, gq_ref, wuq_ref, gkv_ref, wukv_ref,
                  c_ref, a_ref, b_ref, q_ref, k_ref, v_ref):
    nb = _rms(x_ref[...], g_ref[0]).astype(BF16)
    lat = jnp.dot(nb, wc_ref[0], preferred_element_type=F32)
    cq = _rms(lat[:, :MLA_RANK], gq_ref[0]).astype(BF16)
    ckv = _rms(lat[:, MLA_RANK:], gkv_ref[0]).astype(BF16)
    c, a, b = c_ref[...], a_ref[...], b_ref[...]
    half = MLA_ROPE // 2
    kr = jnp.dot(nb, wkr_ref[0], preferred_element_type=F32)
    kr = _rope(kr, c, a, b, half).astype(BF16)
    q = jnp.dot(cq, wuq_ref[0], preferred_element_type=F32)
    kv = jnp.dot(ckv, wukv_ref[0], preferred_element_type=F32)
    scale = math.log2(math.e) / math.sqrt(MLA_QK)
    ones = jnp.ones((x_ref.shape[0], HEAD_DIM), BF16)
    for h in range(N_HEADS):
        lo = h * MLA_QK_PAD
        q_ref[h, :, :MLA_NOPE] = (q[:, lo:lo + MLA_NOPE] * scale).astype(BF16)
        qr = _rope(q[:, lo + MLA_NOPE:lo + MLA_QK_PAD], c, a, b, half)
        q_ref[h, :, MLA_NOPE:] = (qr * scale).astype(BF16)
        k_ref[h, :, :MLA_NOPE] = kv[:, lo:lo + MLA_NOPE].astype(BF16)
        k_ref[h, :, MLA_NOPE:] = kr
        v_ref[h, :, :HEAD_DIM] = kv[:, lo + MLA_NOPE:lo + MLA_QK_PAD].astype(BF16)
        v_ref[h, :, HEAD_DIM:] = ones


def _mproj(x, g, wc, wkr, gq, wuq, gkv, wukv, tabs, layer, *, tm=512):
    s, d = x.shape
    tab_spec = pl.BlockSpec((tm, LANES), lambda m: (m, 0))

    def wspec(w):
        return pl.BlockSpec((1,) + w.shape[1:], lambda m: (layer, 0, 0))

    def gspec(gv):
        return pl.BlockSpec((1, 1, gv.shape[2]), lambda m: (layer, 0, 0))

    return pl.pallas_call(
        _mproj_kernel,
        grid=(s // tm,),
        in_specs=[
            pl.BlockSpec((tm, d), lambda m: (m, 0)),
            gspec(g), wspec(wc), wspec(wkr), gspec(gq), wspec(wuq), gspec(gkv), wspec(wukv),
            tab_spec, tab_spec, tab_spec,
        ],
        out_specs=[
            pl.BlockSpec((N_HEADS, tm, MLA_QK_PAD), lambda m: (0, m, 0)),
            pl.BlockSpec((N_HEADS, tm, MLA_QK_PAD), lambda m: (0, m, 0)),
            pl.BlockSpec((N_HEADS, tm, 2 * HEAD_DIM), lambda m: (0, m, 0)),
        ],
        out_shape=[
            jax.ShapeDtypeStruct((N_HEADS, s, MLA_QK_PAD), BF16),
            jax.ShapeDtypeStruct((N_HEADS, s, MLA_QK_PAD), BF16),
            jax.ShapeDtypeStruct((N_HEADS, s, 2 * HEAD_DIM), BF16),
        ],
        compiler_params=_params("parallel"),
        name="mproj",
    )(x, g, wc, wkr, gq, wuq, gkv, wukv, *tabs)


def _flash_kernel(q_ref, k_ref, v_ref, o_ref, m_ref, acc_ref, sa_ref, sb_ref, *, tq):
    half = tq // 2
    i = pl.program_id(1)
    m_ref[...] = jnp.full_like(m_ref, NEG)
    acc_ref[...] = jnp.zeros_like(acc_ref)

    def chunk_rows(c):
        return pl.ds(pl.multiple_of(c * half, half), half)

    def scores(c, s_ref, streams=(0, 1)):
        kc = k_ref[0, chunk_rows(c), :]
        for st in streams:
            q = q_ref[0, st * half:(st + 1) * half, :]
            s_ref[st] = lax.dot_general(q, kc, (((1,), (1,)), ((), ())),
                                        preferred_element_type=F32)

    def accumulate(c, s_ref, streams=(0, 1), masked=()):
        vc = v_ref[0, chunk_rows(c), :]
        for st in streams:
            s = s_ref[st]
            if st in masked:
                rows = lax.broadcasted_iota(jnp.int32, s.shape, 0)
                cols = lax.broadcasted_iota(jnp.int32, s.shape, 1)
                s = jnp.where(cols <= rows, s, NEG)
            m_prev = m_ref[st]
            m_new = jnp.maximum(m_prev, jnp.max(s, axis=-1, keepdims=True))
            alpha = jnp.exp2(m_prev - m_new)
            pr = jnp.exp2(s - m_new).astype(BF16)
            acc_ref[st] = alpha * acc_ref[st] + jnp.dot(pr, vc, preferred_element_type=F32)
            m_ref[st] = m_new

    scores(0, sa_ref)

    def body(jj, carry):
        c = 2 * jj
        scores(c + 1, sb_ref)
        accumulate(c, sa_ref)
        scores(c + 2, sa_ref)
        accumulate(c + 1, sb_ref)
        return carry

    lax.fori_loop(0, i, body, 0)
    scores(2 * i + 1, sb_ref, streams=(1,))
    accumulate(2 * i, sa_ref, masked=(0,))
    accumulate(2 * i + 1, sb_ref, streams=(1,), masked=(1,))
    for st in range(2):
        acc = acc_ref[st]
        o_ref[st * half:(st + 1) * half, :] = (acc[:, :HEAD_DIM] / acc[:, HEAD_DIM:]).astype(BF16)


def _flash(q, k, v, *, tq=1024):
    _, s, _ = q.shape
    return pl.pallas_call(
        functools.partial(_flash_kernel, tq=tq),
        grid=(N_HEADS, s // tq),
        in_specs=[
            pl.BlockSpec((1, tq, MLA_QK_PAD), lambda h, i: (h, i, 0)),
            pl.BlockSpec((1, s, MLA_QK_PAD), lambda h, i: (h, 0, 0)),
            pl.BlockSpec((1, s, 2 * HEAD_DIM), lambda h, i: (h, 0, 0)),
        ],
        out_specs=pl.BlockSpec((tq, HEAD_DIM), lambda h, i: (i, h)),
        out_shape=jax.ShapeDtypeStruct((s, N_HEADS * HEAD_DIM), BF16),
        scratch_shapes=[
            pltpu.VMEM((2, tq // 2, 1), F32),
            pltpu.VMEM((2, tq // 2, 2 * HEAD_DIM), F32),
            pltpu.VMEM((2, tq // 2, tq // 2), F32),
            pltpu.VMEM((2, tq // 2, tq // 2), F32),
        ],
        compiler_params=_params("parallel", "arbitrary"),
        name="mla_flash",
    )(q, k, v)


def _dil_kernel(q_ref, kc_ref, kp_ref, vc_ref, vp_ref, o_ref, qf, kk, vv, o_sc, l_sc):
    t = q_ref.shape[1]
    blk = DIL_SPAN
    n = pl.program_id(0)
    qf[...] = q_ref[0].astype(F32)
    kk[:t] = kp_ref[0].astype(F32)
    kk[t:] = kc_ref[0].astype(F32)
    vv[:t] = vp_ref[0].astype(F32)
    vv[t:] = vc_ref[0].astype(F32)

    qi = lax.broadcasted_iota(jnp.int32, (blk, 2 * blk), 0)
    kj = lax.broadcasted_iota(jnp.int32, (blk, 2 * blk), 1)
    diff = kj - qi
    band = (diff >= 0) & (diff <= DIL_SPAN)

    for idx, (window, d) in enumerate(DIL_PATTERNS):
        seg = blk * d
        shift = d.bit_length() - 1

        def rows(ref, start, count, d=d):
            if d == 1:
                return ref[pl.ds(start, count), :]
            return ref[pl.ds(start, count, stride=d), :]

        def body(it, carry, idx=idx, d=d, seg=seg, shift=shift, rows=rows):
            r = jnp.bitwise_and(it, d - 1)
            b = jnp.right_shift(it, shift)
            start = b * seg + r
            first = jnp.logical_and(n == 0, b == 0)
            valid = band & (kj >= jnp.where(first, blk, 0))
            q = rows(qf, start, blk).astype(BF16)
            k = rows(kk, t + start - seg, 2 * blk).astype(BF16)
            v = rows(vv, t + start - seg, 2 * blk).astype(BF16)
            s = lax.dot_general(q, k, (((1,), (1,)), ((), ())), preferred_element_type=F32)
            s = jnp.where(valid, s, NEG)
            m = jnp.max(s, axis=-1, keepdims=True)
            pr = jnp.exp(s - m)
            l = jnp.sum(pr, axis=-1, keepdims=True)
            o = jnp.dot(pr.astype(BF16), v, preferred_element_type=F32) / l
            lse = jnp.broadcast_to(m + jnp.log(l), (blk, LANES))
            if d == 1:
                o_sc[idx, pl.ds(start, blk), :] = o
                l_sc[idx, pl.ds(start, blk), :] = lse
            else:
                o_sc[idx, pl.ds(start, blk, stride=d), :] = o
                l_sc[idx, pl.ds(start, blk, stride=d), :] = lse
            return carry

        lax.fori_loop(0, t // blk, body, 0, unroll=8)

    lses = [l_sc[p] for p in range(len(DIL_PATTERNS))]
    mx = functools.reduce(jnp.maximum, lses)
    ws = [jnp.exp(l - mx) for l in lses]
    num = sum(w * o_sc[p] for p, w in enumerate(ws))
    o_ref[...] = (num / sum(ws)).astype(BF16)


def _dilated(qkv):
    _, s, width = qkv.shape
    t = DIL_TILE
    npat = len(DIL_PATTERNS)

    def cur(which):
        return pl.BlockSpec((1, t, HEAD_DIM), lambda n, h: (which, n, h))

    def prv(which):
        return pl.BlockSpec((1, t, HEAD_DIM), lambda n, h: (which, jnp.maximum(n - 1, 0), h))

    return pl.pallas_call(
        _dil_kernel,
        grid=(s // t, N_HEADS),
        in_specs=[cur(0), cur(1), prv(1), cur(2), prv(2)],
        out_specs=pl.BlockSpec((t, HEAD_DIM), lambda n, h: (n, h)),
        out_shape=jax.ShapeDtypeStruct((s, width), BF16),
        scratch_shapes=[
            pltpu.VMEM((t, HEAD_DIM), F32),
            pltpu.VMEM((2 * t, HEAD_DIM), F32),
            pltpu.VMEM((2 * t, HEAD_DIM), F32),
            pltpu.VMEM((npat, t, HEAD_DIM), F32),
            pltpu.VMEM((npat, t, LANES), F32),
        ],
        compiler_params=_params("parallel", "parallel"),
        name="dilated",
    )(qkv, qkv, qkv, qkv, qkv)


def _oproj_kernel(x_ref, oa_ref, ob_ref, w_ref, g_ref, o_ref):
    half = oa_ref.shape[1]
    y = jnp.dot(oa_ref[...], w_ref[0, :half], preferred_element_type=F32)
    y = y + jnp.dot(ob_ref[...], w_ref[0, half:], preferred_element_type=F32)
    o_ref[...] = x_ref[...] + _rms(y, g_ref[0])


def _oproj(x, oa, ob, wo, g, layer, *, tm=512):
    s, d = x.shape
    width = oa.shape[1]
    return pl.pallas_call(
        _oproj_kernel,
        grid=(s // tm,),
        in_specs=[
            pl.BlockSpec((tm, d), lambda m: (m, 0)),
            pl.BlockSpec((tm, width), lambda m: (m, 0)),
            pl.BlockSpec((tm, width), lambda m: (m, 0)),
            pl.BlockSpec((1,) + wo.shape[1:], lambda m: (layer, 0, 0)),
            pl.BlockSpec((1, 1, d), lambda m: (layer, 0, 0)),
        ],
        out_specs=pl.BlockSpec((tm, d), lambda m: (m, 0)),
        out_shape=jax.ShapeDtypeStruct((s, d), F32),
        compiler_params=_params("parallel"),
        name="oproj",
    )(x, oa, ob, wo, g)


def _rope_tables(pos, dim):
    half = dim // 2
    inv = ROPE_THETA ** (-jnp.arange(0, dim, 2, dtype=F32) / dim)
    ang = pos.astype(F32)[:, None] * inv
    cos, sin = jnp.cos(ang), jnp.sin(ang)
    n = pos.shape[0]
    c = jnp.concatenate([cos, cos, jnp.ones((n, LANES - dim), F32)], axis=1)
    a = jnp.concatenate([-sin, jnp.zeros((n, LANES - half), F32)], axis=1)
    b = jnp.concatenate([jnp.zeros((n, half), F32), sin, jnp.zeros((n, LANES - dim), F32)], axis=1)
    return c, a, b


def kernel(x, positions, ffn1_pre_g, ffn1_post_g, ffn1_w_gate, ffn1_w_up, ffn1_w_down,
           mix_pre_g, mix_post_g, w_in, mla_q_norm_g, mla_w_uq, mla_kv_norm_g, mla_w_ukv,
           w_o, ffn2_pre_g, ffn2_post_g, ffn2_w_gate, ffn2_w_up, ffn2_w_down):
    batch, _, _ = x.shape
    depth = w_in.shape[0]
    bf = lambda w: w.astype(BF16)

    lat_cols = 2 * MLA_RANK
    wc = bf(w_in[:, :, :lat_cols])
    wkr = bf(jnp.pad(w_in[:, :, lat_cols:lat_cols + MLA_ROPE], ((0, 0), (0, 0), (0, LANES - MLA_ROPE))))
    wd = bf(w_in[:, :, lat_cols + MLA_ROPE:])
    wuq = mla_w_uq.reshape(depth, MLA_RANK, N_HEADS, MLA_QK)
    wuq = bf(jnp.pad(wuq, ((0, 0), (0, 0), (0, 0), (0, MLA_QK_PAD - MLA_QK)))
             ).reshape(depth, MLA_RANK, N_HEADS * MLA_QK_PAD)
    wukv = bf(mla_w_ukv)
    wo = bf(w_o)
    f1 = (bf(ffn1_w_gate), bf(ffn1_w_up), bf(ffn1_w_down))
    f2 = (bf(ffn2_w_gate), bf(ffn2_w_up), bf(ffn2_w_down))
    row = lambda g: g[:, None, :]
    ffn1_pre_g, ffn1_post_g, ffn2_pre_g, ffn2_post_g = map(
        row, (ffn1_pre_g, ffn1_post_g, ffn2_pre_g, ffn2_post_g))
    mix_pre_g, mix_post_g, mla_q_norm_g, mla_kv_norm_g = map(
        row, (mix_pre_g, mix_post_g, mla_q_norm_g, mla_kv_norm_g))

    outs = []
    for bi in range(batch):
        xs = x[bi]
        tabs_a = _rope_tables(positions[bi], MLA_ROPE)
        tabs_p = _rope_tables(positions[bi], PART_ROPE)
        for l in range(depth):
            xs = _ffn(xs, ffn1_pre_g, ffn1_post_g, *f1, l)
            qkv = _dproj(xs, mix_pre_g, wd, tabs_p, l)
            qm, km, vm = _mproj(xs, mix_pre_g, wc, wkr, mla_q_norm_g, wuq,
                                mla_kv_norm_g, wukv, tabs_a, l)
            oa = _flash(qm, km, vm)
            ob = _dilated(qkv)
            xs = _oproj(xs, oa, ob, wo, mix_post_g, l)
            xs = _ffn(xs, ffn2_pre_g, ffn2_post_g, *f2, l)
        outs.append(xs)
    return jnp.stack(outs, axis=0)
```

```python
import functools
import math

import jax
import jax.numpy as jnp
from jax import lax
from jax.experimental import pallas as pl
from jax.experimental.pallas import tpu as pltpu

F32 = jnp.float32
BF16 = jnp.bfloat16

HEAD_DIM = 128
N_HEADS = 8
MLA_RANK = 512
MLA_NOPE = 128
MLA_ROPE = 64
MLA_QK = MLA_NOPE + MLA_ROPE
MLA_QK_PAD = 256
PART_ROPE = 32
DIL_PATTERNS = ((128, 1), (512, 4), (2048, 16))
DIL_SPAN = 128
DIL_TILE = 2048
ROPE_THETA = 500000.0
RMS_EPS = 1e-6
NEG = -1e30
LANES = 128
VMEM_LIMIT = 56 * 1024 * 1024
FFN_NORM_ROWS = 256
FFN_DOT_ROWS = 512


def _rms(xf, g):
    ms = jnp.mean(xf * xf, axis=-1, keepdims=True)
    return xf * lax.rsqrt(ms + RMS_EPS) * g


def _rope(y, c, t):
    return y * c + pltpu.roll(y, LANES // 2, axis=1) * t


def _params(*sem):
    return pltpu.CompilerParams(dimension_semantics=sem, vmem_limit_bytes=VMEM_LIMIT)


def _ffn_kernel(x_ref, gpre_ref, gpost_ref, wg_ref, wu_ref, wd_ref, o_ref, nb_ref):
    j = pl.program_id(1)

    tm = x_ref.shape[0]

    def row_chunks(size):
        return [slice(r, r + size) for r in range(0, tm, size)]

    @pl.when(j == 0)
    def _():
        for rows in row_chunks(FFN_NORM_ROWS):
            nb_ref[rows] = _rms(x_ref[rows], gpre_ref[0]).astype(BF16)
        o_ref[...] = jnp.zeros_like(o_ref)

    for rows in row_chunks(FFN_DOT_ROWS):
        nb = nb_ref[rows]
        g = jnp.dot(nb, wg_ref[0], preferred_element_type=F32)
        u = jnp.dot(nb, wu_ref[0], preferred_element_type=F32)
        h = (g * jax.nn.sigmoid(g) * u).astype(BF16)
        o_ref[rows] += jnp.dot(h, wd_ref[0], preferred_element_type=F32)

    @pl.when(j == pl.num_programs(1) - 1)
    def _():
        for rows in row_chunks(FFN_NORM_ROWS):
            o_ref[rows] = x_ref[rows] + 0.5 * _rms(o_ref[rows], gpost_ref[0])


def _ffn(x, gpre, gpost, wg, wu, wd, layer, *, tm=1024, tn=512):
    s, d = x.shape
    dff = wg.shape[-1]
    return pl.pallas_call(
        _ffn_kernel,
        grid=(s // tm, dff // tn),
        in_specs=[
            pl.BlockSpec((tm, d), lambda m, j: (m, 0)),
            pl.BlockSpec((1, 1, d), lambda m, j: (layer, 0, 0)),
            pl.BlockSpec((1, 1, d), lambda m, j: (layer, 0, 0)),
            pl.BlockSpec((1, d, tn), lambda m, j: (layer, 0, j)),
            pl.BlockSpec((1, d, tn), lambda m, j: (layer, 0, j)),
            pl.BlockSpec((1, tn, d), lambda m, j: (layer, j, 0)),
        ],
        out_specs=pl.BlockSpec((tm, d), lambda m, j: (m, 0)),
        out_shape=jax.ShapeDtypeStruct((s, d), F32),
        scratch_shapes=[pltpu.VMEM((tm, d), BF16)],
        compiler_params=_params("parallel", "arbitrary"),
        name="ffn",
    )(x, gpre, gpost, wg, wu, wd)


def _dproj_kernel(x_ref, g_ref, w_ref, c_ref, t_ref, o_ref, nb_ref):
    j = pl.program_id(1)

    @pl.when(j == 0)
    def _():
        for r in range(0, x_ref.shape[0], FFN_NORM_ROWS):
            rows = slice(r, r + FFN_NORM_ROWS)
            nb_ref[rows] = _rms(x_ref[rows], g_ref[0]).astype(BF16)

    y = jnp.dot(nb_ref[...], w_ref[0], preferred_element_type=F32)
    c = c_ref[0]
    t = t_ref[0]
    for h in range(N_HEADS):
        sl = slice(h * HEAD_DIM, (h + 1) * HEAD_DIM)
        o_ref[0, :, sl] = _rope(y[:, sl], c, t).astype(BF16)


def _dproj(x, g, wd, tabs, layer, *, tm=1024):
    s, d = x.shape
    width = N_HEADS * HEAD_DIM
    tab_spec = pl.BlockSpec((1, tm, LANES), lambda m, j: (j, m, 0))
    return pl.pallas_call(
        _dproj_kernel,
        grid=(s // tm, 3),
        in_specs=[
            pl.BlockSpec((tm, d), lambda m, j: (m, 0)),
            pl.BlockSpec((1, 1, d), lambda m, j: (layer, 0, 0)),
            pl.BlockSpec((1, d, width), lambda m, j: (layer, 0, j)),
            tab_spec, tab_spec,
        ],
        out_specs=pl.BlockSpec((1, tm, width), lambda m, j: (j, m, 0)),
        out_shape=jax.ShapeDtypeStruct((3, s, width), BF16),
        scratch_shapes=[pltpu.VMEM((tm, d), BF16)],
        compiler_params=_params("parallel", "arbitrary"),
        name="dproj",
    )(x, g, wd, *tabs)


def _mproj_kernel(x_ref, g_ref, wc_ref, wkr_ref, gq_ref, wuq_ref, gkv_ref, wukv_ref,
                  c_ref, t_ref, q_ref, k_ref, v_ref):
    nb = _rms(x_ref[...], g_ref[0]).astype(BF16)
    lat = jnp.dot(nb, wc_ref[0], preferred_element_type=F32)
    cq = _rms(lat[:, :MLA_RANK], gq_ref[0]).astype(BF16)
    ckv = _rms(lat[:, MLA_RANK:], gkv_ref[0]).astype(BF16)
    c, t = c_ref[...], t_ref[...]
    kr = jnp.dot(nb, wkr_ref[0], preferred_element_type=F32)
    kr = _rope(kr, c, t).astype(BF16)
    q = jnp.dot(cq, wuq_ref[0], preferred_element_type=F32)
    kv = jnp.dot(ckv, wukv_ref[0], preferred_element_type=F32)
    scale = math.log2(math.e) / math.sqrt(MLA_QK)
    ones = jnp.ones((x_ref.shape[0], HEAD_DIM), BF16)
    for h in range(N_HEADS):
        lo = h * MLA_QK_PAD
        q_ref[h, :, :MLA_NOPE] = (q[:, lo:lo + MLA_NOPE] * scale).astype(BF16)
        qr = _rope(q[:, lo + MLA_NOPE:lo + MLA_QK_PAD], c, t)
        q_ref[h, :, MLA_NOPE:] = (qr * scale).astype(BF16)
        k_ref[h, :, :MLA_NOPE] = kv[:, lo:lo + MLA_NOPE].astype(BF16)
        k_ref[h, :, MLA_NOPE:] = kr
        v_ref[h, :, :HEAD_DIM] = kv[:, lo + MLA_NOPE:lo + MLA_QK_PAD].astype(BF16)
        v_ref[h, :, HEAD_DIM:] = ones


def _mproj(x, g, wc, wkr, gq, wuq, gkv, wukv, tabs, layer, *, tm=512):
    s, d = x.shape
    tab_spec = pl.BlockSpec((tm, LANES), lambda m: (m, 0))

    def wspec(w):
        return pl.BlockSpec((1,) + w.shape[1:], lambda m: (layer, 0, 0))

    def gspec(gv):
        return pl.BlockSpec((1, 1, gv.shape[2]), lambda m: (layer, 0, 0))

    return pl.pallas_call(
        _mproj_kernel,
        grid=(s // tm,),
        in_specs=[
            pl.BlockSpec((tm, d), lambda m: (m, 0)),
            gspec(g), wspec(wc), wspec(wkr), gspec(gq), wspec(wuq), gspec(gkv), wspec(wukv),
            tab_spec, tab_spec,
        ],
        out_specs=[
            pl.BlockSpec((N_HEADS, tm, MLA_QK_PAD), lambda m: (0, m, 0)),
            pl.BlockSpec((N_HEADS, tm, MLA_QK_PAD), lambda m: (0, m, 0)),
            pl.BlockSpec((N_HEADS, tm, 2 * HEAD_DIM), lambda m: (0, m, 0)),
        ],
        out_shape=[
            jax.ShapeDtypeStruct((N_HEADS, s, MLA_QK_PAD), BF16),
            jax.ShapeDtypeStruct((N_HEADS, s, MLA_QK_PAD), BF16),
            jax.ShapeDtypeStruct((N_HEADS, s, 2 * HEAD_DIM), BF16),
        ],
        compiler_params=_params("parallel"),
        name="mproj",
    )(x, g, wc, wkr, gq, wuq, gkv, wukv, *tabs)


def _flash_kernel(q_ref, k_ref, v_ref, o_ref, m_ref, acc_ref, sa_ref, sb_ref, *, tq):
    half = tq // 2
    i = pl.program_id(1)
    m_ref[...] = jnp.full_like(m_ref, NEG)
    acc_ref[...] = jnp.zeros_like(acc_ref)

    def chunk_rows(c):
        return pl.ds(pl.multiple_of(c * half, half), half)

    def scores(c, s_ref, streams=(0, 1)):
        kc = k_ref[0, chunk_rows(c), :]
        for st in streams:
            q = q_ref[0, st * half:(st + 1) * half, :]
            s_ref[st] = lax.dot_general(q, kc, (((1,), (1,)), ((), ())),
                                        preferred_element_type=F32)

    def accumulate(c, s_ref, streams=(0, 1), masked=()):
        vc = v_ref[0, chunk_rows(c), :]
        for st in streams:
            s = s_ref[st]
            if st in masked:
                rows = lax.broadcasted_iota(jnp.int32, s.shape, 0)
                cols = lax.broadcasted_iota(jnp.int32, s.shape, 1)
                s = jnp.where(cols <= rows, s, NEG)
            m_prev = m_ref[st]
            m_new = jnp.maximum(m_prev, jnp.max(s, axis=-1, keepdims=True))
            alpha = jnp.exp2(m_prev - m_new)
            pr = jnp.exp2(s - m_new).astype(BF16)
            acc_ref[st] = alpha * acc_ref[st] + jnp.dot(pr, vc, preferred_element_type=F32)
            m_ref[st] = m_new

    scores(0, sa_ref)

    def body(jj, carry):
        c = 2 * jj
        scores(c + 1, sb_ref)
        accumulate(c, sa_ref)
        scores(c + 2, sa_ref)
        accumulate(c + 1, sb_ref)
        return carry

    lax.fori_loop(0, i, body, 0)
    scores(2 * i + 1, sb_ref, streams=(1,))
    accumulate(2 * i, sa_ref, masked=(0,))
    accumulate(2 * i + 1, sb_ref, streams=(1,), masked=(1,))
    for st in range(2):
        acc = acc_ref[st]
        o_ref[st * half:(st + 1) * half, :] = (acc[:, :HEAD_DIM] / acc[:, HEAD_DIM:]).astype(BF16)


def _flash(q, k, v, *, tq=1024):
    _, s, _ = q.shape
    return pl.pallas_call(
        functools.partial(_flash_kernel, tq=tq),
        grid=(N_HEADS, s // tq),
        in_specs=[
            pl.BlockSpec((1, tq, MLA_QK_PAD), lambda h, i: (h, i, 0)),
            pl.BlockSpec((1, s, MLA_QK_PAD), lambda h, i: (h, 0, 0)),
            pl.BlockSpec((1, s, 2 * HEAD_DIM), lambda h, i: (h, 0, 0)),
        ],
        out_specs=pl.BlockSpec((tq, HEAD_DIM), lambda h, i: (i, h)),
        out_shape=jax.ShapeDtypeStruct((s, N_HEADS * HEAD_DIM), BF16),
        scratch_shapes=[
            pltpu.VMEM((2, tq // 2, 1), F32),
            pltpu.VMEM((2, tq // 2, 2 * HEAD_DIM), F32),
            pltpu.VMEM((2, tq // 2, tq // 2), F32),
            pltpu.VMEM((2, tq // 2, tq // 2), F32),
        ],
        compiler_params=_params("parallel", "arbitrary"),
        name="mla_flash",
    )(q, k, v)


def _dil_kernel(q_ref, kc_ref, kp_ref, vc_ref, vp_ref, o_ref, qf, kk, vv, o_sc, l_sc):
    t = q_ref.shape[1]
    blk = DIL_SPAN
    n = pl.program_id(0)
    qf[...] = q_ref[0].astype(F32)
    kk[:t] = kp_ref[0].astype(F32)
    kk[t:] = kc_ref[0].astype(F32)
    vv[:t] = vp_ref[0].astype(F32)
    vv[t:] = vc_ref[0].astype(F32)

    qi = lax.broadcasted_iota(jnp.int32, (blk, 2 * blk), 0)
    kj = lax.broadcasted_iota(jnp.int32, (blk, 2 * blk), 1)
    diff = kj - qi
    band = (diff >= 0) & (diff <= DIL_SPAN)

    for idx, (window, d) in enumerate(DIL_PATTERNS):
        seg = blk * d
        shift = d.bit_length() - 1

        def rows(ref, start, count, d=d):
            if d == 1:
                return ref[pl.ds(start, count), :]
            return ref[pl.ds(start, count, stride=d), :]

        def body(it, carry, idx=idx, d=d, seg=seg, shift=shift, rows=rows):
            r = jnp.bitwise_and(it, d - 1)
            b = jnp.right_shift(it, shift)
            start = b * seg + r
            first = jnp.logical_and(n == 0, b == 0)
            valid = band & (kj >= jnp.where(first, blk, 0))
            q = rows(qf, start, blk).astype(BF16)
            k = rows(kk, t + start - seg, 2 * blk).astype(BF16)
            v = rows(vv, t + start - seg, 2 * blk).astype(BF16)
            s = lax.dot_general(q, k, (((1,), (1,)), ((), ())), preferred_element_type=F32)
            s = jnp.where(valid, s, NEG)
            m = jnp.max(s, axis=-1, keepdims=True)
            pr = jnp.exp(s - m)
            l = jnp.sum(pr, axis=-1, keepdims=True)
            o = jnp.dot(pr.astype(BF16), v, preferred_element_type=F32) / l
            lse = jnp.broadcast_to(m + jnp.log(l), (blk, LANES))
            if d == 1:
                o_sc[idx, pl.ds(start, blk), :] = o
                l_sc[idx, pl.ds(start, blk), :] = lse
            else:
                o_sc[idx, pl.ds(start, blk, stride=d), :] = o
                l_sc[idx, pl.ds(start, blk, stride=d), :] = lse
            return carry

        lax.fori_loop(0, t // blk, body, 0, unroll=16)

    lses = [l_sc[p] for p in range(len(DIL_PATTERNS))]
    mx = functools.reduce(jnp.maximum, lses)
    ws = [jnp.exp(l - mx) for l in lses]
    num = sum(w * o_sc[p] for p, w in enumerate(ws))
    o_ref[...] = (num / sum(ws)).astype(BF16)


def _dilated(qkv):
    _, s, width = qkv.shape
    t = DIL_TILE
    npat = len(DIL_PATTERNS)

    def cur(which):
        return pl.BlockSpec((1, t, HEAD_DIM), lambda n, h: (which, n, h))

    def prv(which):
        return pl.BlockSpec((1, t, HEAD_DIM), lambda n, h: (which, jnp.maximum(n - 1, 0), h))

    return pl.pallas_call(
        _dil_kernel,
        grid=(s // t, N_HEADS),
        in_specs=[cur(0), cur(1), prv(1), cur(2), prv(2)],
        out_specs=pl.BlockSpec((t, HEAD_DIM), lambda n, h: (n, h)),
        out_shape=jax.ShapeDtypeStruct((s, width), BF16),
        scratch_shapes=[
            pltpu.VMEM((t, HEAD_DIM), F32),
            pltpu.VMEM((2 * t, HEAD_DIM), F32),
            pltpu.VMEM((2 * t, HEAD_DIM), F32),
            pltpu.VMEM((npat, t, HEAD_DIM), F32),
            pltpu.VMEM((npat, t, LANES), F32),
        ],
        compiler_params=_params("parallel", "parallel"),
        name="dilated",
    )(qkv, qkv, qkv, qkv, qkv)


def _oproj_kernel(x_ref, oa_ref, ob_ref, w_ref, g_ref, o_ref):
    half = oa_ref.shape[1]
    y = jnp.dot(oa_ref[...], w_ref[0, :half], preferred_element_type=F32)
    y = y + jnp.dot(ob_ref[...], w_ref[0, half:], preferred_element_type=F32)
    o_ref[...] = x_ref[...] + _rms(y, g_ref[0])


def _oproj(x, oa, ob, wo, g, layer, *, tm=512):
    s, d = x.shape
    width = oa.shape[1]
    return pl.pallas_call(
        _oproj_kernel,
        grid=(s // tm,),
        in_specs=[
            pl.BlockSpec((tm, d), lambda m: (m, 0)),
            pl.BlockSpec((tm, width), lambda m: (m, 0)),
            pl.BlockSpec((tm, width), lambda m: (m, 0)),
            pl.BlockSpec((1,) + wo.shape[1:], lambda m: (layer, 0, 0)),
            pl.BlockSpec((1, 1, d), lambda m: (layer, 0, 0)),
        ],
        out_specs=pl.BlockSpec((tm, d), lambda m: (m, 0)),
        out_shape=jax.ShapeDtypeStruct((s, d), F32),
        compiler_params=_params("parallel"),
        name="oproj",
    )(x, oa, ob, wo, g)


def _rope_tables(pos, dim):
    half = dim // 2
    inv = ROPE_THETA ** (-jnp.arange(0, dim, 2, dtype=F32) / dim)
    ang = pos.astype(F32)[:, None] * inv
    cos, sin = jnp.cos(ang), jnp.sin(ang)
    n = pos.shape[0]
    one = jnp.ones((n, LANES // 2 - half), F32)
    zero = jnp.zeros((n, LANES // 2 - half), F32)
    c = jnp.concatenate([cos, one, cos, one], axis=1)
    t = jnp.concatenate([-sin, zero, sin, zero], axis=1)
    return c, t


def _spread_rope_cols(w, dim):
    half = dim // 2
    n = w.shape[-1]
    first, second = w[..., :half], w[..., half:dim]
    rest = jnp.concatenate([w[..., dim:], jnp.zeros(w.shape[:-1] + (LANES - n,), w.dtype)], axis=-1)
    split = LANES // 2 - half
    return jnp.concatenate([first, rest[..., :split], second, rest[..., split:]], axis=-1)


def kernel(x, positions, ffn1_pre_g, ffn1_post_g, ffn1_w_gate, ffn1_w_up, ffn1_w_down,
           mix_pre_g, mix_post_g, w_in, mla_q_norm_g, mla_w_uq, mla_kv_norm_g, mla_w_ukv,
           w_o, ffn2_pre_g, ffn2_post_g, ffn2_w_gate, ffn2_w_up, ffn2_w_down):
    batch, _, _ = x.shape
    depth = w_in.shape[0]
    bf = lambda w: w.astype(BF16)

    d_model = w_in.shape[1]
    width = N_HEADS * HEAD_DIM
    lat_cols = 2 * MLA_RANK
    w_in = bf(w_in)
    wc = w_in[:, :, :lat_cols]
    wkr = _spread_rope_cols(w_in[:, :, lat_cols:lat_cols + MLA_ROPE], MLA_ROPE)
    wd = w_in[:, :, lat_cols + MLA_ROPE:].reshape(depth, d_model, 3, N_HEADS, HEAD_DIM)
    wd = jnp.concatenate([_spread_rope_cols(wd[:, :, :2], PART_ROPE), wd[:, :, 2:]], axis=2)
    wd = wd.reshape(depth, d_model, 3 * width)
    wuq = bf(mla_w_uq).reshape(depth, MLA_RANK, N_HEADS, MLA_QK)
    wuq = jnp.concatenate([wuq[..., :MLA_NOPE], _spread_rope_cols(wuq[..., MLA_NOPE:], MLA_ROPE)],
                          axis=-1).reshape(depth, MLA_RANK, N_HEADS * MLA_QK_PAD)
    wukv = bf(mla_w_ukv)
    wo = bf(w_o)
    f1 = (bf(ffn1_w_gate), bf(ffn1_w_up), bf(ffn1_w_down))
    f2 = (bf(ffn2_w_gate), bf(ffn2_w_up), bf(ffn2_w_down))
    row = lambda g: g[:, None, :]
    ffn1_pre_g, ffn1_post_g, ffn2_pre_g, ffn2_post_g = map(
        row, (ffn1_pre_g, ffn1_post_g, ffn2_pre_g, ffn2_post_g))
    mix_pre_g, mix_post_g, mla_q_norm_g, mla_kv_norm_g = map(
        row, (mix_pre_g, mix_post_g, mla_q_norm_g, mla_kv_norm_g))

    outs = []
    for bi in range(batch):
        xs = x[bi]
        tabs_a = _rope_tables(positions[bi], MLA_ROPE)
        cp, tp = _rope_tables(positions[bi], PART_ROPE)
        qscale = 1.0 / math.sqrt(HEAD_DIM)
        tabs_p = (jnp.stack([cp * qscale, cp, jnp.ones_like(cp)]),
                  jnp.stack([tp * qscale, tp, jnp.zeros_like(tp)]))
        for l in range(depth):
            xs = _ffn(xs, ffn1_pre_g, ffn1_post_g, *f1, l)
            qkv = _dproj(xs, mix_pre_g, wd, tabs_p, l)
            qm, km, vm = _mproj(xs, mix_pre_g, wc, wkr, mla_q_norm_g, wuq,
                                mla_kv_norm_g, wukv, tabs_a, l)
            oa = _flash(qm, km, vm)
            ob = _dilated(qkv)
            xs = _oproj(xs, oa, ob, wo, mix_post_g, l)
            xs = _ffn(xs, ffn2_pre_g, ffn2_post_g, *f2, l)
        outs.append(xs)
    return jnp.stack(outs, axis=0)
```

```python
import functools
import math

import jax
import jax.numpy as jnp
from jax import lax
from jax.experimental import pallas as pl
from jax.experimental.pallas import tpu as pltpu

F32 = jnp.float32
BF16 = jnp.bfloat16

HEAD_DIM = 128
N_HEADS = 8
MLA_RANK = 512
MLA_NOPE = 128
MLA_ROPE = 64
MLA_QK = MLA_NOPE + MLA_ROPE
MLA_QK_PAD = 256
PART_ROPE = 32
DIL_PATTERNS = ((128, 1), (512, 4), (2048, 16))
DIL_SPAN = 128
DIL_TILE = 2048
ROPE_THETA = 500000.0
RMS_EPS = 1e-6
NEG = -1e30
LANES = 128
VMEM_LIMIT = 56 * 1024 * 1024
FFN_NORM_ROWS = 256
FFN_DOT_ROWS = 512


def _rms(xf, g):
    ms = jnp.mean(xf * xf, axis=-1, keepdims=True)
    return xf * lax.rsqrt(ms + RMS_EPS) * g


def _rope(y, c, a, b, half):
    return y * c + pltpu.roll(y, LANES - half, axis=1) * a + pltpu.roll(y, half, axis=1) * b


def _params(*sem):
    return pltpu.CompilerParams(dimension_semantics=sem, vmem_limit_bytes=VMEM_LIMIT)


def _ffn_kernel(x_ref, gpre_ref, gpost_ref, wg_ref, wu_ref, wd_ref, o_ref, nb_ref):
    j = pl.program_id(1)

    tm = x_ref.shape[0]

    def row_chunks(size):
        return [slice(r, r + size) for r in range(0, tm, size)]

    @pl.when(j == 0)
    def _():
        for rows in row_chunks(FFN_NORM_ROWS):
            nb_ref[rows] = _rms(x_ref[rows], gpre_ref[0]).astype(BF16)
        o_ref[...] = jnp.zeros_like(o_ref)

    for rows in row_chunks(FFN_DOT_ROWS):
        nb = nb_ref[rows]
        g = jnp.dot(nb, wg_ref[0], preferred_element_type=F32)
        u = jnp.dot(nb, wu_ref[0], preferred_element_type=F32)
        h = (g * jax.nn.sigmoid(g) * u).astype(BF16)
        o_ref[rows] += jnp.dot(h, wd_ref[0], preferred_element_type=F32)

    @pl.when(j == pl.num_programs(1) - 1)
    def _():
        for rows in row_chunks(FFN_NORM_ROWS):
            o_ref[rows] = x_ref[rows] + 0.5 * _rms(o_ref[rows], gpost_ref[0])


def _ffn(x, gpre, gpost, wg, wu, wd, layer, *, tm=1024, tn=512):
    s, d = x.shape
    dff = wg.shape[-1]
    return pl.pallas_call(
        _ffn_kernel,
        grid=(s // tm, dff // tn),
        in_specs=[
            pl.BlockSpec((tm, d), lambda m, j: (m, 0)),
            pl.BlockSpec((1, 1, d), lambda m, j: (layer, 0, 0)),
            pl.BlockSpec((1, 1, d), lambda m, j: (layer, 0, 0)),
            pl.BlockSpec((1, d, tn), lambda m, j: (layer, 0, j)),
            pl.BlockSpec((1, d, tn), lambda m, j: (layer, 0, j)),
            pl.BlockSpec((1, tn, d), lambda m, j: (layer, j, 0)),
        ],
        out_specs=pl.BlockSpec((tm, d), lambda m, j: (m, 0)),
        out_shape=jax.ShapeDtypeStruct((s, d), F32),
        scratch_shapes=[pltpu.VMEM((tm, d), BF16)],
        compiler_params=_params("parallel", "arbitrary"),
        name="ffn",
    )(x, gpre, gpost, wg, wu, wd)


def _dproj_kernel(x_ref, g_ref, w_ref, c_ref, a_ref, b_ref, o_ref, nb_ref):
    j = pl.program_id(1)

    @pl.when(j == 0)
    def _():
        for r in range(0, x_ref.shape[0], FFN_NORM_ROWS):
            rows = slice(r, r + FFN_NORM_ROWS)
            nb_ref[rows] = _rms(x_ref[rows], g_ref[0]).astype(BF16)

    y = jnp.dot(nb_ref[...], w_ref[0], preferred_element_type=F32)
    c, a, b = c_ref[0], a_ref[0], b_ref[0]
    for h in range(N_HEADS):
        sl = slice(h * HEAD_DIM, (h + 1) * HEAD_DIM)
        o_ref[0, :, sl] = _rope(y[:, sl], c, a, b, PART_ROPE // 2).astype(BF16)


def _dproj(x, g, wd, tabs, layer, *, tm=1024):
    s, d = x.shape
    width = N_HEADS * HEAD_DIM
    tab_spec = pl.BlockSpec((1, tm, LANES), lambda m, j: (j, m, 0))
    return pl.pallas_call(
        _dproj_kernel,
        grid=(s // tm, 3),
        in_specs=[
            pl.BlockSpec((tm, d), lambda m, j: (m, 0)),
            pl.BlockSpec((1, 1, d), lambda m, j: (layer, 0, 0)),
            pl.BlockSpec((1, d, width), lambda m, j: (layer, 0, j)),
            tab_spec, tab_spec, tab_spec,
        ],
        out_specs=pl.BlockSpec((1, tm, width), lambda m, j: (j, m, 0)),
        out_shape=jax.ShapeDtypeStruct((3, s, width), BF16),
        scratch_shapes=[pltpu.VMEM((tm, d), BF16)],
        compiler_params=_params("parallel", "arbitrary"),
        name="dproj",
    )(x, g, wd, *tabs)


def _mproj_kernel(x_ref, g_ref, wc_ref, wkr_ref, gq_ref, wuq_ref, gkv_ref, wukv_ref,
                  c_ref, a_ref, b_ref, q_ref, k_ref, v_ref):
    nb = _rms(x_ref[...], g_ref[0]).astype(BF16)
    lat = jnp.dot(nb, wc_ref[0], preferred_element_type=F32)
    cq = _rms(lat[:, :MLA_RANK], gq_ref[0]).astype(BF16)
    ckv = _rms(lat[:, MLA_RANK:], gkv_ref[0]).astype(BF16)
    c, a, b = c_ref[...], a_ref[...], b_ref[...]
    half = MLA_ROPE // 2
    kr = jnp.dot(nb, wkr_ref[0], preferred_element_type=F32)
    kr = _rope(kr, c, a, b, half).astype(BF16)
    q = jnp.dot(cq, wuq_ref[0], preferred_element_type=F32)
    kv = jnp.dot(ckv, wukv_ref[0], preferred_element_type=F32)
    scale = math.log2(math.e) / math.sqrt(MLA_QK)
    ones = jnp.ones((x_ref.shape[0], HEAD_DIM), BF16)
    for h in range(N_HEADS):
        lo = h * MLA_QK_PAD
        q_ref[h, :, :MLA_NOPE] = (q[:, lo:lo + MLA_NOPE] * scale).astype(BF16)
        qr = _rope(q[:, lo + MLA_NOPE:lo + MLA_QK_PAD], c, a, b, half)
        q_ref[h, :, MLA_NOPE:] = (qr * scale).astype(BF16)
        k_ref[h, :, :MLA_NOPE] = kv[:, lo:lo + MLA_NOPE].astype(BF16)
        k_ref[h, :, MLA_NOPE:] = kr
        v_ref[h, :, :HEAD_DIM] = kv[:, lo + MLA_NOPE:lo + MLA_QK_PAD].astype(BF16)
        v_ref[h, :, HEAD_DIM:] = ones


def _mproj(x, g, wc, wkr, gq, wuq, gkv, wukv, tabs, layer, *, tm=512):
    s, d = x.shape
    tab_spec = pl.BlockSpec((tm, LANES), lambda m: (m, 0))

    def wspec(w):
        return pl.BlockSpec((1,) + w.shape[1:], lambda m: (layer, 0, 0))

    def gspec(gv):
        return pl.BlockSpec((1, 1, gv.shape[2]), lambda m: (layer, 0, 0))

    return pl.pallas_call(
        _mproj_kernel,
        grid=(s // tm,),
        in_specs=[
            pl.BlockSpec((tm, d), lambda m: (m, 0)),
            gspec(g), wspec(wc), wspec(wkr), gspec(gq), wspec(wuq), gspec(gkv), wspec(wukv),
            tab_spec, tab_spec, tab_spec,
        ],
        out_specs=[
            pl.BlockSpec((N_HEADS, tm, MLA_QK_PAD), lambda m: (0, m, 0)),
            pl.BlockSpec((N_HEADS, tm, MLA_QK_PAD), lambda m: (0, m, 0)),
            pl.BlockSpec((N_HEADS, tm, 2 * HEAD_DIM), lambda m: (0, m, 0)),
        ],
        out_shape=[
            jax.ShapeDtypeStruct((N_HEADS, s, MLA_QK_PAD), BF16),
            jax.ShapeDtypeStruct((N_HEADS, s, MLA_QK_PAD), BF16),
            jax.ShapeDtypeStruct((N_HEADS, s, 2 * HEAD_DIM), BF16),
        ],
        compiler_params=_params("parallel"),
        name="mproj",
    )(x, g, wc, wkr, gq, wuq, gkv, wukv, *tabs)


def _flash_kernel(q_ref, k_ref, v_ref, o_ref, m_ref, acc_ref,
                  sa_ref, sb_ref, pa_ref, pb_ref, *, tq):
    half = tq // 2
    i = pl.program_id(1)
    m_ref[...] = jnp.full_like(m_ref, NEG)
    acc_ref[...] = jnp.zeros_like(acc_ref)

    def key_rows(c, n=tq):
        return pl.ds(pl.multiple_of(c * tq, tq), n)

    def scores(c, s_ref, p_ref):
        kc = k_ref[0, key_rows(c), :]
        for st in range(2):
            q = q_ref[0, st * half:(st + 1) * half, :]
            s = lax.dot_general(q, kc, (((1,), (1,)), ((), ())), preferred_element_type=F32)
            s_ref[st] = s
            pm = s[:, :LANES]
            for t in range(1, tq // LANES):
                pm = jnp.maximum(pm, s[:, t * LANES:(t + 1) * LANES])
            p_ref[st] = pm

    def update(st, s, row_max, vc):
        m_prev = m_ref[st]
        m_new = jnp.maximum(m_prev, row_max)
        alpha = jnp.exp2(m_prev - m_new)
        pr = jnp.exp2(s - m_new).astype(BF16)
        acc_ref[st] = alpha * acc_ref[st] + jnp.dot(pr, vc, preferred_element_type=F32)
        m_ref[st] = m_new

    def accumulate(c, s_ref, p_ref):
        vc = v_ref[0, key_rows(c), :]
        for st in range(2):
            update(st, s_ref[st], jnp.max(p_ref[st], axis=-1, keepdims=True), vc)

    def accumulate_own(s_ref):
        s0 = s_ref[0, :, :half]
        rows = lax.broadcasted_iota(jnp.int32, s0.shape, 0)
        cols = lax.broadcasted_iota(jnp.int32, s0.shape, 1)
        s0 = jnp.where(cols <= rows, s0, NEG)
        update(0, s0, jnp.max(s0, axis=-1, keepdims=True), v_ref[0, key_rows(i, half), :])
        s1 = s_ref[1]
        rows = lax.broadcasted_iota(jnp.int32, s1.shape, 0)
        cols = lax.broadcasted_iota(jnp.int32, s1.shape, 1)
        s1 = jnp.where(cols <= rows + half, s1, NEG)
        update(1, s1, jnp.max(s1, axis=-1, keepdims=True), v_ref[0, key_rows(i), :])

    scores(0, sa_ref, pa_ref)

    def pair(jj, carry):
        c = 2 * jj
        scores(c + 1, sb_ref, pb_ref)
        accumulate(c, sa_ref, pa_ref)
        scores(c + 2, sa_ref, pa_ref)
        accumulate(c + 1, sb_ref, pb_ref)
        return carry

    lax.fori_loop(0, lax.div(i, 2), pair, 0)

    @pl.when(lax.rem(i, 2) == 1)
    def _():
        scores(i, sb_ref, pb_ref)
        accumulate(i - 1, sa_ref, pa_ref)
        accumulate_own(sb_ref)

    @pl.when(lax.rem(i, 2) == 0)
    def _():
        accumulate_own(sa_ref)

    for st in range(2):
        acc = acc_ref[st]
        o_ref[st * half:(st + 1) * half, :] = (acc[:, :HEAD_DIM] / acc[:, HEAD_DIM:]).astype(BF16)


def _flash(q, k, v, *, tq=1024):
    _, s, _ = q.shape
    return pl.pallas_call(
        functools.partial(_flash_kernel, tq=tq),
        grid=(N_HEADS, s // tq),
        in_specs=[
            pl.BlockSpec((1, tq, MLA_QK_PAD), lambda h, i: (h, i, 0)),
            pl.BlockSpec((1, s, MLA_QK_PAD), lambda h, i: (h, 0, 0)),
            pl.BlockSpec((1, s, 2 * HEAD_DIM), lambda h, i: (h, 0, 0)),
        ],
        out_specs=pl.BlockSpec((tq, HEAD_DIM), lambda h, i: (i, h)),
        out_shape=jax.ShapeDtypeStruct((s, N_HEADS * HEAD_DIM), BF16),
        scratch_shapes=[
            pltpu.VMEM((2, tq // 2, 1), F32),
            pltpu.VMEM((2, tq // 2, 2 * HEAD_DIM), F32),
            pltpu.VMEM((2, tq // 2, tq), F32),
            pltpu.VMEM((2, tq // 2, tq), F32),
            pltpu.VMEM((2, tq // 2, LANES), F32),
            pltpu.VMEM((2, tq // 2, LANES), F32),
        ],
        compiler_params=_params("parallel", "arbitrary"),
        name="mla_flash",
    )(q, k, v)


def _dil_kernel(q_ref, kc_ref, kp_ref, vc_ref, vp_ref, o_ref, qf, kk, vv, o_sc, l_sc):
    t = q_ref.shape[1]
    blk = DIL_SPAN
    n = pl.program_id(0)
    qf[...] = q_ref[0].astype(F32)
    kk[:t] = kp_ref[0].astype(F32)
    kk[t:] = kc_ref[0].astype(F32)
    vv[:t] = vp_ref[0].astype(F32)
    vv[t:] = vc_ref[0].astype(F32)

    qi = lax.broadcasted_iota(jnp.int32, (blk, 2 * blk), 0)
    kj = lax.broadcasted_iota(jnp.int32, (blk, 2 * blk), 1)
    diff = kj - qi
    band = (diff >= 0) & (diff <= DIL_SPAN)

    for idx, (window, d) in enumerate(DIL_PATTERNS):
        seg = blk * d
        shift = d.bit_length() - 1

        def rows(ref, start, count, d=d):
            if d == 1:
                return ref[pl.ds(start, count), :]
            return ref[pl.ds(start, count, stride=d), :]

        def body(it, carry, idx=idx, d=d, seg=seg, shift=shift, rows=rows):
            r = jnp.bitwise_and(it, d - 1)
            b = jnp.right_shift(it, shift)
            start = b * seg + r
            first = jnp.logical_and(n == 0, b == 0)
            valid = band & (kj >= jnp.where(first, blk, 0))
            q = rows(qf, start, blk).astype(BF16)
            k = rows(kk, t + start - seg, 2 * blk).astype(BF16)
            v = rows(vv, t + start - seg, 2 * blk).astype(BF16)
            s = lax.dot_general(q, k, (((1,), (1,)), ((), ())), preferred_element_type=F32)
            s = jnp.where(valid, s, NEG)
            m = jnp.max(s, axis=-1, keepdims=True)
            pr = jnp.exp(s - m)
            l = jnp.sum(pr, axis=-1, keepdims=True)
            o = jnp.dot(pr.astype(BF16), v, preferred_element_type=F32) / l
            lse = jnp.broadcast_to(m + jnp.log(l), (blk, LANES))
            if d == 1:
                o_sc[idx, pl.ds(start, blk), :] = o
                l_sc[idx, pl.ds(start, blk), :] = lse
            else:
                o_sc[idx, pl.ds(start, blk, stride=d), :] = o
                l_sc[idx, pl.ds(start, blk, stride=d), :] = lse
            return carry

        lax.fori_loop(0, t // blk, body, 0, unroll=16)

    lses = [l_sc[p] for p in range(len(DIL_PATTERNS))]
    mx = functools.reduce(jnp.maximum, lses)
    ws = [jnp.exp(l - mx) for l in lses]
    num = sum(w * o_sc[p] for p, w in enumerate(ws))
    o_ref[...] = (num / sum(ws)).astype(BF16)


def _dilated(qkv):
    _, s, width = qkv.shape
    t = DIL_TILE
    npat = len(DIL_PATTERNS)

    def cur(which):
        return pl.BlockSpec((1, t, HEAD_DIM), lambda n, h: (which, n, h))

    def prv(which):
        return pl.BlockSpec((1, t, HEAD_DIM), lambda n, h: (which, jnp.maximum(n - 1, 0), h))

    return pl.pallas_call(
        _dil_kernel,
        grid=(s // t, N_HEADS),
        in_specs=[cur(0), cur(1), prv(1), cur(2), prv(2)],
        out_specs=pl.BlockSpec((t, HEAD_DIM), lambda n, h: (n, h)),
        out_shape=jax.ShapeDtypeStruct((s, width), BF16),
        scratch_shapes=[
            pltpu.VMEM((t, HEAD_DIM), F32),
            pltpu.VMEM((2 * t, HEAD_DIM), F32),
            pltpu.VMEM((2 * t, HEAD_DIM), F32),
            pltpu.VMEM((npat, t, HEAD_DIM), F32),
            pltpu.VMEM((npat, t, LANES), F32),
        ],
        compiler_params=_params("parallel", "parallel"),
        name="dilated",
    )(qkv, qkv, qkv, qkv, qkv)


def _oproj_kernel(x_ref, oa_ref, ob_ref, w_ref, g_ref, o_ref):
    half = oa_ref.shape[1]
    y = jnp.dot(oa_ref[...], w_ref[0, :half], preferred_element_type=F32)
    y = y + jnp.dot(ob_ref[...], w_ref[0, half:], preferred_element_type=F32)
    o_ref[...] = x_ref[...] + _rms(y, g_ref[0])


def _oproj(x, oa, ob, wo, g, layer, *, tm=512):
    s, d = x.shape
    width = oa.shape[1]
    return pl.pallas_call(
        _oproj_kernel,
        grid=(s // tm,),
        in_specs=[
            pl.BlockSpec((tm, d), lambda m: (m, 0)),
            pl.BlockSpec((tm, width), lambda m: (m, 0)),
            pl.BlockSpec((tm, width), lambda m: (m, 0)),
            pl.BlockSpec((1,) + wo.shape[1:], lambda m: (layer, 0, 0)),
            pl.BlockSpec((1, 1, d), lambda m: (layer, 0, 0)),
        ],
        out_specs=pl.BlockSpec((tm, d), lambda m: (m, 0)),
        out_shape=jax.ShapeDtypeStruct((s, d), F32),
        compiler_params=_params("parallel"),
        name="oproj",
    )(x, oa, ob, wo, g)


def _rope_tables(pos, dim):
    half = dim // 2
    inv = ROPE_THETA ** (-jnp.arange(0, dim, 2, dtype=F32) / dim)
    ang = pos.astype(F32)[:, None] * inv
    cos, sin = jnp.cos(ang), jnp.sin(ang)
    n = pos.shape[0]
    c = jnp.concatenate([cos, cos, jnp.ones((n, LANES - dim), F32)], axis=1)
    a = jnp.concatenate([-sin, jnp.zeros((n, LANES - half), F32)], axis=1)
    b = jnp.concatenate([jnp.zeros((n, half), F32), sin, jnp.zeros((n, LANES - dim), F32)], axis=1)
    return c, a, b


def kernel(x, positions, ffn1_pre_g, ffn1_post_g, ffn1_w_gate, ffn1_w_up, ffn1_w_down,
           mix_pre_g, mix_post_g, w_in, mla_q_norm_g, mla_w_uq, mla_kv_norm_g, mla_w_ukv,
           w_o, ffn2_pre_g, ffn2_post_g, ffn2_w_gate, ffn2_w_up, ffn2_w_down):
    batch, _, _ = x.shape
    depth = w_in.shape[0]
    bf = lambda w: w.astype(BF16)

    lat_cols = 2 * MLA_RANK
    w_in = bf(w_in)
    wc = w_in[:, :, :lat_cols]
    wkr = jnp.pad(w_in[:, :, lat_cols:lat_cols + MLA_ROPE], ((0, 0), (0, 0), (0, LANES - MLA_ROPE)))
    wd = w_in[:, :, lat_cols + MLA_ROPE:]
    wuq = bf(mla_w_uq).reshape(depth, MLA_RANK, N_HEADS, MLA_QK)
    wuq = jnp.pad(wuq, ((0, 0), (0, 0), (0, 0), (0, MLA_QK_PAD - MLA_QK))
                  ).reshape(depth, MLA_RANK, N_HEADS * MLA_QK_PAD)
    wukv = bf(mla_w_ukv)
    wo = bf(w_o)
    f1 = (bf(ffn1_w_gate), bf(ffn1_w_up), bf(ffn1_w_down))
    f2 = (bf(ffn2_w_gate), bf(ffn2_w_up), bf(ffn2_w_down))
    row = lambda g: g[:, None, :]
    ffn1_pre_g, ffn1_post_g, ffn2_pre_g, ffn2_post_g = map(
        row, (ffn1_pre_g, ffn1_post_g, ffn2_pre_g, ffn2_post_g))
    mix_pre_g, mix_post_g, mla_q_norm_g, mla_kv_norm_g = map(
        row, (mix_pre_g, mix_post_g, mla_q_norm_g, mla_kv_norm_g))

    outs = []
    for bi in range(batch):
        xs = x[bi]
        tabs_a = _rope_tables(positions[bi], MLA_ROPE)
        cp, ap, bp = _rope_tables(positions[bi], PART_ROPE)
        qscale = 1.0 / math.sqrt(HEAD_DIM)
        tabs_p = (jnp.stack([cp * qscale, cp, jnp.ones_like(cp)]),
                  jnp.stack([ap * qscale, ap, jnp.zeros_like(ap)]),
                  jnp.stack([bp * qscale, bp, jnp.zeros_like(bp)]))
        for l in range(depth):
            xs = _ffn(xs, ffn1_pre_g, ffn1_post_g, *f1, l)
            qkv = _dproj(xs, mix_pre_g, wd, tabs_p, l)
            qm, km, vm = _mproj(xs, mix_pre_g, wc, wkr, mla_q_norm_g, wuq,
                                mla_kv_norm_g, wukv, tabs_a, l)
            oa = _flash(qm, km, vm)
            ob = _dilated(qkv)
            xs = _oproj(xs, oa, ob, wo, mix_post_g, l)
            xs = _ffn(xs, ffn2_pre_g, ffn2_post_g, *f2, l)
        outs.append(xs)
    return jnp.stack(outs, axis=0)
```

```python
import functools
import math

import jax
import jax.numpy as jnp
from jax import lax
from jax.experimental import pallas as pl
from jax.experimental.pallas import tpu as pltpu

F32 = jnp.float32
BF16 = jnp.bfloat16

HEAD_DIM = 128
N_HEADS = 8
MLA_RANK = 512
MLA_NOPE = 128
MLA_ROPE = 64
MLA_QK = MLA_NOPE + MLA_ROPE
MLA_QK_PAD = 256
PART_ROPE = 32
DIL_PATTERNS = ((128, 1), (512, 4), (2048, 16))
DIL_SPAN = 128
DIL_TILE = 2048
ROPE_THETA = 500000.0
RMS_EPS = 1e-6
NEG = -1e30
LANES = 128
VMEM_LIMIT = 58 * 1024 * 1024
FFN_NORM_ROWS = 256
FFN_DOT_ROWS = 512


def _rms(xf, g):
    ms = jnp.mean(xf * xf, axis=-1, keepdims=True)
    return xf * lax.rsqrt(ms + RMS_EPS) * g


def _rope(y, c, a, b, half):
    return y * c + pltpu.roll(y, LANES - half, axis=1) * a + pltpu.roll(y, half, axis=1) * b


def _params(*sem):
    return pltpu.CompilerParams(dimension_semantics=sem, vmem_limit_bytes=VMEM_LIMIT)


def _ffn_kernel(*refs, cast_next):
    x_ref, gpre_ref, gpost_ref, wg_ref, wu_ref, wd_ref = refs[:6]
    if cast_next:
        o_ref, nb_ref = refs[9], refs[13]
        for src_ref, dst_ref in zip(refs[6:9], refs[10:13]):
            dst_ref[...] = src_ref[...].astype(BF16)
    else:
        o_ref, nb_ref = refs[6:]
    j = pl.program_id(1)

    tm = x_ref.shape[0]

    def row_chunks(size):
        return [slice(r, r + size) for r in range(0, tm, size)]

    @pl.when(j == 0)
    def _():
        for rows in row_chunks(FFN_NORM_ROWS):
            nb_ref[rows] = _rms(x_ref[rows], gpre_ref[0]).astype(BF16)
        o_ref[...] = jnp.zeros_like(o_ref)

    for rows in row_chunks(FFN_DOT_ROWS):
        nb = nb_ref[rows]
        g = jnp.dot(nb, wg_ref[0], preferred_element_type=F32)
        u = jnp.dot(nb, wu_ref[0], preferred_element_type=F32)
        h = (g * jax.nn.sigmoid(g) * u).astype(BF16)
        o_ref[rows] += jnp.dot(h, wd_ref[0], preferred_element_type=F32)

    @pl.when(j == pl.num_programs(1) - 1)
    def _():
        for rows in row_chunks(FFN_NORM_ROWS):
            o_ref[rows] = x_ref[rows] + 0.5 * _rms(o_ref[rows], gpost_ref[0])


def _ffn(x, gpre, gpost, weights, layer, cast_next=None, *, tm=1024, tn=512):
    s, d = x.shape
    wg, wu, wd = weights
    dff = wg.shape[-1]
    nm = s // tm
    dm = d // nm
    in_specs = [
        pl.BlockSpec((tm, d), lambda m, j: (m, 0)),
        pl.BlockSpec((1, 1, d), lambda m, j: (layer, 0, 0)),
        pl.BlockSpec((1, 1, d), lambda m, j: (layer, 0, 0)),
        pl.BlockSpec((1, d, tn), lambda m, j: (layer, 0, j)),
        pl.BlockSpec((1, d, tn), lambda m, j: (layer, 0, j)),
        pl.BlockSpec((1, tn, d), lambda m, j: (layer, j, 0)),
    ]
    out_specs = [pl.BlockSpec((tm, d), lambda m, j: (m, 0))]
    out_shape = [jax.ShapeDtypeStruct((s, d), F32)]
    args = [x, gpre, gpost, wg, wu, wd]
    if cast_next is not None:
        *nxt, nl = cast_next
        in_specs += [
            pl.BlockSpec((1, dm, tn), lambda m, j: (nl, m, j)),
            pl.BlockSpec((1, dm, tn), lambda m, j: (nl, m, j)),
            pl.BlockSpec((1, tn, dm), lambda m, j: (nl, j, m)),
        ]
        out_specs += [
            pl.BlockSpec((1, dm, tn), lambda m, j: (0, m, j)),
            pl.BlockSpec((1, dm, tn), lambda m, j: (0, m, j)),
            pl.BlockSpec((1, tn, dm), lambda m, j: (0, j, m)),
        ]
        out_shape += [jax.ShapeDtypeStruct((1,) + w.shape[1:], BF16) for w in nxt]
        args += nxt
    out = pl.pallas_call(
        functools.partial(_ffn_kernel, cast_next=cast_next is not None),
        grid=(nm, dff // tn),
        in_specs=in_specs,
        out_specs=out_specs,
        out_shape=out_shape,
        scratch_shapes=[pltpu.VMEM((tm, d), BF16)],
        compiler_params=_params("parallel", "arbitrary"),
        name="ffn",
    )(*args)
    return out[0], tuple(out[1:])


def _dproj_kernel(x_ref, g_ref, w_ref, c_ref, a_ref, b_ref, o_ref, nb_ref):
    j = pl.program_id(1)

    @pl.when(j == 0)
    def _():
        for r in range(0, x_ref.shape[0], FFN_NORM_ROWS):
            rows = slice(r, r + FFN_NORM_ROWS)
            nb_ref[rows] = _rms(x_ref[rows], g_ref[0]).astype(BF16)

    y = jnp.dot(nb_ref[...], w_ref[0], preferred_element_type=F32)
    c, a, b = c_ref[0], a_ref[0], b_ref[0]
    for h in range(N_HEADS):
        sl = slice(h * HEAD_DIM, (h + 1) * HEAD_DIM)
        o_ref[0, :, sl] = _rope(y[:, sl], c, a, b, PART_ROPE // 2).astype(BF16)


def _dproj(x, g, wd, tabs, layer, *, tm=1024):
    s, d = x.shape
    width = N_HEADS * HEAD_DIM
    tab_spec = pl.BlockSpec((1, tm, LANES), lambda m, j: (j, m, 0))
    return pl.pallas_call(
        _dproj_kernel,
        grid=(s // tm, 3),
        in_specs=[
            pl.BlockSpec((tm, d), lambda m, j: (m, 0)),
            pl.BlockSpec((1, 1, d), lambda m, j: (layer, 0, 0)),
            pl.BlockSpec((1, d, width), lambda m, j: (layer, 0, j)),
            tab_spec, tab_spec, tab_spec,
        ],
        out_specs=pl.BlockSpec((1, tm, width), lambda m, j: (j, m, 0)),
        out_shape=jax.ShapeDtypeStruct((3, s, width), BF16),
        scratch_shapes=[pltpu.VMEM((tm, d), BF16)],
        compiler_params=_params("parallel", "arbitrary"),
        name="dproj",
    )(x, g, wd, *tabs)


def _mproj_kernel(x_ref, g_ref, wc_ref, wkr_ref, gq_ref, wuq_ref, gkv_ref, wukv_ref,
                  c_ref, a_ref, b_ref, q_ref, k_ref, v_ref):
    nb = _rms(x_ref[...], g_ref[0]).astype(BF16)
    lat = jnp.dot(nb, wc_ref[0], preferred_element_type=F32)
    cq = _rms(lat[:, :MLA_RANK], gq_ref[0]).astype(BF16)
    ckv = _rms(lat[:, MLA_RANK:], gkv_ref[0]).astype(BF16)
    c, a, b = c_ref[...], a_ref[...], b_ref[...]
    half = MLA_ROPE // 2
    kr = jnp.dot(nb, wkr_ref[0], preferred_element_type=F32)
    kr = _rope(kr, c, a, b, half).astype(BF16)
    q = jnp.dot(cq, wuq_ref[0], preferred_element_type=F32)
    kv = jnp.dot(ckv, wukv_ref[0], preferred_element_type=F32)
    scale = math.log2(math.e) / math.sqrt(MLA_QK)
    ones = jnp.ones((x_ref.shape[0], HEAD_DIM), BF16)
    for h in range(N_HEADS):
        lo = h * MLA_QK_PAD
        q_ref[h, :, :MLA_NOPE] = (q[:, lo:lo + MLA_NOPE] * scale).astype(BF16)
        qr = _rope(q[:, lo + MLA_NOPE:lo + MLA_QK_PAD], c, a, b, half)
        q_ref[h, :, MLA_NOPE:] = (qr * scale).astype(BF16)
        k_ref[h, :, :MLA_NOPE] = kv[:, lo:lo + MLA_NOPE].astype(BF16)
        k_ref[h, :, MLA_NOPE:] = kr
        v_ref[h, :, :HEAD_DIM] = kv[:, lo + MLA_NOPE:lo + MLA_QK_PAD].astype(BF16)
        v_ref[h, :, HEAD_DIM:] = ones


def _mproj(x, g, wc, wkr, gq, wuq, gkv, wukv, tabs, layer, *, tm=512):
    s, d = x.shape
    tab_spec = pl.BlockSpec((tm, LANES), lambda m: (m, 0))

    def wspec(w):
        return pl.BlockSpec((1,) + w.shape[1:], lambda m: (layer, 0, 0))

    def gspec(gv):
        return pl.BlockSpec((1, 1, gv.shape[2]), lambda m: (layer, 0, 0))

    return pl.pallas_call(
        _mproj_kernel,
        grid=(s // tm,),
        in_specs=[
            pl.BlockSpec((tm, d), lambda m: (m, 0)),
            gspec(g), wspec(wc), wspec(wkr), gspec(gq), wspec(wuq), gspec(gkv), wspec(wukv),
            tab_spec, tab_spec, tab_spec,
        ],
        out_specs=[
            pl.BlockSpec((N_HEADS, tm, MLA_QK_PAD), lambda m: (0, m, 0)),
            pl.BlockSpec((N_HEADS, tm, MLA_QK_PAD), lambda m: (0, m, 0)),
            pl.BlockSpec((N_HEADS, tm, 2 * HEAD_DIM), lambda m: (0, m, 0)),
        ],
        out_shape=[
            jax.ShapeDtypeStruct((N_HEADS, s, MLA_QK_PAD), BF16),
            jax.ShapeDtypeStruct((N_HEADS, s, MLA_QK_PAD), BF16),
            jax.ShapeDtypeStruct((N_HEADS, s, 2 * HEAD_DIM), BF16),
        ],
        compiler_params=_params("parallel"),
        name="mproj",
    )(x, g, wc, wkr, gq, wuq, gkv, wukv, *tabs)


def _flash_kernel(q_ref, k_ref, v_ref, o_ref, m_ref, acc_ref,
                  sa_ref, sb_ref, pa_ref, pb_ref, *, tq):
    half = tq // 2
    i = pl.program_id(1)
    m_ref[...] = jnp.full_like(m_ref, NEG)
    acc_ref[...] = jnp.zeros_like(acc_ref)

    def key_rows(c, n=tq):
        return pl.ds(pl.multiple_of(c * tq, tq), n)

    def scores(c, s_ref, p_ref):
        kc = k_ref[0, key_rows(c), :]
        for st in range(2):
            q = q_ref[0, st * half:(st + 1) * half, :]
            s = lax.dot_general(q, kc, (((1,), (1,)), ((), ())), preferred_element_type=F32)
            s_ref[st] = s
            pm = s[:, :LANES]
            for t in range(1, tq // LANES):
                pm = jnp.maximum(pm, s[:, t * LANES:(t + 1) * LANES])
            p_ref[st] = pm

    def update(st, s, row_max, vc):
        m_prev = m_ref[st]
        m_new = jnp.maximum(m_prev, row_max)
        alpha = jnp.exp2(m_prev - m_new)
        pr = jnp.exp2(s - m_new).astype(BF16)
        acc_ref[st] = alpha * acc_ref[st] + jnp.dot(pr, vc, preferred_element_type=F32)
        m_ref[st] = m_new

    def accumulate(c, s_ref, p_ref):
        vc = v_ref[0, key_rows(c), :]
        for st in range(2):
            update(st, s_ref[st], jnp.max(p_ref[st], axis=-1, keepdims=True), vc)

    def accumulate_own(s_ref):
        s0 = s_ref[0, :, :half]
        rows = lax.broadcasted_iota(jnp.int32, s0.shape, 0)
        cols = lax.broadcasted_iota(jnp.int32, s0.shape, 1)
        s0 = jnp.where(cols <= rows, s0, NEG)
        update(0, s0, jnp.max(s0, axis=-1, keepdims=True), v_ref[0, key_rows(i, half), :])
        s1 = s_ref[1]
        rows = lax.broadcasted_iota(jnp.int32, s1.shape, 0)
        cols = lax.broadcasted_iota(jnp.int32, s1.shape, 1)
        s1 = jnp.where(cols <= rows + half, s1, NEG)
        update(1, s1, jnp.max(s1, axis=-1, keepdims=True), v_ref[0, key_rows(i), :])

    scores(0, sa_ref, pa_ref)

    def pair(jj, carry):
        c = 2 * jj
        scores(c + 1, sb_ref, pb_ref)
        accumulate(c, sa_ref, pa_ref)
        scores(c + 2, sa_ref, pa_ref)
        accumulate(c + 1, sb_ref, pb_ref)
        return carry

    lax.fori_loop(0, lax.div(i, 2), pair, 0)

    @pl.when(lax.rem(i, 2) == 1)
    def _():
        scores(i, sb_ref, pb_ref)
        accumulate(i - 1, sa_ref, pa_ref)
        accumulate_own(sb_ref)

    @pl.when(lax.rem(i, 2) == 0)
    def _():
        accumulate_own(sa_ref)

    for st in range(2):
        acc = acc_ref[st]
        o_ref[st * half:(st + 1) * half, :] = (acc[:, :HEAD_DIM] / acc[:, HEAD_DIM:]).astype(BF16)


def _flash(q, k, v, *, tq=1024):
    _, s, _ = q.shape
    return pl.pallas_call(
        functools.partial(_flash_kernel, tq=tq),
        grid=(N_HEADS, s // tq),
        in_specs=[
            pl.BlockSpec((1, tq, MLA_QK_PAD), lambda h, i: (h, i, 0)),
            pl.BlockSpec((1, s, MLA_QK_PAD), lambda h, i: (h, 0, 0)),
            pl.BlockSpec((1, s, 2 * HEAD_DIM), lambda h, i: (h, 0, 0)),
        ],
        out_specs=pl.BlockSpec((tq, HEAD_DIM), lambda h, i: (i, h)),
        out_shape=jax.ShapeDtypeStruct((s, N_HEADS * HEAD_DIM), BF16),
        scratch_shapes=[
            pltpu.VMEM((2, tq // 2, 1), F32),
            pltpu.VMEM((2, tq // 2, 2 * HEAD_DIM), F32),
            pltpu.VMEM((2, tq // 2, tq), F32),
            pltpu.VMEM((2, tq // 2, tq), F32),
            pltpu.VMEM((2, tq // 2, LANES), F32),
            pltpu.VMEM((2, tq // 2, LANES), F32),
        ],
        compiler_params=_params("parallel", "arbitrary"),
        name="mla_flash",
    )(q, k, v)


def _dil_kernel(q_ref, kc_ref, kp_ref, vc_ref, vp_ref, o_ref, qf, kk, vv, o_sc, l_sc):
    t = q_ref.shape[1]
    blk = DIL_SPAN
    n = pl.program_id(0)
    qf[...] = q_ref[0].astype(F32)
    kk[:t] = kp_ref[0].astype(F32)
    kk[t:] = kc_ref[0].astype(F32)
    vv[:t] = vp_ref[0].astype(F32)
    vv[t:] = vc_ref[0].astype(F32)

    qi = lax.broadcasted_iota(jnp.int32, (blk, 2 * blk), 0)
    kj = lax.broadcasted_iota(jnp.int32, (blk, 2 * blk), 1)
    diff = kj - qi
    band = (diff >= 0) & (diff <= DIL_SPAN)

    for idx, (window, d) in enumerate(DIL_PATTERNS):
        seg = blk * d
        shift = d.bit_length() - 1

        def rows(ref, start, count, d=d):
            if d == 1:
                return ref[pl.ds(start, count), :]
            return ref[pl.ds(start, count, stride=d), :]

        def body(it, carry, idx=idx, d=d, seg=seg, shift=shift, rows=rows):
            r = jnp.bitwise_and(it, d - 1)
            b = jnp.right_shift(it, shift)
            start = b * seg + r
            first = jnp.logical_and(n == 0, b == 0)
            valid = band & (kj >= jnp.where(first, blk, 0))
            q = rows(qf, start, blk).astype(BF16)
            k = rows(kk, t + start - seg, 2 * blk).astype(BF16)
            v = rows(vv, t + start - seg, 2 * blk).astype(BF16)
            s = lax.dot_general(q, k, (((1,), (1,)), ((), ())), preferred_element_type=F32)
            s = jnp.where(valid, s, NEG)
            m = jnp.max(s, axis=-1, keepdims=True)
            pr = jnp.exp(s - m)
            l = jnp.sum(pr, axis=-1, keepdims=True)
            o = jnp.dot(pr.astype(BF16), v, preferred_element_type=F32) / l
            lse = jnp.broadcast_to(m + jnp.log(l), (blk, LANES))
            if d == 1:
                o_sc[idx, pl.ds(start, blk), :] = o
                l_sc[idx, pl.ds(start, blk), :] = lse
            else:
                o_sc[idx, pl.ds(start, blk, stride=d), :] = o
                l_sc[idx, pl.ds(start, blk, stride=d), :] = lse
            return carry

        lax.fori_loop(0, t // blk, body, 0, unroll=16)

    lses = [l_sc[p] for p in range(len(DIL_PATTERNS))]
    mx = functools.reduce(jnp.maximum, lses)
    ws = [jnp.exp(l - mx) for l in lses]
    num = sum(w * o_sc[p] for p, w in enumerate(ws))
    o_ref[...] = (num / sum(ws)).astype(BF16)


def _dilated(qkv):
    _, s, width = qkv.shape
    t = DIL_TILE
    npat = len(DIL_PATTERNS)

    def cur(which):
        return pl.BlockSpec((1, t, HEAD_DIM), lambda n, h: (which, n, h))

    def prv(which):
        return pl.BlockSpec((1, t, HEAD_DIM), lambda n, h: (which, jnp.maximum(n - 1, 0), h))

    return pl.pallas_call(
        _dil_kernel,
        grid=(s // t, N_HEADS),
        in_specs=[cur(0), cur(1), prv(1), cur(2), prv(2)],
        out_specs=pl.BlockSpec((t, HEAD_DIM), lambda n, h: (n, h)),
        out_shape=jax.ShapeDtypeStruct((s, width), BF16),
        scratch_shapes=[
            pltpu.VMEM((t, HEAD_DIM), F32),
            pltpu.VMEM((2 * t, HEAD_DIM), F32),
            pltpu.VMEM((2 * t, HEAD_DIM), F32),
            pltpu.VMEM((npat, t, HEAD_DIM), F32),
            pltpu.VMEM((npat, t, LANES), F32),
        ],
        compiler_params=_params("parallel", "parallel"),
        name="dilated",
    )(qkv, qkv, qkv, qkv, qkv)


def _oproj_kernel(x_ref, oa_ref, ob_ref, w_ref, g_ref, o_ref):
    half = oa_ref.shape[1]
    y = jnp.dot(oa_ref[...], w_ref[0, :half], preferred_element_type=F32)
    y = y + jnp.dot(ob_ref[...], w_ref[0, half:], preferred_element_type=F32)
    o_ref[...] = x_ref[...] + _rms(y, g_ref[0])


def _oproj(x, oa, ob, wo, g, layer, *, tm=512):
    s, d = x.shape
    width = oa.shape[1]
    return pl.pallas_call(
        _oproj_kernel,
        grid=(s // tm,),
        in_specs=[
            pl.BlockSpec((tm, d), lambda m: (m, 0)),
            pl.BlockSpec((tm, width), lambda m: (m, 0)),
            pl.BlockSpec((tm, width), lambda m: (m, 0)),
            pl.BlockSpec((1,) + wo.shape[1:], lambda m: (layer, 0, 0)),
            pl.BlockSpec((1, 1, d), lambda m: (layer, 0, 0)),
        ],
        out_specs=pl.BlockSpec((tm, d), lambda m: (m, 0)),
        out_shape=jax.ShapeDtypeStruct((s, d), F32),
        compiler_params=_params("parallel"),
        name="oproj",
    )(x, oa, ob, wo, g)


def _rope_tables(pos, dim):
    half = dim // 2
    inv = ROPE_THETA ** (-jnp.arange(0, dim, 2, dtype=F32) / dim)
    ang = pos.astype(F32)[:, None] * inv
    cos, sin = jnp.cos(ang), jnp.sin(ang)
    n = pos.shape[0]
    c = jnp.concatenate([cos, cos, jnp.ones((n, LANES - dim), F32)], axis=1)
    a = jnp.concatenate([-sin, jnp.zeros((n, LANES - half), F32)], axis=1)
    b = jnp.concatenate([jnp.zeros((n, half), F32), sin, jnp.zeros((n, LANES - dim), F32)], axis=1)
    return c, a, b


def kernel(x, positions, ffn1_pre_g, ffn1_post_g, ffn1_w_gate, ffn1_w_up, ffn1_w_down,
           mix_pre_g, mix_post_g, w_in, mla_q_norm_g, mla_w_uq, mla_kv_norm_g, mla_w_ukv,
           w_o, ffn2_pre_g, ffn2_post_g, ffn2_w_gate, ffn2_w_up, ffn2_w_down):
    batch, _, _ = x.shape
    depth = w_in.shape[0]
    bf = lambda w: w.astype(BF16)

    lat_cols = 2 * MLA_RANK
    w_in = bf(w_in)
    wc = w_in[:, :, :lat_cols]
    wkr = jnp.pad(w_in[:, :, lat_cols:lat_cols + MLA_ROPE], ((0, 0), (0, 0), (0, LANES - MLA_ROPE)))
    wd = w_in[:, :, lat_cols + MLA_ROPE:]
    wuq = bf(mla_w_uq).reshape(depth, MLA_RANK, N_HEADS, MLA_QK)
    wuq = jnp.pad(wuq, ((0, 0), (0, 0), (0, 0), (0, MLA_QK_PAD - MLA_QK))
                  ).reshape(depth, MLA_RANK, N_HEADS * MLA_QK_PAD)
    wukv = bf(mla_w_ukv)
    wo = bf(w_o)
    ffn_f32 = ((ffn1_w_gate, ffn1_w_up, ffn1_w_down), (ffn2_w_gate, ffn2_w_up, ffn2_w_down))
    ffn_gains = ((ffn1_pre_g, ffn1_post_g), (ffn2_pre_g, ffn2_post_g))
    ffn_bf16 = {(0, 0): tuple(bf(w[:1]) for w in ffn_f32[0])}

    def ffn(xs, l, which):
        nxt = (l, 1) if which == 0 else (l + 1, 0)
        cast_next = None
        if nxt[0] < depth and nxt not in ffn_bf16:
            cast_next = ffn_f32[nxt[1]] + (nxt[0],)
        gpre, gpost = (g[l][None, None, :] for g in ffn_gains[which])
        xs, converted = _ffn(xs, gpre, gpost, ffn_bf16[(l, which)], 0, cast_next)
        if cast_next is not None:
            ffn_bf16[nxt] = converted
        return xs
    row = lambda g: g[:, None, :]
    mix_pre_g, mix_post_g, mla_q_norm_g, mla_kv_norm_g = map(
        row, (mix_pre_g, mix_post_g, mla_q_norm_g, mla_kv_norm_g))

    outs = []
    for bi in range(batch):
        xs = x[bi]
        tabs_a = _rope_tables(positions[bi], MLA_ROPE)
        cp, ap, bp = _rope_tables(positions[bi], PART_ROPE)
        qscale = 1.0 / math.sqrt(HEAD_DIM)
        tabs_p = (jnp.stack([cp * qscale, cp, jnp.ones_like(cp)]),
                  jnp.stack([ap * qscale, ap, jnp.zeros_like(ap)]),
                  jnp.stack([bp * qscale, bp, jnp.zeros_like(bp)]))
        for l in range(depth):
            xs = ffn(xs, l, 0)
            qkv = _dproj(xs, mix_pre_g, wd, tabs_p, l)
            qm, km, vm = _mproj(xs, mix_pre_g, wc, wkr, mla_q_norm_g, wuq,
                                mla_kv_norm_g, wukv, tabs_a, l)
            oa = _flash(qm, km, vm)
            ob = _dilated(qkv)
            xs = _oproj(xs, oa, ob, wo, mix_post_g, l)
            xs = ffn(xs, l, 1)
        outs.append(xs)
    return jnp.stack(outs, axis=0)
```

```python
import functools
import math

import jax
import jax.numpy as jnp
from jax import lax
from jax.experimental import pallas as pl
from jax.experimental.pallas import tpu as pltpu

F32 = jnp.float32
BF16 = jnp.bfloat16

HEAD_DIM = 128
N_HEADS = 8
MLA_RANK = 512
MLA_NOPE = 128
MLA_ROPE = 64
MLA_QK = MLA_NOPE + MLA_ROPE
MLA_QK_PAD = 256
PART_ROPE = 32
DIL_PATTERNS = ((128, 1), (512, 4), (2048, 16))
DIL_SPAN = 128
DIL_TILE = 2048
ROPE_THETA = 500000.0
RMS_EPS = 1e-6
NEG = -1e30
LANES = 128
VMEM_LIMIT = 58 * 1024 * 1024
FFN_NORM_ROWS = 256
FFN_DOT_ROWS = 512


def _rms(xf, g):
    ms = jnp.mean(xf * xf, axis=-1, keepdims=True)
    return xf * lax.rsqrt(ms + RMS_EPS) * g


def _rope(y, c, a, b, half):
    return y * c + pltpu.roll(y, LANES - half, axis=1) * a + pltpu.roll(y, half, axis=1) * b


def _params(*sem):
    return pltpu.CompilerParams(dimension_semantics=sem, vmem_limit_bytes=VMEM_LIMIT)


def _ffn_kernel(*refs, cast_next):
    x_ref, gpre_ref, gpost_ref, wg_ref, wu_ref, wd_ref = refs[:6]
    if cast_next:
        o_ref, nb_ref, r_ref = refs[9], refs[13], refs[14]
        for src_ref, dst_ref in zip(refs[6:9], refs[10:13]):
            dst_ref[...] = src_ref[...].astype(BF16)
    else:
        o_ref, nb_ref, r_ref = refs[6:]
    j = pl.program_id(1)

    tm = x_ref.shape[0]

    def row_chunks(size):
        return [slice(r, r + size) for r in range(0, tm, size)]

    @pl.when(j == 0)
    def _():
        for rows in row_chunks(FFN_NORM_ROWS):
            nb_ref[rows] = _rms(x_ref[rows], gpre_ref[0]).astype(BF16)
        o_ref[...] = jnp.zeros_like(o_ref)

    for rows in row_chunks(FFN_DOT_ROWS):
        nb = nb_ref[rows]
        g = jnp.dot(nb, wg_ref[0], preferred_element_type=F32)
        u = jnp.dot(nb, wu_ref[0], preferred_element_type=F32)
        h = (g * jax.nn.sigmoid(g) * u).astype(BF16)
        o_ref[rows] += jnp.dot(h, wd_ref[0], preferred_element_type=F32)

    @pl.when(j == pl.num_programs(1) - 1)
    def _():
        for rows in row_chunks(FFN_NORM_ROWS):
            y = o_ref[rows]
            r_ref[rows] = 0.5 * lax.rsqrt(jnp.mean(y * y, axis=-1, keepdims=True) + RMS_EPS)
        for rows in row_chunks(FFN_NORM_ROWS):
            o_ref[rows] = x_ref[rows] + o_ref[rows] * r_ref[rows] * gpost_ref[0]


def _ffn(x, gpre, gpost, weights, layer, cast_next=None, *, tm=1024, tn=512):
    s, d = x.shape
    wg, wu, wd = weights
    dff = wg.shape[-1]
    nm = s // tm
    dm = d // nm
    in_specs = [
        pl.BlockSpec((tm, d), lambda m, j: (m, 0)),
        pl.BlockSpec((1, 1, d), lambda m, j: (layer, 0, 0)),
        pl.BlockSpec((1, 1, d), lambda m, j: (layer, 0, 0)),
        pl.BlockSpec((1, d, tn), lambda m, j: (layer, 0, j)),
        pl.BlockSpec((1, d, tn), lambda m, j: (layer, 0, j)),
        pl.BlockSpec((1, tn, d), lambda m, j: (layer, j, 0)),
    ]
    out_specs = [pl.BlockSpec((tm, d), lambda m, j: (m, 0))]
    out_shape = [jax.ShapeDtypeStruct((s, d), F32)]
    args = [x, gpre, gpost, wg, wu, wd]
    if cast_next is not None:
        *nxt, nl = cast_next
        in_specs += [
            pl.BlockSpec((1, dm, tn), lambda m, j: (nl, m, j)),
            pl.BlockSpec((1, dm, tn), lambda m, j: (nl, m, j)),
            pl.BlockSpec((1, tn, dm), lambda m, j: (nl, j, m)),
        ]
        out_specs += [
            pl.BlockSpec((1, dm, tn), lambda m, j: (0, m, j)),
            pl.BlockSpec((1, dm, tn), lambda m, j: (0, m, j)),
            pl.BlockSpec((1, tn, dm), lambda m, j: (0, j, m)),
        ]
        out_shape += [jax.ShapeDtypeStruct((1,) + w.shape[1:], BF16) for w in nxt]
        args += nxt
    out = pl.pallas_call(
        functools.partial(_ffn_kernel, cast_next=cast_next is not None),
        grid=(nm, dff // tn),
        in_specs=in_specs,
        out_specs=out_specs,
        out_shape=out_shape,
        scratch_shapes=[pltpu.VMEM((tm, d), BF16), pltpu.VMEM((tm, 1), F32)],
        compiler_params=_params("parallel", "arbitrary"),
        name="ffn",
    )(*args)
    return out[0], tuple(out[1:])


def _dproj_kernel(x_ref, g_ref, w_ref, c_ref, a_ref, b_ref, o_ref, nb_ref):
    j = pl.program_id(1)

    @pl.when(j == 0)
    def _():
        for r in range(0, x_ref.shape[0], FFN_NORM_ROWS):
            rows = slice(r, r + FFN_NORM_ROWS)
            nb_ref[rows] = _rms(x_ref[rows], g_ref[0]).astype(BF16)

    y = jnp.dot(nb_ref[...], w_ref[0], preferred_element_type=F32)
    c, a, b = c_ref[0], a_ref[0], b_ref[0]
    for h in range(N_HEADS):
        sl = slice(h * HEAD_DIM, (h + 1) * HEAD_DIM)
        o_ref[0, :, sl] = _rope(y[:, sl], c, a, b, PART_ROPE // 2)


def _dproj(x, g, wd, tabs, layer, *, tm=1024):
    s, d = x.shape
    width = N_HEADS * HEAD_DIM
    tab_spec = pl.BlockSpec((1, tm, LANES), lambda m, j: (j, m, 0))
    return pl.pallas_call(
        _dproj_kernel,
        grid=(s // tm, 3),
        in_specs=[
            pl.BlockSpec((tm, d), lambda m, j: (m, 0)),
            pl.BlockSpec((1, 1, d), lambda m, j: (layer, 0, 0)),
            pl.BlockSpec((1, d, width), lambda m, j: (layer, 0, j)),
            tab_spec, tab_spec, tab_spec,
        ],
        out_specs=pl.BlockSpec((1, tm, width), lambda m, j: (j, m, 0)),
        out_shape=jax.ShapeDtypeStruct((3, s, width), F32),
        scratch_shapes=[pltpu.VMEM((tm, d), BF16)],
        compiler_params=_params("parallel", "arbitrary"),
        name="dproj",
    )(x, g, wd, *tabs)


def _mproj_kernel(x_ref, g_ref, wc_ref, wkr_ref, gq_ref, wuq_ref, gkv_ref, wukv_ref,
                  c_ref, a_ref, b_ref, q_ref, k_ref, v_ref):
    nb = _rms(x_ref[...], g_ref[0]).astype(BF16)
    lat = jnp.dot(nb, wc_ref[0], preferred_element_type=F32)
    cq = _rms(lat[:, :MLA_RANK], gq_ref[0]).astype(BF16)
    ckv = _rms(lat[:, MLA_RANK:], gkv_ref[0]).astype(BF16)
    c, a, b = c_ref[...], a_ref[...], b_ref[...]
    half = MLA_ROPE // 2
    kr = jnp.dot(nb, wkr_ref[0], preferred_element_type=F32)
    kr = _rope(kr, c, a, b, half).astype(BF16)
    q = jnp.dot(cq, wuq_ref[0], preferred_element_type=F32)
    kv = jnp.dot(ckv, wukv_ref[0], preferred_element_type=F32)
    scale = math.log2(math.e) / math.sqrt(MLA_QK)
    ones = jnp.ones((x_ref.shape[0], HEAD_DIM), BF16)
    for h in range(N_HEADS):
        lo = h * MLA_QK_PAD
        q_ref[h, :, :MLA_NOPE] = (q[:, lo:lo + MLA_NOPE] * scale).astype(BF16)
        qr = _rope(q[:, lo + MLA_NOPE:lo + MLA_QK_PAD], c, a, b, half)
        q_ref[h, :, MLA_NOPE:] = (qr * scale).astype(BF16)
        k_ref[h, :, :MLA_NOPE] = kv[:, lo:lo + MLA_NOPE].astype(BF16)
        k_ref[h, :, MLA_NOPE:] = kr
        v_ref[h, :, :HEAD_DIM] = kv[:, lo + MLA_NOPE:lo + MLA_QK_PAD].astype(BF16)
        v_ref[h, :, HEAD_DIM:] = ones


def _mproj(x, g, wc, wkr, gq, wuq, gkv, wukv, tabs, layer, *, tm=512):
    s, d = x.shape
    tab_spec = pl.BlockSpec((tm, LANES), lambda m: (m, 0))

    def wspec(w):
        return pl.BlockSpec((1,) + w.shape[1:], lambda m: (layer, 0, 0))

    def gspec(gv):
        return pl.BlockSpec((1, 1, gv.shape[2]), lambda m: (layer, 0, 0))

    return pl.pallas_call(
        _mproj_kernel,
        grid=(s // tm,),
        in_specs=[
            pl.BlockSpec((tm, d), lambda m: (m, 0)),
            gspec(g), wspec(wc), wspec(wkr), gspec(gq), wspec(wuq), gspec(gkv), wspec(wukv),
            tab_spec, tab_spec, tab_spec,
        ],
        out_specs=[
            pl.BlockSpec((N_HEADS, tm, MLA_QK_PAD), lambda m: (0, m, 0)),
            pl.BlockSpec((N_HEADS, tm, MLA_QK_PAD), lambda m: (0, m, 0)),
            pl.BlockSpec((N_HEADS, tm, 2 * HEAD_DIM), lambda m: (0, m, 0)),
        ],
        out_shape=[
            jax.ShapeDtypeStruct((N_HEADS, s, MLA_QK_PAD), BF16),
            jax.ShapeDtypeStruct((N_HEADS, s, MLA_QK_PAD), BF16),
            jax.ShapeDtypeStruct((N_HEADS, s, 2 * HEAD_DIM), BF16),
        ],
        compiler_params=_params("parallel"),
        name="mproj",
    )(x, g, wc, wkr, gq, wuq, gkv, wukv, *tabs)


def _flash_kernel(q_ref, k_ref, v_ref, o_ref, m_ref, acc_ref,
                  sa_ref, sb_ref, pa_ref, pb_ref, *, tq):
    half = tq // 2
    i = pl.program_id(1)
    m_ref[...] = jnp.full_like(m_ref, NEG)
    acc_ref[...] = jnp.zeros_like(acc_ref)

    def key_rows(c, n=tq):
        return pl.ds(pl.multiple_of(c * tq, tq), n)

    def scores(c, s_ref, p_ref):
        kc = k_ref[0, key_rows(c), :]
        for st in range(2):
            q = q_ref[0, st * half:(st + 1) * half, :]
            s = lax.dot_general(q, kc, (((1,), (1,)), ((), ())), preferred_element_type=F32)
            s_ref[st] = s
            pm = s[:, :LANES]
            for t in range(1, tq // LANES):
                pm = jnp.maximum(pm, s[:, t * LANES:(t + 1) * LANES])
            p_ref[st] = pm

    def update(st, s, row_max, vc):
        m_prev = m_ref[st]
        m_new = jnp.maximum(m_prev, row_max)
        alpha = jnp.exp2(m_prev - m_new)
        pr = jnp.exp2(s - m_new).astype(BF16)
        acc_ref[st] = alpha * acc_ref[st] + jnp.dot(pr, vc, preferred_element_type=F32)
        m_ref[st] = m_new

    def accumulate(c, s_ref, p_ref):
        vc = v_ref[0, key_rows(c), :]
        for st in range(2):
            update(st, s_ref[st], jnp.max(p_ref[st], axis=-1, keepdims=True), vc)

    def accumulate_own(s_ref):
        s0 = s_ref[0, :, :half]
        rows = lax.broadcasted_iota(jnp.int32, s0.shape, 0)
        cols = lax.broadcasted_iota(jnp.int32, s0.shape, 1)
        s0 = jnp.where(cols <= rows, s0, NEG)
        update(0, s0, jnp.max(s0, axis=-1, keepdims=True), v_ref[0, key_rows(i, half), :])
        s1 = s_ref[1]
        rows = lax.broadcasted_iota(jnp.int32, s1.shape, 0)
        cols = lax.broadcasted_iota(jnp.int32, s1.shape, 1)
        s1 = jnp.where(cols <= rows + half, s1, NEG)
        update(1, s1, jnp.max(s1, axis=-1, keepdims=True), v_ref[0, key_rows(i), :])

    scores(0, sa_ref, pa_ref)

    def pair(jj, carry):
        c = 2 * jj
        scores(c + 1, sb_ref, pb_ref)
        accumulate(c, sa_ref, pa_ref)
        scores(c + 2, sa_ref, pa_ref)
        accumulate(c + 1, sb_ref, pb_ref)
        return carry

    lax.fori_loop(0, lax.div(i, 2), pair, 0)

    @pl.when(lax.rem(i, 2) == 1)
    def _():
        scores(i, sb_ref, pb_ref)
        accumulate(i - 1, sa_ref, pa_ref)
        accumulate_own(sb_ref)

    @pl.when(lax.rem(i, 2) == 0)
    def _():
        accumulate_own(sa_ref)

    for st in range(2):
        acc = acc_ref[st]
        o_ref[st * half:(st + 1) * half, :] = (acc[:, :HEAD_DIM] / acc[:, HEAD_DIM:]).astype(BF16)


def _flash(q, k, v, *, tq=1024):
    _, s, _ = q.shape
    return pl.pallas_call(
        functools.partial(_flash_kernel, tq=tq),
        grid=(N_HEADS, s // tq),
        in_specs=[
            pl.BlockSpec((1, tq, MLA_QK_PAD), lambda h, i: (h, i, 0)),
            pl.BlockSpec((1, s, MLA_QK_PAD), lambda h, i: (h, 0, 0)),
            pl.BlockSpec((1, s, 2 * HEAD_DIM), lambda h, i: (h, 0, 0)),
        ],
        out_specs=pl.BlockSpec((tq, HEAD_DIM), lambda h, i: (i, h)),
        out_shape=jax.ShapeDtypeStruct((s, N_HEADS * HEAD_DIM), BF16),
        scratch_shapes=[
            pltpu.VMEM((2, tq // 2, 1), F32),
            pltpu.VMEM((2, tq // 2, 2 * HEAD_DIM), F32),
            pltpu.VMEM((2, tq // 2, tq), F32),
            pltpu.VMEM((2, tq // 2, tq), F32),
            pltpu.VMEM((2, tq // 2, LANES), F32),
            pltpu.VMEM((2, tq // 2, LANES), F32),
        ],
        compiler_params=_params("parallel", "arbitrary"),
        name="mla_flash",
    )(q, k, v)


def _dil_kernel(q_ref, kc_ref, kp_ref, vc_ref, vp_ref, o_ref, o_sc, l_sc):
    t = q_ref.shape[1]
    blk = DIL_SPAN
    n = pl.program_id(0)

    qi = lax.broadcasted_iota(jnp.int32, (blk, 2 * blk), 0)
    kj = lax.broadcasted_iota(jnp.int32, (blk, 2 * blk), 1)
    diff = kj - qi
    band = (diff >= 0) & (diff <= DIL_SPAN)
    band_first = band & (kj >= jnp.where(n == 0, blk, 0))

    def rows(ref, start, d):
        idx = pl.ds(start, blk) if d == 1 else pl.ds(start, blk, stride=d)
        return ref[0, idx, :].astype(BF16)

    for idx, (window, d) in enumerate(DIL_PATTERNS):
        seg = blk * d
        for it in range(t // blk):
            r, b = it % d, it // d
            start = b * seg + r
            q = rows(q_ref, start, d)
            if b == 0:
                kp, vp = rows(kp_ref, t - seg + r, d), rows(vp_ref, t - seg + r, d)
            else:
                kp, vp = rows(kc_ref, start - seg, d), rows(vc_ref, start - seg, d)
            k = jnp.concatenate([kp, rows(kc_ref, start, d)], axis=0)
            v = jnp.concatenate([vp, rows(vc_ref, start, d)], axis=0)
            s = lax.dot_general(q, k, (((1,), (1,)), ((), ())), preferred_element_type=F32)
            s = jnp.where(band_first if b == 0 else band, s, NEG)
            m = jnp.max(s, axis=-1, keepdims=True)
            pr = jnp.exp(s - m)
            l = jnp.sum(pr, axis=-1, keepdims=True)
            o = jnp.dot(pr.astype(BF16), v, preferred_element_type=F32) / l
            lse = jnp.broadcast_to(m + jnp.log(l), (blk, LANES))
            out_rows = pl.ds(start, blk) if d == 1 else pl.ds(start, blk, stride=d)
            o_sc[idx, out_rows, :] = o
            l_sc[idx, out_rows, :] = lse

    lses = [l_sc[p] for p in range(len(DIL_PATTERNS))]
    mx = functools.reduce(jnp.maximum, lses)
    ws = [jnp.exp(l - mx) for l in lses]
    num = sum(w * o_sc[p] for p, w in enumerate(ws))
    o_ref[...] = (num / sum(ws)).astype(BF16)


def _dilated(qkv):
    _, s, width = qkv.shape
    t = DIL_TILE
    npat = len(DIL_PATTERNS)

    def cur(which):
        return pl.BlockSpec((1, t, HEAD_DIM), lambda n, h: (which, n, h))

    def prv(which):
        return pl.BlockSpec((1, t, HEAD_DIM), lambda n, h: (which, jnp.maximum(n - 1, 0), h))

    return pl.pallas_call(
        _dil_kernel,
        grid=(s // t, N_HEADS),
        in_specs=[cur(0), cur(1), prv(1), cur(2), prv(2)],
        out_specs=pl.BlockSpec((t, HEAD_DIM), lambda n, h: (n, h)),
        out_shape=jax.ShapeDtypeStruct((s, width), BF16),
        scratch_shapes=[
            pltpu.VMEM((npat, t, HEAD_DIM), F32),
            pltpu.VMEM((npat, t, LANES), F32),
        ],
        compiler_params=_params("parallel", "parallel"),
        name="dilated",
    )(qkv, qkv, qkv, qkv, qkv)


def _oproj_kernel(x_ref, oa_ref, ob_ref, w_ref, g_ref, o_ref):
    half = oa_ref.shape[1]
    y = jnp.dot(oa_ref[...], w_ref[0, :half], preferred_element_type=F32)
    y = y + jnp.dot(ob_ref[...], w_ref[0, half:], preferred_element_type=F32)
    o_ref[...] = x_ref[...] + _rms(y, g_ref[0])


def _oproj(x, oa, ob, wo, g, layer, *, tm=512):
    s, d = x.shape
    width = oa.shape[1]
    return pl.pallas_call(
        _oproj_kernel,
        grid=(s // tm,),
        in_specs=[
            pl.BlockSpec((tm, d), lambda m: (m, 0)),
            pl.BlockSpec((tm, width), lambda m: (m, 0)),
            pl.BlockSpec((tm, width), lambda m: (m, 0)),
            pl.BlockSpec((1,) + wo.shape[1:], lambda m: (layer, 0, 0)),
            pl.BlockSpec((1, 1, d), lambda m: (layer, 0, 0)),
        ],
        out_specs=pl.BlockSpec((tm, d), lambda m: (m, 0)),
        out_shape=jax.ShapeDtypeStruct((s, d), F32),
        compiler_params=_params("parallel"),
        name="oproj",
    )(x, oa, ob, wo, g)


def _rope_tables(pos, dim):
    half = dim // 2
    inv = ROPE_THETA ** (-jnp.arange(0, dim, 2, dtype=F32) / dim)
    ang = pos.astype(F32)[:, None] * inv
    cos, sin = jnp.cos(ang), jnp.sin(ang)
    n = pos.shape[0]
    c = jnp.concatenate([cos, cos, jnp.ones((n, LANES - dim), F32)], axis=1)
    a = jnp.concatenate([-sin, jnp.zeros((n, LANES - half), F32)], axis=1)
    b = jnp.concatenate([jnp.zeros((n, half), F32), sin, jnp.zeros((n, LANES - dim), F32)], axis=1)
    return c, a, b


def kernel(x, positions, ffn1_pre_g, ffn1_post_g, ffn1_w_gate, ffn1_w_up, ffn1_w_down,
           mix_pre_g, mix_post_g, w_in, mla_q_norm_g, mla_w_uq, mla_kv_norm_g, mla_w_ukv,
           w_o, ffn2_pre_g, ffn2_post_g, ffn2_w_gate, ffn2_w_up, ffn2_w_down):
    batch, _, _ = x.shape
    depth = w_in.shape[0]
    bf = lambda w: w.astype(BF16)

    lat_cols = 2 * MLA_RANK
    w_in = bf(w_in)
    wc = w_in[:, :, :lat_cols]
    wkr = jnp.pad(w_in[:, :, lat_cols:lat_cols + MLA_ROPE], ((0, 0), (0, 0), (0, LANES - MLA_ROPE)))
    wd = w_in[:, :, lat_cols + MLA_ROPE:]
    wuq = bf(mla_w_uq).reshape(depth, MLA_RANK, N_HEADS, MLA_QK)
    wuq = jnp.pad(wuq, ((0, 0), (0, 0), (0, 0), (0, MLA_QK_PAD - MLA_QK))
                  ).reshape(depth, MLA_RANK, N_HEADS * MLA_QK_PAD)
    wukv = bf(mla_w_ukv)
    wo = bf(w_o)
    ffn_f32 = ((ffn1_w_gate, ffn1_w_up, ffn1_w_down), (ffn2_w_gate, ffn2_w_up, ffn2_w_down))
    ffn_gains = ((ffn1_pre_g, ffn1_post_g), (ffn2_pre_g, ffn2_post_g))
    ffn_bf16 = {(0, 0): tuple(bf(w[:1]) for w in ffn_f32[0])}

    def ffn(xs, l, which):
        nxt = (l, 1) if which == 0 else (l + 1, 0)
        cast_next = None
        if nxt[0] < depth and nxt not in ffn_bf16:
            cast_next = ffn_f32[nxt[1]] + (nxt[0],)
        gpre, gpost = (g[l][None, None, :] for g in ffn_gains[which])
        xs, converted = _ffn(xs, gpre, gpost, ffn_bf16[(l, which)], 0, cast_next)
        if cast_next is not None:
            ffn_bf16[nxt] = converted
        return xs
    row = lambda g: g[:, None, :]
    mix_pre_g, mix_post_g, mla_q_norm_g, mla_kv_norm_g = map(
        row, (mix_pre_g, mix_post_g, mla_q_norm_g, mla_kv_norm_g))

    outs = []
    for bi in range(batch):
        xs = x[bi]
        tabs_a = _rope_tables(positions[bi], MLA_ROPE)
        cp, ap, bp = _rope_tables(positions[bi], PART_ROPE)
        qscale = 1.0 / math.sqrt(HEAD_DIM)
        tabs_p = (jnp.stack([cp * qscale, cp, jnp.ones_like(cp)]),
                  jnp.stack([ap * qscale, ap, jnp.zeros_like(ap)]),
                  jnp.stack([bp * qscale, bp, jnp.zeros_like(bp)]))
        for l in range(depth):
            xs = ffn(xs, l, 0)
            qkv = _dproj(xs, mix_pre_g, wd, tabs_p, l)
            qm, km, vm = _mproj(xs, mix_pre_g, wc, wkr, mla_q_norm_g, wuq,
                                mla_kv_norm_g, wukv, tabs_a, l)
            oa = _flash(qm, km, vm)
            ob = _dilated(qkv)
            xs = _oproj(xs, oa, ob, wo, mix_post_g, l)
            xs = ffn(xs, l, 1)
        outs.append(xs)
    return jnp.stack(outs, axis=0)
```

```python
import functools
import math

import jax
import jax.numpy as jnp
from jax import lax
from jax.experimental import pallas as pl
from jax.experimental.pallas import tpu as pltpu

F32 = jnp.float32
BF16 = jnp.bfloat16

HEAD_DIM = 128
N_HEADS = 8
MLA_RANK = 512
MLA_NOPE = 128
MLA_ROPE = 64
MLA_QK = MLA_NOPE + MLA_ROPE
MLA_QK_PAD = 256
PART_ROPE = 32
DIL_PATTERNS = ((128, 1), (512, 4), (2048, 16))
DIL_SPAN = 128
DIL_TILE = 2048
ROPE_THETA = 500000.0
RMS_EPS = 1e-6
NEG = -1e30
LANES = 128
VMEM_LIMIT = 58 * 1024 * 1024
FFN_NORM_ROWS = 256
FFN_DOT_ROWS = 512


def _rms(xf, g):
    ms = jnp.mean(xf * xf, axis=-1, keepdims=True)
    return xf * lax.rsqrt(ms + RMS_EPS) * g


def _rope(y, c, a, b, half):
    return y * c + pltpu.roll(y, LANES - half, axis=1) * a + pltpu.roll(y, half, axis=1) * b


def _params(*sem):
    return pltpu.CompilerParams(dimension_semantics=sem, vmem_limit_bytes=VMEM_LIMIT)


def _ffn_kernel(*refs, cast_next):
    x_ref, gpre_ref, gpost_ref, wg_ref, wu_ref, wd_ref = refs[:6]
    if cast_next:
        o_ref, nb_ref, r_ref = refs[9], refs[13], refs[14]
        for src_ref, dst_ref in zip(refs[6:9], refs[10:13]):
            dst_ref[...] = src_ref[...].astype(BF16)
    else:
        o_ref, nb_ref, r_ref = refs[6:]
    j = pl.program_id(1)

    tm = x_ref.shape[0]

    def row_chunks(size):
        return [slice(r, r + size) for r in range(0, tm, size)]

    @pl.when(j == 0)
    def _():
        for rows in row_chunks(FFN_NORM_ROWS):
            nb_ref[rows] = _rms(x_ref[rows], gpre_ref[0]).astype(BF16)
        o_ref[...] = jnp.zeros_like(o_ref)

    for rows in row_chunks(FFN_DOT_ROWS):
        nb = nb_ref[rows]
        g = jnp.dot(nb, wg_ref[0], preferred_element_type=F32)
        u = jnp.dot(nb, wu_ref[0], preferred_element_type=F32)
        h = (g * jax.nn.sigmoid(g) * u).astype(BF16)
        o_ref[rows] += jnp.dot(h, wd_ref[0], preferred_element_type=F32)

    @pl.when(j == pl.num_programs(1) - 1)
    def _():
        for rows in row_chunks(FFN_NORM_ROWS):
            y = o_ref[rows]
            r_ref[rows] = 0.5 * lax.rsqrt(jnp.mean(y * y, axis=-1, keepdims=True) + RMS_EPS)
        for rows in row_chunks(FFN_NORM_ROWS):
            o_ref[rows] = x_ref[rows] + o_ref[rows] * r_ref[rows] * gpost_ref[0]


def _ffn(x, gpre, gpost, weights, layer, cast_next=None, *, tm=1024, tn=512):
    s, d = x.shape
    wg, wu, wd = weights
    dff = wg.shape[-1]
    nm = s // tm
    dm = d // nm
    in_specs = [
        pl.BlockSpec((tm, d), lambda m, j: (m, 0)),
        pl.BlockSpec((1, 1, d), lambda m, j: (layer, 0, 0)),
        pl.BlockSpec((1, 1, d), lambda m, j: (layer, 0, 0)),
        pl.BlockSpec((1, d, tn), lambda m, j: (layer, 0, j)),
        pl.BlockSpec((1, d, tn), lambda m, j: (layer, 0, j)),
        pl.BlockSpec((1, tn, d), lambda m, j: (layer, j, 0)),
    ]
    out_specs = [pl.BlockSpec((tm, d), lambda m, j: (m, 0))]
    out_shape = [jax.ShapeDtypeStruct((s, d), F32)]
    args = [x, gpre, gpost, wg, wu, wd]
    if cast_next is not None:
        *nxt, nl = cast_next
        in_specs += [
            pl.BlockSpec((1, dm, tn), lambda m, j: (nl, m, j)),
            pl.BlockSpec((1, dm, tn), lambda m, j: (nl, m, j)),
            pl.BlockSpec((1, tn, dm), lambda m, j: (nl, j, m)),
        ]
        out_specs += [
            pl.BlockSpec((1, dm, tn), lambda m, j: (0, m, j)),
            pl.BlockSpec((1, dm, tn), lambda m, j: (0, m, j)),
            pl.BlockSpec((1, tn, dm), lambda m, j: (0, j, m)),
        ]
        out_shape += [jax.ShapeDtypeStruct((1,) + w.shape[1:], BF16) for w in nxt]
        args += nxt
    out = pl.pallas_call(
        functools.partial(_ffn_kernel, cast_next=cast_next is not None),
        grid=(nm, dff // tn),
        in_specs=in_specs,
        out_specs=out_specs,
        out_shape=out_shape,
        scratch_shapes=[pltpu.VMEM((tm, d), BF16), pltpu.VMEM((tm, 1), F32)],
        compiler_params=_params("parallel", "arbitrary"),
        name="ffn",
    )(*args)
    return out[0], tuple(out[1:])


def _dproj_kernel(x_ref, g_ref, w_ref, c_ref, a_ref, b_ref, o_ref, nb_ref):
    j = pl.program_id(1)

    @pl.when(j == 0)
    def _():
        for r in range(0, x_ref.shape[0], FFN_NORM_ROWS):
            rows = slice(r, r + FFN_NORM_ROWS)
            nb_ref[rows] = _rms(x_ref[rows], g_ref[0]).astype(BF16)

    y = jnp.dot(nb_ref[...], w_ref[0], preferred_element_type=F32)
    c, a, b = c_ref[0], a_ref[0], b_ref[0]
    for h in range(N_HEADS):
        sl = slice(h * HEAD_DIM, (h + 1) * HEAD_DIM)
        o_ref[0, :, sl] = _rope(y[:, sl], c, a, b, PART_ROPE // 2)


def _dproj(x, g, wd, tabs, layer, *, tm=1024):
    s, d = x.shape
    width = N_HEADS * HEAD_DIM
    tab_spec = pl.BlockSpec((1, tm, LANES), lambda m, j: (j, m, 0))
    return pl.pallas_call(
        _dproj_kernel,
        grid=(s // tm, 3),
        in_specs=[
            pl.BlockSpec((tm, d), lambda m, j: (m, 0)),
            pl.BlockSpec((1, 1, d), lambda m, j: (layer, 0, 0)),
            pl.BlockSpec((1, d, width), lambda m, j: (layer, 0, j)),
            tab_spec, tab_spec, tab_spec,
        ],
        out_specs=pl.BlockSpec((1, tm, width), lambda m, j: (j, m, 0)),
        out_shape=jax.ShapeDtypeStruct((3, s, width), F32),
        scratch_shapes=[pltpu.VMEM((tm, d), BF16)],
        compiler_params=_params("parallel", "arbitrary"),
        name="dproj",
    )(x, g, wd, *tabs)


def _mproj_kernel(x_ref, g_ref, wc_ref, wkr_ref, gq_ref, wuq_ref, gkv_ref, wukv_ref,
                  c_ref, a_ref, b_ref, q_ref, k_ref, v_ref):
    nb = _rms(x_ref[...], g_ref[0]).astype(BF16)
    lat = jnp.dot(nb, wc_ref[0], preferred_element_type=F32)
    cq = _rms(lat[:, :MLA_RANK], gq_ref[0]).astype(BF16)
    ckv = _rms(lat[:, MLA_RANK:], gkv_ref[0]).astype(BF16)
    c, a, b = c_ref[...], a_ref[...], b_ref[...]
    half = MLA_ROPE // 2
    kr = jnp.dot(nb, wkr_ref[0], preferred_element_type=F32)
    kr = _rope(kr, c, a, b, half).astype(BF16)
    q = jnp.dot(cq, wuq_ref[0], preferred_element_type=F32)
    kv = jnp.dot(ckv, wukv_ref[0], preferred_element_type=F32)
    scale = math.log2(math.e) / math.sqrt(MLA_QK)
    ones = jnp.ones((x_ref.shape[0], HEAD_DIM), BF16)
    for h in range(N_HEADS):
        lo = h * MLA_QK_PAD
        q_ref[h, :, :MLA_NOPE] = (q[:, lo:lo + MLA_NOPE] * scale).astype(BF16)
        qr = _rope(q[:, lo + MLA_NOPE:lo + MLA_QK_PAD], c, a, b, half)
        q_ref[h, :, MLA_NOPE:] = (qr * scale).astype(BF16)
        k_ref[h, :, :MLA_NOPE] = kv[:, lo:lo + MLA_NOPE].astype(BF16)
        k_ref[h, :, MLA_NOPE:] = kr
        v_ref[h, :, :HEAD_DIM] = kv[:, lo + MLA_NOPE:lo + MLA_QK_PAD].astype(BF16)
        v_ref[h, :, HEAD_DIM:] = ones


def _mproj(x, g, wc, wkr, gq, wuq, gkv, wukv, tabs, layer, *, tm=512):
    s, d = x.shape
    tab_spec = pl.BlockSpec((tm, LANES), lambda m: (m, 0))

    def wspec(w):
        return pl.BlockSpec((1,) + w.shape[1:], lambda m: (layer, 0, 0))

    def gspec(gv):
        return pl.BlockSpec((1, 1, gv.shape[2]), lambda m: (layer, 0, 0))

    return pl.pallas_call(
        _mproj_kernel,
        grid=(s // tm,),
        in_specs=[
            pl.BlockSpec((tm, d), lambda m: (m, 0)),
            gspec(g), wspec(wc), wspec(wkr), gspec(gq), wspec(wuq), gspec(gkv), wspec(wukv),
            tab_spec, tab_spec, tab_spec,
        ],
        out_specs=[
            pl.BlockSpec((N_HEADS, tm, MLA_QK_PAD), lambda m: (0, m, 0)),
            pl.BlockSpec((N_HEADS, tm, MLA_QK_PAD), lambda m: (0, m, 0)),
            pl.BlockSpec((N_HEADS, tm, 2 * HEAD_DIM), lambda m: (0, m, 0)),
        ],
        out_shape=[
            jax.ShapeDtypeStruct((N_HEADS, s, MLA_QK_PAD), BF16),
            jax.ShapeDtypeStruct((N_HEADS, s, MLA_QK_PAD), BF16),
            jax.ShapeDtypeStruct((N_HEADS, s, 2 * HEAD_DIM), BF16),
        ],
        compiler_params=_params("parallel"),
        name="mproj",
    )(x, g, wc, wkr, gq, wuq, gkv, wukv, *tabs)


def _flash_kernel(q_ref, k_ref, v_ref, o_ref, m_ref, acc_ref,
                  sa_ref, sb_ref, sc_ref, pa_ref, pb_ref, pc_ref, *, tq):
    half = tq // 2
    i = pl.program_id(1)
    last = pl.num_programs(1) - 1
    m_ref[...] = jnp.full_like(m_ref, NEG)
    acc_ref[...] = jnp.zeros_like(acc_ref)

    def key_rows(c, n=tq):
        return pl.ds(pl.multiple_of(c * tq, tq), n)

    def scores(c, s_ref, p_ref, tile=i):
        kc = k_ref[0, key_rows(c), :]
        for st in range(2):
            q = q_ref[0, pl.ds(pl.multiple_of(tile * tq + st * half, half), half), :]
            s = lax.dot_general(q, kc, (((1,), (1,)), ((), ())), preferred_element_type=F32)
            s_ref[st] = s
            pm = s[:, :LANES]
            for t in range(1, tq // LANES):
                pm = jnp.maximum(pm, s[:, t * LANES:(t + 1) * LANES])
            p_ref[st] = pm

    def update(st, s, row_max, vc):
        m_prev = m_ref[st]
        m_new = jnp.maximum(m_prev, row_max)
        alpha = jnp.exp2(m_prev - m_new)
        pr = jnp.exp2(s - m_new).astype(BF16)
        acc_ref[st] = alpha * acc_ref[st] + jnp.dot(pr, vc, preferred_element_type=F32)
        m_ref[st] = m_new

    def accumulate(c, s_ref, p_ref):
        vc = v_ref[0, key_rows(c), :]
        for st in range(2):
            update(st, s_ref[st], jnp.max(p_ref[st], axis=-1, keepdims=True), vc)

    def finish(s_ref):
        scores(0, sc_ref, pc_ref, tile=jnp.minimum(i + 1, last))
        s0 = s_ref[0, :, :half]
        rows = lax.broadcasted_iota(jnp.int32, s0.shape, 0)
        cols = lax.broadcasted_iota(jnp.int32, s0.shape, 1)
        s0 = jnp.where(cols <= rows, s0, NEG)
        update(0, s0, jnp.max(s0, axis=-1, keepdims=True), v_ref[0, key_rows(i, half), :])
        s1 = s_ref[1]
        rows = lax.broadcasted_iota(jnp.int32, s1.shape, 0)
        cols = lax.broadcasted_iota(jnp.int32, s1.shape, 1)
        s1 = jnp.where(cols <= rows + half, s1, NEG)
        update(1, s1, jnp.max(s1, axis=-1, keepdims=True), v_ref[0, key_rows(i), :])
        for st in range(2):
            acc = acc_ref[st]
            o_ref[st * half:(st + 1) * half, :] = (
                acc[:, :HEAD_DIM] / acc[:, HEAD_DIM:]).astype(BF16)

    @pl.when(i == 0)
    def _():
        scores(0, sa_ref, pa_ref)
        finish(sa_ref)

    @pl.when(i > 0)
    def _():
        scores(1, sa_ref, pa_ref)
        accumulate(0, sc_ref, pc_ref)

    def pair(jj, carry):
        c = 2 * jj + 1
        scores(c + 1, sb_ref, pb_ref)
        accumulate(c, sa_ref, pa_ref)
        scores(c + 2, sa_ref, pa_ref)
        accumulate(c + 1, sb_ref, pb_ref)
        return carry

    lax.fori_loop(0, lax.div(jnp.maximum(i - 1, 0), 2), pair, 0)

    @pl.when(lax.rem(i, 2) == 1)
    def _():
        finish(sa_ref)

    @pl.when(jnp.logical_and(lax.rem(i, 2) == 0, i > 0))
    def _():
        scores(i, sb_ref, pb_ref)
        accumulate(i - 1, sa_ref, pa_ref)
        finish(sb_ref)


def _flash(q, k, v, *, tq=1024):
    _, s, _ = q.shape
    score_buf = pltpu.VMEM((2, tq // 2, tq), F32)
    pmax_buf = pltpu.VMEM((2, tq // 2, LANES), F32)
    return pl.pallas_call(
        functools.partial(_flash_kernel, tq=tq),
        grid=(N_HEADS, s // tq),
        in_specs=[
            pl.BlockSpec((1, s, MLA_QK_PAD), lambda h, i: (h, 0, 0)),
            pl.BlockSpec((1, s, MLA_QK_PAD), lambda h, i: (h, 0, 0)),
            pl.BlockSpec((1, s, 2 * HEAD_DIM), lambda h, i: (h, 0, 0)),
        ],
        out_specs=pl.BlockSpec((tq, HEAD_DIM), lambda h, i: (i, h)),
        out_shape=jax.ShapeDtypeStruct((s, N_HEADS * HEAD_DIM), BF16),
        scratch_shapes=[
            pltpu.VMEM((2, tq // 2, 1), F32),
            pltpu.VMEM((2, tq // 2, 2 * HEAD_DIM), F32),
            score_buf, score_buf, score_buf, pmax_buf, pmax_buf, pmax_buf,
        ],
        compiler_params=_params("arbitrary", "arbitrary"),
        name="mla_flash",
    )(q, k, v)


def _dil_kernel(q_ref, kc_ref, kp_ref, vc_ref, vp_ref, o_ref, o_sc, l_sc):
    t = q_ref.shape[1]
    blk = DIL_SPAN
    n = pl.program_id(0)

    qi = lax.broadcasted_iota(jnp.int32, (blk, 2 * blk), 0)
    kj = lax.broadcasted_iota(jnp.int32, (blk, 2 * blk), 1)
    diff = kj - qi
    band = (diff >= 0) & (diff <= DIL_SPAN)
    band_first = band & (kj >= jnp.where(n == 0, blk, 0))

    ones = jnp.ones((2 * blk, HEAD_DIM), BF16)

    def rows(ref, start, d):
        idx = pl.ds(start, blk) if d == 1 else pl.ds(start, blk, stride=d)
        return ref[0, idx, :].astype(BF16)

    for idx, (window, d) in enumerate(DIL_PATTERNS):
        seg = blk * d
        for it in range(t // blk):
            r, b = it % d, it // d
            start = b * seg + r
            q = rows(q_ref, start, d)
            if b == 0:
                kp, vp = rows(kp_ref, t - seg + r, d), rows(vp_ref, t - seg + r, d)
            else:
                kp, vp = rows(kc_ref, start - seg, d), rows(vc_ref, start - seg, d)
            k = jnp.concatenate([kp, rows(kc_ref, start, d)], axis=0)
            v = jnp.concatenate([vp, rows(vc_ref, start, d)], axis=0)
            v = jnp.concatenate([v, ones], axis=1)
            s = lax.dot_general(q, k, (((1,), (1,)), ((), ())), preferred_element_type=F32)
            s = jnp.where(band_first if b == 0 else band, s, NEG)
            m = jnp.max(s, axis=-1, keepdims=True)
            pr = jnp.exp2(s - m)
            ol = jnp.dot(pr.astype(BF16), v, preferred_element_type=F32)
            l = ol[:, HEAD_DIM:]
            out_rows = pl.ds(start, blk) if d == 1 else pl.ds(start, blk, stride=d)
            o_sc[idx, out_rows, :] = ol[:, :HEAD_DIM] / l
            l_sc[idx, out_rows, :] = m + jnp.log2(l)

    lses = [l_sc[p] for p in range(len(DIL_PATTERNS))]
    mx = functools.reduce(jnp.maximum, lses)
    ws = [jnp.exp2(l - mx) for l in lses]
    num = sum(w * o_sc[p] for p, w in enumerate(ws))
    o_ref[...] = (num / sum(ws)).astype(BF16)


def _dilated(qkv):
    _, s, width = qkv.shape
    t = DIL_TILE
    npat = len(DIL_PATTERNS)

    def cur(which):
        return pl.BlockSpec((1, t, HEAD_DIM), lambda n, h: (which, n, h))

    def prv(which):
        return pl.BlockSpec((1, t, HEAD_DIM), lambda n, h: (which, jnp.maximum(n - 1, 0), h))

    return pl.pallas_call(
        _dil_kernel,
        grid=(s // t, N_HEADS),
        in_specs=[cur(0), cur(1), prv(1), cur(2), prv(2)],
        out_specs=pl.BlockSpec((t, HEAD_DIM), lambda n, h: (n, h)),
        out_shape=jax.ShapeDtypeStruct((s, width), BF16),
        scratch_shapes=[
            pltpu.VMEM((npat, t, HEAD_DIM), F32),
            pltpu.VMEM((npat, t, LANES), F32),
        ],
        compiler_params=_params("parallel", "parallel"),
        name="dilated",
    )(qkv, qkv, qkv, qkv, qkv)


def _oproj_kernel(x_ref, oa_ref, ob_ref, w_ref, g_ref, o_ref):
    half = oa_ref.shape[1]
    y = jnp.dot(oa_ref[...], w_ref[0, :half], preferred_element_type=F32)
    y = y + jnp.dot(ob_ref[...], w_ref[0, half:], preferred_element_type=F32)
    o_ref[...] = x_ref[...] + _rms(y, g_ref[0])


def _oproj(x, oa, ob, wo, g, layer, *, tm=512):
    s, d = x.shape
    width = oa.shape[1]
    return pl.pallas_call(
        _oproj_kernel,
        grid=(s // tm,),
        in_specs=[
            pl.BlockSpec((tm, d), lambda m: (m, 0)),
            pl.BlockSpec((tm, width), lambda m: (m, 0)),
            pl.BlockSpec((tm, width), lambda m: (m, 0)),
            pl.BlockSpec((1,) + wo.shape[1:], lambda m: (layer, 0, 0)),
            pl.BlockSpec((1, 1, d), lambda m: (layer, 0, 0)),
        ],
        out_specs=pl.BlockSpec((tm, d), lambda m: (m, 0)),
        out_shape=jax.ShapeDtypeStruct((s, d), F32),
        compiler_params=_params("parallel"),
        name="oproj",
    )(x, oa, ob, wo, g)


def _rope_tables(pos, dim):
    half = dim // 2
    inv = ROPE_THETA ** (-jnp.arange(0, dim, 2, dtype=F32) / dim)
    ang = pos.astype(F32)[:, None] * inv
    cos, sin = jnp.cos(ang), jnp.sin(ang)
    n = pos.shape[0]
    c = jnp.concatenate([cos, cos, jnp.ones((n, LANES - dim), F32)], axis=1)
    a = jnp.concatenate([-sin, jnp.zeros((n, LANES - half), F32)], axis=1)
    b = jnp.concatenate([jnp.zeros((n, half), F32), sin, jnp.zeros((n, LANES - dim), F32)], axis=1)
    return c, a, b


def kernel(x, positions, ffn1_pre_g, ffn1_post_g, ffn1_w_gate, ffn1_w_up, ffn1_w_down,
           mix_pre_g, mix_post_g, w_in, mla_q_norm_g, mla_w_uq, mla_kv_norm_g, mla_w_ukv,
           w_o, ffn2_pre_g, ffn2_post_g, ffn2_w_gate, ffn2_w_up, ffn2_w_down):
    batch, _, _ = x.shape
    depth = w_in.shape[0]
    bf = lambda w: w.astype(BF16)

    lat_cols = 2 * MLA_RANK
    w_in = bf(w_in)
    wc = w_in[:, :, :lat_cols]
    wkr = jnp.pad(w_in[:, :, lat_cols:lat_cols + MLA_ROPE], ((0, 0), (0, 0), (0, LANES - MLA_ROPE)))
    wd = w_in[:, :, lat_cols + MLA_ROPE:]
    wuq = bf(mla_w_uq).reshape(depth, MLA_RANK, N_HEADS, MLA_QK)
    wuq = jnp.pad(wuq, ((0, 0), (0, 0), (0, 0), (0, MLA_QK_PAD - MLA_QK))
                  ).reshape(depth, MLA_RANK, N_HEADS * MLA_QK_PAD)
    wukv = bf(mla_w_ukv)
    wo = bf(w_o)
    ffn_f32 = ((ffn1_w_gate, ffn1_w_up, ffn1_w_down), (ffn2_w_gate, ffn2_w_up, ffn2_w_down))
    ffn_gains = ((ffn1_pre_g, ffn1_post_g), (ffn2_pre_g, ffn2_post_g))
    ffn_bf16 = {(0, 0): tuple(bf(w[:1]) for w in ffn_f32[0])}

    def ffn(xs, l, which):
        nxt = (l, 1) if which == 0 else (l + 1, 0)
        cast_next = None
        if nxt[0] < depth and nxt not in ffn_bf16:
            cast_next = ffn_f32[nxt[1]] + (nxt[0],)
        gpre, gpost = (g[l][None, None, :] for g in ffn_gains[which])
        xs, converted = _ffn(xs, gpre, gpost, ffn_bf16[(l, which)], 0, cast_next)
        if cast_next is not None:
            ffn_bf16[nxt] = converted
        return xs
    row = lambda g: g[:, None, :]
    mix_pre_g, mix_post_g, mla_q_norm_g, mla_kv_norm_g = map(
        row, (mix_pre_g, mix_post_g, mla_q_norm_g, mla_kv_norm_g))

    outs = []
    for bi in range(batch):
        xs = x[bi]
        tabs_a = _rope_tables(positions[bi], MLA_ROPE)
        cp, ap, bp = _rope_tables(positions[bi], PART_ROPE)
        qscale = math.log2(math.e) / math.sqrt(HEAD_DIM)
        tabs_p = (jnp.stack([cp * qscale, cp, jnp.ones_like(cp)]),
                  jnp.stack([ap * qscale, ap, jnp.zeros_like(ap)]),
                  jnp.stack([bp * qscale, bp, jnp.zeros_like(bp)]))
        for l in range(depth):
            xs = ffn(xs, l, 0)
            qkv = _dproj(xs, mix_pre_g, wd, tabs_p, l)
            qm, km, vm = _mproj(xs, mix_pre_g, wc, wkr, mla_q_norm_g, wuq,
                                mla_kv_norm_g, wukv, tabs_a, l)
            oa = _flash(qm, km, vm)
            ob = _dilated(qkv)
            xs = _oproj(xs, oa, ob, wo, mix_post_g, l)
            xs = ffn(xs, l, 1)
        outs.append(xs)
    return jnp.stack(outs, axis=0)
```

```python
import functools
import math

import jax
import jax.numpy as jnp
from jax import lax
from jax.experimental import pallas as pl
from jax.experimental.pallas import tpu as pltpu

F32 = jnp.float32
BF16 = jnp.bfloat16

HEAD_DIM = 128
N_HEADS = 8
MLA_RANK = 512
MLA_NOPE = 128
MLA_ROPE = 64
MLA_QK = MLA_NOPE + MLA_ROPE
MLA_QK_PAD = 256
PART_ROPE = 32
DIL_PATTERNS = ((128, 1), (512, 4), (2048, 16))
DIL_SPAN = 128
DIL_TILE = 2048
ROPE_THETA = 500000.0
RMS_EPS = 1e-6
NEG = -1e30
LANES = 128
VMEM_LIMIT = 58 * 1024 * 1024
FFN_NORM_ROWS = 256
FFN_DOT_ROWS = 512


def _rms(xf, g):
    ms = jnp.mean(xf * xf, axis=-1, keepdims=True)
    return xf * lax.rsqrt(ms + RMS_EPS) * g


def _rope(y, c, a, b, half):
    return y * c + pltpu.roll(y, LANES - half, axis=1) * a + pltpu.roll(y, half, axis=1) * b


def _params(*sem):
    return pltpu.CompilerParams(dimension_semantics=sem, vmem_limit_bytes=VMEM_LIMIT)


def _ffn_kernel(*refs, cast_next):
    x_ref, gpre_ref, gpost_ref, wg_ref, wu_ref, wd_ref = refs[:6]
    if cast_next:
        o_ref, nb_ref, r_ref = refs[9], refs[13], refs[14]
        for src_ref, dst_ref in zip(refs[6:9], refs[10:13]):
            dst_ref[...] = src_ref[...].astype(BF16)
    else:
        o_ref, nb_ref, r_ref = refs[6:]
    j = pl.program_id(1)

    tm = x_ref.shape[0]

    def row_chunks(size):
        return [slice(r, r + size) for r in range(0, tm, size)]

    @pl.when(j == 0)
    def _():
        for rows in row_chunks(FFN_NORM_ROWS):
            nb_ref[rows] = _rms(x_ref[rows], gpre_ref[0]).astype(BF16)
        o_ref[...] = jnp.zeros_like(o_ref)

    for rows in row_chunks(FFN_DOT_ROWS):
        nb = nb_ref[rows]
        g = jnp.dot(nb, wg_ref[0], preferred_element_type=F32)
        u = jnp.dot(nb, wu_ref[0], preferred_element_type=F32)
        h = (g * jax.nn.sigmoid(g) * u).astype(BF16)
        o_ref[rows] += jnp.dot(h, wd_ref[0], preferred_element_type=F32)

    @pl.when(j == pl.num_programs(1) - 1)
    def _():
        for rows in row_chunks(FFN_NORM_ROWS):
            y = o_ref[rows]
            r_ref[rows] = 0.5 * lax.rsqrt(jnp.mean(y * y, axis=-1, keepdims=True) + RMS_EPS)
        for rows in row_chunks(FFN_NORM_ROWS):
            o_ref[rows] = x_ref[rows] + o_ref[rows] * r_ref[rows] * gpost_ref[0]


def _ffn(x, gpre, gpost, weights, layer, cast_next=None, *, tm=1024, tn=512):
    s, d = x.shape
    wg, wu, wd = weights
    dff = wg.shape[-1]
    nm = s // tm
    dm = d // nm
    in_specs = [
        pl.BlockSpec((tm, d), lambda m, j: (m, 0)),
        pl.BlockSpec((1, 1, d), lambda m, j: (layer, 0, 0)),
        pl.BlockSpec((1, 1, d), lambda m, j: (layer, 0, 0)),
        pl.BlockSpec((1, d, tn), lambda m, j: (layer, 0, j)),
        pl.BlockSpec((1, d, tn), lambda m, j: (layer, 0, j)),
        pl.BlockSpec((1, tn, d), lambda m, j: (layer, j, 0)),
    ]
    out_specs = [pl.BlockSpec((tm, d), lambda m, j: (m, 0))]
    out_shape = [jax.ShapeDtypeStruct((s, d), F32)]
    args = [x, gpre, gpost, wg, wu, wd]
    if cast_next is not None:
        *nxt, nl = cast_next
        in_specs += [
            pl.BlockSpec((1, dm, tn), lambda m, j: (nl, m, j)),
            pl.BlockSpec((1, dm, tn), lambda m, j: (nl, m, j)),
            pl.BlockSpec((1, tn, dm), lambda m, j: (nl, j, m)),
        ]
        out_specs += [
            pl.BlockSpec((1, dm, tn), lambda m, j: (0, m, j)),
            pl.BlockSpec((1, dm, tn), lambda m, j: (0, m, j)),
            pl.BlockSpec((1, tn, dm), lambda m, j: (0, j, m)),
        ]
        out_shape += [jax.ShapeDtypeStruct((1,) + w.shape[1:], BF16) for w in nxt]
        args += nxt
    out = pl.pallas_call(
        functools.partial(_ffn_kernel, cast_next=cast_next is not None),
        grid=(nm, dff // tn),
        in_specs=in_specs,
        out_specs=out_specs,
        out_shape=out_shape,
        scratch_shapes=[pltpu.VMEM((tm, d), BF16), pltpu.VMEM((tm, 1), F32)],
        compiler_params=_params("parallel", "arbitrary"),
        name="ffn",
    )(*args)
    return out[0], tuple(out[1:])


def _dproj_kernel(x_ref, g_ref, w_ref, c_ref, a_ref, b_ref, o_ref, nb_ref):
    j = pl.program_id(1)

    @pl.when(j == 0)
    def _():
        for r in range(0, x_ref.shape[0], FFN_NORM_ROWS):
            rows = slice(r, r + FFN_NORM_ROWS)
            nb_ref[rows] = _rms(x_ref[rows], g_ref[0]).astype(BF16)

    y = jnp.dot(nb_ref[...], w_ref[0], preferred_element_type=F32)
    c, a, b = c_ref[0], a_ref[0], b_ref[0]
    for h in range(N_HEADS):
        sl = slice(h * HEAD_DIM, (h + 1) * HEAD_DIM)
        o_ref[0, :, sl] = _rope(y[:, sl], c, a, b, PART_ROPE // 2)


def _dproj(x, g, wd, tabs, layer, *, tm=1024):
    s, d = x.shape
    width = N_HEADS * HEAD_DIM
    tab_spec = pl.BlockSpec((1, tm, LANES), lambda m, j: (j, m, 0))
    return pl.pallas_call(
        _dproj_kernel,
        grid=(s // tm, 3),
        in_specs=[
            pl.BlockSpec((tm, d), lambda m, j: (m, 0)),
            pl.BlockSpec((1, 1, d), lambda m, j: (layer, 0, 0)),
            pl.BlockSpec((1, d, width), lambda m, j: (layer, 0, j)),
            tab_spec, tab_spec, tab_spec,
        ],
        out_specs=pl.BlockSpec((1, tm, width), lambda m, j: (j, m, 0)),
        out_shape=jax.ShapeDtypeStruct((3, s, width), F32),
        scratch_shapes=[pltpu.VMEM((tm, d), BF16)],
        compiler_params=_params("parallel", "arbitrary"),
        name="dproj",
    )(x, g, wd, *tabs)


def _mproj_kernel(x_ref, g_ref, wc_ref, wkr_ref, gq_ref, wuq_ref, gkv_ref, wukv_ref,
                  c_ref, a_ref, b_ref, q_ref, k_ref, v_ref):
    nb = _rms(x_ref[...], g_ref[0]).astype(BF16)
    lat = jnp.dot(nb, wc_ref[0], preferred_element_type=F32)
    cq = _rms(lat[:, :MLA_RANK], gq_ref[0]).astype(BF16)
    ckv = _rms(lat[:, MLA_RANK:], gkv_ref[0]).astype(BF16)
    c, a, b = c_ref[...], a_ref[...], b_ref[...]
    half = MLA_ROPE // 2
    kr = jnp.dot(nb, wkr_ref[0], preferred_element_type=F32)
    kr = _rope(kr, c, a, b, half).astype(BF16)
    q = jnp.dot(cq, wuq_ref[0], preferred_element_type=F32)
    kv = jnp.dot(ckv, wukv_ref[0], preferred_element_type=F32)
    scale = math.log2(math.e) / math.sqrt(MLA_QK)
    ones = jnp.ones((x_ref.shape[0], HEAD_DIM), BF16)
    for h in range(N_HEADS):
        lo = h * MLA_QK_PAD
        q_ref[h, :, :MLA_NOPE] = (q[:, lo:lo + MLA_NOPE] * scale).astype(BF16)
        qr = _rope(q[:, lo + MLA_NOPE:lo + MLA_QK_PAD], c, a, b, half)
        q_ref[h, :, MLA_NOPE:] = (qr * scale).astype(BF16)
        k_ref[h, :, :MLA_NOPE] = kv[:, lo:lo + MLA_NOPE].astype(BF16)
        k_ref[h, :, MLA_NOPE:] = kr
        v_ref[h, :, :HEAD_DIM] = kv[:, lo + MLA_NOPE:lo + MLA_QK_PAD].astype(BF16)
        v_ref[h, :, HEAD_DIM:] = ones


def _mproj(x, g, wc, wkr, gq, wuq, gkv, wukv, tabs, layer, *, tm=512):
    s, d = x.shape
    tab_spec = pl.BlockSpec((tm, LANES), lambda m: (m, 0))

    def wspec(w):
        return pl.BlockSpec((1,) + w.shape[1:], lambda m: (layer, 0, 0))

    def gspec(gv):
        return pl.BlockSpec((1, 1, gv.shape[2]), lambda m: (layer, 0, 0))

    return pl.pallas_call(
        _mproj_kernel,
        grid=(s // tm,),
        in_specs=[
            pl.BlockSpec((tm, d), lambda m: (m, 0)),
            gspec(g), wspec(wc), wspec(wkr), gspec(gq), wspec(wuq), gspec(gkv), wspec(wukv),
            tab_spec, tab_spec, tab_spec,
        ],
        out_specs=[
            pl.BlockSpec((N_HEADS, tm, MLA_QK_PAD), lambda m: (0, m, 0)),
            pl.BlockSpec((N_HEADS, tm, MLA_QK_PAD), lambda m: (0, m, 0)),
            pl.BlockSpec((N_HEADS, tm, 2 * HEAD_DIM), lambda m: (0, m, 0)),
        ],
        out_shape=[
            jax.ShapeDtypeStruct((N_HEADS, s, MLA_QK_PAD), BF16),
            jax.ShapeDtypeStruct((N_HEADS, s, MLA_QK_PAD), BF16),
            jax.ShapeDtypeStruct((N_HEADS, s, 2 * HEAD_DIM), BF16),
        ],
        compiler_params=_params("parallel"),
        name="mproj",
    )(x, g, wc, wkr, gq, wuq, gkv, wukv, *tabs)


def _flash_kernel(q_ref, k_ref, v_ref, o_ref, m_ref, acc_ref,
                  sa_ref, sb_ref, sc_ref, pa_ref, pb_ref, pc_ref, *, tq):
    half = tq // 2
    i = pl.program_id(1)
    last = pl.num_programs(1) - 1
    m_ref[...] = jnp.full_like(m_ref, NEG)
    acc_ref[...] = jnp.zeros_like(acc_ref)

    def key_rows(c, n=tq):
        return pl.ds(pl.multiple_of(c * tq, tq), n)

    def scores(c, s_ref, p_ref, tile=i):
        kc = k_ref[0, key_rows(c), :]
        for st in range(2):
            q = q_ref[0, pl.ds(pl.multiple_of(tile * tq + st * half, half), half), :]
            s = lax.dot_general(q, kc, (((1,), (1,)), ((), ())), preferred_element_type=F32)
            s_ref[st] = s
            pm = s[:, :LANES]
            for t in range(1, tq // LANES):
                pm = jnp.maximum(pm, s[:, t * LANES:(t + 1) * LANES])
            p_ref[st] = pm

    def update(st, s, row_max, vc):
        m_prev = m_ref[st]
        m_new = jnp.maximum(m_prev, row_max)
        alpha = jnp.exp2(m_prev - m_new)
        pr = jnp.exp2(s - m_new).astype(BF16)
        acc_ref[st] = alpha * acc_ref[st] + jnp.dot(pr, vc, preferred_element_type=F32)
        m_ref[st] = m_new

    def accumulate(c, s_ref, p_ref):
        vc = v_ref[0, key_rows(c), :]
        for st in range(2):
            update(st, s_ref[st], jnp.max(p_ref[st], axis=-1, keepdims=True), vc)

    def finish(s_ref):
        scores(0, sc_ref, pc_ref, tile=jnp.minimum(i + 1, last))
        s0 = s_ref[0, :, :half]
        rows = lax.broadcasted_iota(jnp.int32, s0.shape, 0)
        cols = lax.broadcasted_iota(jnp.int32, s0.shape, 1)
        s0 = jnp.where(cols <= rows, s0, NEG)
        update(0, s0, jnp.max(s0, axis=-1, keepdims=True), v_ref[0, key_rows(i, half), :])
        s1 = s_ref[1]
        rows = lax.broadcasted_iota(jnp.int32, s1.shape, 0)
        cols = lax.broadcasted_iota(jnp.int32, s1.shape, 1)
        s1 = jnp.where(cols <= rows + half, s1, NEG)
        update(1, s1, jnp.max(s1, axis=-1, keepdims=True), v_ref[0, key_rows(i), :])
        for st in range(2):
            acc = acc_ref[st]
            o_ref[st * half:(st + 1) * half, :] = (
                acc[:, :HEAD_DIM] / acc[:, HEAD_DIM:]).astype(BF16)

    @pl.when(i == 0)
    def _():
        scores(0, sa_ref, pa_ref)
        finish(sa_ref)

    @pl.when(i > 0)
    def _():
        scores(1, sa_ref, pa_ref)
        accumulate(0, sc_ref, pc_ref)

    def pair(jj, carry):
        c = 2 * jj + 1
        scores(c + 1, sb_ref, pb_ref)
        accumulate(c, sa_ref, pa_ref)
        scores(c + 2, sa_ref, pa_ref)
        accumulate(c + 1, sb_ref, pb_ref)
        return carry

    lax.fori_loop(0, lax.div(jnp.maximum(i - 1, 0), 2), pair, 0)

    @pl.when(lax.rem(i, 2) == 1)
    def _():
        finish(sa_ref)

    @pl.when(jnp.logical_and(lax.rem(i, 2) == 0, i > 0))
    def _():
        scores(i, sb_ref, pb_ref)
        accumulate(i - 1, sa_ref, pa_ref)
        finish(sb_ref)


def _flash(q, k, v, *, tq=1024):
    _, s, _ = q.shape
    score_buf = pltpu.VMEM((2, tq // 2, tq), F32)
    pmax_buf = pltpu.VMEM((2, tq // 2, LANES), F32)
    return pl.pallas_call(
        functools.partial(_flash_kernel, tq=tq),
        grid=(N_HEADS, s // tq),
        in_specs=[
            pl.BlockSpec((1, s, MLA_QK_PAD), lambda h, i: (h, 0, 0)),
            pl.BlockSpec((1, s, MLA_QK_PAD), lambda h, i: (h, 0, 0)),
            pl.BlockSpec((1, s, 2 * HEAD_DIM), lambda h, i: (h, 0, 0)),
        ],
        out_specs=pl.BlockSpec((tq, HEAD_DIM), lambda h, i: (i, h)),
        out_shape=jax.ShapeDtypeStruct((s, N_HEADS * HEAD_DIM), BF16),
        scratch_shapes=[
            pltpu.VMEM((2, tq // 2, 1), F32),
            pltpu.VMEM((2, tq // 2, 2 * HEAD_DIM), F32),
            score_buf, score_buf, score_buf, pmax_buf, pmax_buf, pmax_buf,
        ],
        compiler_params=_params("arbitrary", "arbitrary"),
        name="mla_flash",
    )(q, k, v)


def _dil_kernel(q_ref, kc_ref, kp_ref, vc_ref, vp_ref, o_ref, o_sc, l_sc):
    t = q_ref.shape[1]
    blk = DIL_SPAN
    n = pl.program_id(0)

    qi = lax.broadcasted_iota(jnp.int32, (blk, 2 * blk), 0)
    kj = lax.broadcasted_iota(jnp.int32, (blk, 2 * blk), 1)
    diff = kj - qi
    band = (diff >= 0) & (diff <= DIL_SPAN)
    band_first = band & (kj >= jnp.where(n == 0, blk, 0))

    ones = jnp.ones((2 * blk, HEAD_DIM), BF16)

    def rows(ref, start, d):
        idx = pl.ds(start, blk) if d == 1 else pl.ds(start, blk, stride=d)
        return ref[0, idx, :].astype(BF16)

    for idx, (window, d) in enumerate(DIL_PATTERNS):
        seg = blk * d
        for it in range(t // blk):
            r, b = it % d, it // d
            start = b * seg + r
            q = rows(q_ref, start, d)
            if b == 0:
                kp, vp = rows(kp_ref, t - seg + r, d), rows(vp_ref, t - seg + r, d)
            else:
                kp, vp = rows(kc_ref, start - seg, d), rows(vc_ref, start - seg, d)
            k = jnp.concatenate([kp, rows(kc_ref, start, d)], axis=0)
            v = jnp.concatenate([vp, rows(vc_ref, start, d)], axis=0)
            v = jnp.concatenate([v, ones], axis=1)
            s = lax.dot_general(q, k, (((1,), (1,)), ((), ())), preferred_element_type=F32)
            s = jnp.where(band_first if b == 0 else band, s, NEG)
            m = jnp.max(s, axis=-1, keepdims=True)
            pr = jnp.exp2(s - m)
            ol = jnp.dot(pr.astype(BF16), v, preferred_element_type=F32)
            l = ol[:, HEAD_DIM:]
            out_rows = pl.ds(start, blk) if d == 1 else pl.ds(start, blk, stride=d)
            o_sc[idx, out_rows, :] = ol[:, :HEAD_DIM] / l
            l_sc[idx, out_rows, :] = m + jnp.log2(l)

    lses = [l_sc[p] for p in range(len(DIL_PATTERNS))]
    mx = functools.reduce(jnp.maximum, lses)
    ws = [jnp.exp2(l - mx) for l in lses]
    num = sum(w * o_sc[p] for p, w in enumerate(ws))
    o_ref[...] = (num / sum(ws)).astype(BF16)


def _dilated(qkv):
    _, s, width = qkv.shape
    t = DIL_TILE
    npat = len(DIL_PATTERNS)

    def cur(which):
        return pl.BlockSpec((1, t, HEAD_DIM), lambda n, h: (which, n, h))

    def prv(which):
        return pl.BlockSpec((1, t, HEAD_DIM), lambda n, h: (which, jnp.maximum(n - 1, 0), h))

    return pl.pallas_call(
        _dil_kernel,
        grid=(s // t, N_HEADS),
        in_specs=[cur(0), cur(1), prv(1), cur(2), prv(2)],
        out_specs=pl.BlockSpec((t, HEAD_DIM), lambda n, h: (n, h)),
        out_shape=jax.ShapeDtypeStruct((s, width), BF16),
        scratch_shapes=[
            pltpu.VMEM((npat, t, HEAD_DIM), F32),
            pltpu.VMEM((npat, t, LANES), F32),
        ],
        compiler_params=_params("parallel", "parallel"),
        name="dilated",
    )(qkv, qkv, qkv, qkv, qkv)


def _oproj_kernel(x_ref, oa_ref, ob_ref, w_ref, g_ref, o_ref):
    half = oa_ref.shape[1]
    y = jnp.dot(oa_ref[...], w_ref[0, :half], preferred_element_type=F32)
    y = y + jnp.dot(ob_ref[...], w_ref[0, half:], preferred_element_type=F32)
    o_ref[...] = x_ref[...] + _rms(y, g_ref[0])


def _oproj(x, oa, ob, wo, g, layer, *, tm=512):
    s, d = x.shape
    width = oa.shape[1]
    return pl.pallas_call(
        _oproj_kernel,
        grid=(s // tm,),
        in_specs=[
            pl.BlockSpec((tm, d), lambda m: (m, 0)),
            pl.BlockSpec((tm, width), lambda m: (m, 0)),
            pl.BlockSpec((tm, width), lambda m: (m, 0)),
            pl.BlockSpec((1,) + wo.shape[1:], lambda m: (layer, 0, 0)),
            pl.BlockSpec((1, 1, d), lambda m: (layer, 0, 0)),
        ],
        out_specs=pl.BlockSpec((tm, d), lambda m: (m, 0)),
        out_shape=jax.ShapeDtypeStruct((s, d), F32),
        compiler_params=_params("parallel"),
        name="oproj",
    )(x, oa, ob, wo, g)


def _split_w_in_kernel(w_ref, wc_ref, wkr_ref, wd_ref):
    lat = 2 * MLA_RANK
    w = w_ref[0]
    wc_ref[0] = w[:, :lat].astype(BF16)
    kr = w[:, lat:lat + MLA_ROPE]
    wkr_ref[0] = jnp.concatenate([kr, jnp.zeros_like(kr)], axis=1).astype(BF16)
    wd_ref[0] = w[:, lat + MLA_ROPE:].astype(BF16)


def _split_w_in(w_in, *, rows=256):
    depth, d, n = w_in.shape
    lat = 2 * MLA_RANK
    widths = (lat, LANES, n - lat - MLA_ROPE)
    return pl.pallas_call(
        _split_w_in_kernel,
        grid=(depth, d // rows),
        in_specs=[pl.BlockSpec((1, rows, n), lambda l, r: (l, r, 0))],
        out_specs=[pl.BlockSpec((1, rows, w), lambda l, r: (l, r, 0)) for w in widths],
        out_shape=[jax.ShapeDtypeStruct((depth, d, w), BF16) for w in widths],
        compiler_params=_params("parallel", "parallel"),
        name="split_w_in",
    )(w_in)


def _rope_tables(pos, dim):
    half = dim // 2
    inv = ROPE_THETA ** (-jnp.arange(0, dim, 2, dtype=F32) / dim)
    ang = pos.astype(F32)[:, None] * inv
    cos, sin = jnp.cos(ang), jnp.sin(ang)
    n = pos.shape[0]
    c = jnp.concatenate([cos, cos, jnp.ones((n, LANES - dim), F32)], axis=1)
    a = jnp.concatenate([-sin, jnp.zeros((n, LANES - half), F32)], axis=1)
    b = jnp.concatenate([jnp.zeros((n, half), F32), sin, jnp.zeros((n, LANES - dim), F32)], axis=1)
    return c, a, b


def kernel(x, positions, ffn1_pre_g, ffn1_post_g, ffn1_w_gate, ffn1_w_up, ffn1_w_down,
           mix_pre_g, mix_post_g, w_in, mla_q_norm_g, mla_w_uq, mla_kv_norm_g, mla_w_ukv,
           w_o, ffn2_pre_g, ffn2_post_g, ffn2_w_gate, ffn2_w_up, ffn2_w_down):
    batch, _, _ = x.shape
    depth = w_in.shape[0]
    bf = lambda w: w.astype(BF16)

    wc, wkr, wd = _split_w_in(w_in)
    wuq = bf(mla_w_uq).reshape(depth, MLA_RANK, N_HEADS, MLA_QK)
    wuq = jnp.pad(wuq, ((0, 0), (0, 0), (0, 0), (0, MLA_QK_PAD - MLA_QK))
                  ).reshape(depth, MLA_RANK, N_HEADS * MLA_QK_PAD)
    wukv = bf(mla_w_ukv)
    wo = bf(w_o)
    ffn_f32 = ((ffn1_w_gate, ffn1_w_up, ffn1_w_down), (ffn2_w_gate, ffn2_w_up, ffn2_w_down))
    ffn_gains = ((ffn1_pre_g, ffn1_post_g), (ffn2_pre_g, ffn2_post_g))
    ffn_bf16 = {(0, 0): tuple(bf(w[:1]) for w in ffn_f32[0])}

    def ffn(xs, l, which):
        nxt = (l, 1) if which == 0 else (l + 1, 0)
        cast_next = None
        if nxt[0] < depth and nxt not in ffn_bf16:
            cast_next = ffn_f32[nxt[1]] + (nxt[0],)
        gpre, gpost = (g[l][None, None, :] for g in ffn_gains[which])
        xs, converted = _ffn(xs, gpre, gpost, ffn_bf16[(l, which)], 0, cast_next)
        if cast_next is not None:
            ffn_bf16[nxt] = converted
        return xs
    row = lambda g: g[:, None, :]
    mix_pre_g, mix_post_g, mla_q_norm_g, mla_kv_norm_g = map(
        row, (mix_pre_g, mix_post_g, mla_q_norm_g, mla_kv_norm_g))

    outs = []
    for bi in range(batch):
        xs = x[bi]
        tabs_a = _rope_tables(positions[bi], MLA_ROPE)
        cp, ap, bp = _rope_tables(positions[bi], PART_ROPE)
        qscale = math.log2(math.e) / math.sqrt(HEAD_DIM)
        tabs_p = (jnp.stack([cp * qscale, cp, jnp.ones_like(cp)]),
                  jnp.stack([ap * qscale, ap, jnp.zeros_like(ap)]),
                  jnp.stack([bp * qscale, bp, jnp.zeros_like(bp)]))
        for l in range(depth):
            xs = ffn(xs, l, 0)
            qkv = _dproj(xs, mix_pre_g, wd, tabs_p, l)
            qm, km, vm = _mproj(xs, mix_pre_g, wc, wkr, mla_q_norm_g, wuq,
                                mla_kv_norm_g, wukv, tabs_a, l)
            oa = _flash(qm, km, vm)
            ob = _dilated(qkv)
            xs = _oproj(xs, oa, ob, wo, mix_post_g, l)
            xs = ffn(xs, l, 1)
        outs.append(xs)
    return jnp.stack(outs, axis=0)
```

```python
import functools
import math

import jax
import jax.numpy as jnp
from jax import lax
from jax.experimental import pallas as pl
from jax.experimental.pallas import tpu as pltpu

F32 = jnp.float32
BF16 = jnp.bfloat16

HEAD_DIM = 128
N_HEADS = 8
MLA_RANK = 512
MLA_NOPE = 128
MLA_ROPE = 64
MLA_QK = MLA_NOPE + MLA_ROPE
MLA_QK_PAD = 256
PART_ROPE = 32
DIL_PATTERNS = ((128, 1), (512, 4), (2048, 16))
DIL_SPAN = 128
DIL_TILE = 2048
ROPE_THETA = 500000.0
RMS_EPS = 1e-6
NEG = -1e30
LANES = 128
VMEM_LIMIT = 58 * 1024 * 1024
FFN_NORM_ROWS = 256
FFN_DOT_ROWS = 512
DPROJ_DOT_ROWS = 128


def _rms(xf, g):
    ms = jnp.mean(xf * xf, axis=-1, keepdims=True)
    return xf * lax.rsqrt(ms + RMS_EPS) * g


def _rope(y, c, a, b, half):
    return y * c + pltpu.roll(y, LANES - half, axis=1) * a + pltpu.roll(y, half, axis=1) * b


def _params(*sem):
    return pltpu.CompilerParams(dimension_semantics=sem, vmem_limit_bytes=VMEM_LIMIT)


def _ffn_kernel(*refs, cast_next):
    x_ref, gpre_ref, gpost_ref, wg_ref, wu_ref, wd_ref = refs[:6]
    if cast_next:
        o_ref, nb_ref, r_ref = refs[9], refs[13], refs[14]
        for src_ref, dst_ref in zip(refs[6:9], refs[10:13]):
            dst_ref[...] = src_ref[...].astype(BF16)
    else:
        o_ref, nb_ref, r_ref = refs[6:]
    j = pl.program_id(1)

    tm = x_ref.shape[0]

    def row_chunks(size):
        return [slice(r, r + size) for r in range(0, tm, size)]

    @pl.when(j == 0)
    def _():
        for rows in row_chunks(FFN_NORM_ROWS):
            nb_ref[rows] = _rms(x_ref[rows], gpre_ref[0]).astype(BF16)
        o_ref[...] = jnp.zeros_like(o_ref)

    for rows in row_chunks(FFN_DOT_ROWS):
        nb = nb_ref[rows]
        g = jnp.dot(nb, wg_ref[0], preferred_element_type=F32)
        u = jnp.dot(nb, wu_ref[0], preferred_element_type=F32)
        h = (g * jax.nn.sigmoid(g) * u).astype(BF16)
        o_ref[rows] += jnp.dot(h, wd_ref[0], preferred_element_type=F32)

    @pl.when(j == pl.num_programs(1) - 1)
    def _():
        for rows in row_chunks(FFN_NORM_ROWS):
            y = o_ref[rows]
            r_ref[rows] = 0.5 * lax.rsqrt(jnp.mean(y * y, axis=-1, keepdims=True) + RMS_EPS)
        for rows in row_chunks(FFN_NORM_ROWS):
            o_ref[rows] = x_ref[rows] + o_ref[rows] * r_ref[rows] * gpost_ref[0]


def _ffn(x, gpre, gpost, weights, layer, cast_next=None, *, tm=1024, tn=512):
    s, d = x.shape
    wg, wu, wd = weights
    dff = wg.shape[-1]
    nm = s // tm
    dm = d // nm
    in_specs = [
        pl.BlockSpec((tm, d), lambda m, j: (m, 0)),
        pl.BlockSpec((1, 1, d), lambda m, j: (layer, 0, 0)),
        pl.BlockSpec((1, 1, d), lambda m, j: (layer, 0, 0)),
        pl.BlockSpec((1, d, tn), lambda m, j: (layer, 0, j)),
        pl.BlockSpec((1, d, tn), lambda m, j: (layer, 0, j)),
        pl.BlockSpec((1, tn, d), lambda m, j: (layer, j, 0)),
    ]
    out_specs = [pl.BlockSpec((tm, d), lambda m, j: (m, 0))]
    out_shape = [jax.ShapeDtypeStruct((s, d), F32)]
    args = [x, gpre, gpost, wg, wu, wd]
    if cast_next is not None:
        *nxt, nl = cast_next
        in_specs += [
            pl.BlockSpec((1, dm, tn), lambda m, j: (nl, m, j)),
            pl.BlockSpec((1, dm, tn), lambda m, j: (nl, m, j)),
            pl.BlockSpec((1, tn, dm), lambda m, j: (nl, j, m)),
        ]
        out_specs += [
            pl.BlockSpec((1, dm, tn), lambda m, j: (0, m, j)),
            pl.BlockSpec((1, dm, tn), lambda m, j: (0, m, j)),
            pl.BlockSpec((1, tn, dm), lambda m, j: (0, j, m)),
        ]
        out_shape += [jax.ShapeDtypeStruct((1,) + w.shape[1:], BF16) for w in nxt]
        args += nxt
    out = pl.pallas_call(
        functools.partial(_ffn_kernel, cast_next=cast_next is not None),
        grid=(nm, dff // tn),
        in_specs=in_specs,
        out_specs=out_specs,
        out_shape=out_shape,
        scratch_shapes=[pltpu.VMEM((tm, d), BF16), pltpu.VMEM((tm, 1), F32)],
        compiler_params=_params("parallel", "arbitrary"),
        name="ffn",
    )(*args)
    return out[0], tuple(out[1:])


def _dproj_kernel(x_ref, g_ref, w_ref, c_ref, a_ref, b_ref, o_ref, nb_ref):
    j = pl.program_id(1)

    @pl.when(j == 0)
    def _():
        for r in range(0, x_ref.shape[0], FFN_NORM_ROWS):
            rows = slice(r, r + FFN_NORM_ROWS)
            nb_ref[rows] = _rms(x_ref[rows], g_ref[0]).astype(BF16)

    for r in range(0, x_ref.shape[0], DPROJ_DOT_ROWS):
        rows = slice(r, r + DPROJ_DOT_ROWS)
        y = jnp.dot(nb_ref[rows], w_ref[0], preferred_element_type=F32)
        c, a, b = c_ref[0, rows], a_ref[0, rows], b_ref[0, rows]
        for h in range(N_HEADS):
            sl = slice(h * HEAD_DIM, (h + 1) * HEAD_DIM)
            o_ref[0, rows, sl] = _rope(y[:, sl], c, a, b, PART_ROPE // 2)


def _dproj(x, g, wd, tabs, layer, *, tm=1024):
    s, d = x.shape
    width = N_HEADS * HEAD_DIM
    tab_spec = pl.BlockSpec((1, tm, LANES), lambda m, j: (j, m, 0))
    return pl.pallas_call(
        _dproj_kernel,
        grid=(s // tm, 3),
        in_specs=[
            pl.BlockSpec((tm, d), lambda m, j: (m, 0)),
            pl.BlockSpec((1, 1, d), lambda m, j: (layer, 0, 0)),
            pl.BlockSpec((1, d, width), lambda m, j: (layer, 0, j)),
            tab_spec, tab_spec, tab_spec,
        ],
        out_specs=pl.BlockSpec((1, tm, width), lambda m, j: (j, m, 0)),
        out_shape=jax.ShapeDtypeStruct((3, s, width), F32),
        scratch_shapes=[pltpu.VMEM((tm, d), BF16)],
        compiler_params=_params("parallel", "arbitrary"),
        name="dproj",
    )(x, g, wd, *tabs)


def _mproj_kernel(x_ref, g_ref, wc_ref, wkr_ref, gq_ref, wuq_ref, gkv_ref, wukv_ref,
                  c_ref, a_ref, b_ref, q_ref, k_ref, v_ref):
    nb = _rms(x_ref[...], g_ref[0]).astype(BF16)
    lat = jnp.dot(nb, wc_ref[0], preferred_element_type=F32)
    cq = _rms(lat[:, :MLA_RANK], gq_ref[0]).astype(BF16)
    ckv = _rms(lat[:, MLA_RANK:], gkv_ref[0]).astype(BF16)
    c, a, b = c_ref[...], a_ref[...], b_ref[...]
    half = MLA_ROPE // 2
    kr = jnp.dot(nb, wkr_ref[0], preferred_element_type=F32)
    kr = _rope(kr, c, a, b, half).astype(BF16)
    q = jnp.dot(cq, wuq_ref[0], preferred_element_type=F32)
    kv = jnp.dot(ckv, wukv_ref[0], preferred_element_type=F32)
    scale = math.log2(math.e) / math.sqrt(MLA_QK)
    ones = jnp.ones((x_ref.shape[0], HEAD_DIM), BF16)
    for h in range(N_HEADS):
        lo = h * MLA_QK_PAD
        q_ref[h, :, :MLA_NOPE] = (q[:, lo:lo + MLA_NOPE] * scale).astype(BF16)
        qr = _rope(q[:, lo + MLA_NOPE:lo + MLA_QK_PAD], c, a, b, half)
        q_ref[h, :, MLA_NOPE:] = (qr * scale).astype(BF16)
        k_ref[h, :, :MLA_NOPE] = kv[:, lo:lo + MLA_NOPE].astype(BF16)
        k_ref[h, :, MLA_NOPE:] = kr
        v_ref[h, :, :HEAD_DIM] = kv[:, lo + MLA_NOPE:lo + MLA_QK_PAD].astype(BF16)
        v_ref[h, :, HEAD_DIM:] = ones


def _mproj(x, g, wc, wkr, gq, wuq, gkv, wukv, tabs, layer, *, tm=512):
    s, d = x.shape
    tab_spec = pl.BlockSpec((tm, LANES), lambda m: (m, 0))

    def wspec(w):
        return pl.BlockSpec((1,) + w.shape[1:], lambda m: (layer, 0, 0))

    def gspec(gv):
        return pl.BlockSpec((1, 1, gv.shape[2]), lambda m: (layer, 0, 0))

    return pl.pallas_call(
        _mproj_kernel,
        grid=(s // tm,),
        in_specs=[
            pl.BlockSpec((tm, d), lambda m: (m, 0)),
            gspec(g), wspec(wc), wspec(wkr), gspec(gq), wspec(wuq), gspec(gkv), wspec(wukv),
            tab_spec, tab_spec, tab_spec,
        ],
        out_specs=[
            pl.BlockSpec((N_HEADS, tm, MLA_QK_PAD), lambda m: (0, m, 0)),
            pl.BlockSpec((N_HEADS, tm, MLA_QK_PAD), lambda m: (0, m, 0)),
            pl.BlockSpec((N_HEADS, tm, 2 * HEAD_DIM), lambda m: (0, m, 0)),
        ],
        out_shape=[
            jax.ShapeDtypeStruct((N_HEADS, s, MLA_QK_PAD), BF16),
            jax.ShapeDtypeStruct((N_HEADS, s, MLA_QK_PAD), BF16),
            jax.ShapeDtypeStruct((N_HEADS, s, 2 * HEAD_DIM), BF16),
        ],
        compiler_params=_params("parallel"),
        name="mproj",
    )(x, g, wc, wkr, gq, wuq, gkv, wukv, *tabs)


def _flash_kernel(q_ref, k_ref, v_ref, o_ref, m_ref, acc_ref,
                  sa_ref, sb_ref, sc_ref, pa_ref, pb_ref, pc_ref, *, tq):
    half = tq // 2
    i = pl.program_id(1)
    last = pl.num_programs(1) - 1
    m_ref[...] = jnp.full_like(m_ref, NEG)
    acc_ref[...] = jnp.zeros_like(acc_ref)

    def key_rows(c, n=tq):
        return pl.ds(pl.multiple_of(c * tq, tq), n)

    def scores(c, s_ref, p_ref, tile=i):
        kc = k_ref[0, key_rows(c), :]
        for st in range(2):
            q = q_ref[0, pl.ds(pl.multiple_of(tile * tq + st * half, half), half), :]
            s = lax.dot_general(q, kc, (((1,), (1,)), ((), ())), preferred_element_type=F32)
            s_ref[st] = s
            pm = s[:, :LANES]
            for t in range(1, tq // LANES):
                pm = jnp.maximum(pm, s[:, t * LANES:(t + 1) * LANES])
            p_ref[st] = pm

    def update(st, s, row_max, vc):
        m_prev = m_ref[st]
        m_new = jnp.maximum(m_prev, row_max)
        alpha = jnp.exp2(m_prev - m_new)
        pr = jnp.exp2(s - m_new).astype(BF16)
        acc_ref[st] = alpha * acc_ref[st] + jnp.dot(pr, vc, preferred_element_type=F32)
        m_ref[st] = m_new

    def accumulate(c, s_ref, p_ref):
        vc = v_ref[0, key_rows(c), :]
        for st in range(2):
            update(st, s_ref[st], jnp.max(p_ref[st], axis=-1, keepdims=True), vc)

    def finish(s_ref):
        scores(0, sc_ref, pc_ref, tile=jnp.minimum(i + 1, last))
        s0 = s_ref[0, :, :half]
        rows = lax.broadcasted_iota(jnp.int32, s0.shape, 0)
        cols = lax.broadcasted_iota(jnp.int32, s0.shape, 1)
        s0 = jnp.where(cols <= rows, s0, NEG)
        update(0, s0, jnp.max(s0, axis=-1, keepdims=True), v_ref[0, key_rows(i, half), :])
        s1 = s_ref[1]
        rows = lax.broadcasted_iota(jnp.int32, s1.shape, 0)
        cols = lax.broadcasted_iota(jnp.int32, s1.shape, 1)
        s1 = jnp.where(cols <= rows + half, s1, NEG)
        update(1, s1, jnp.max(s1, axis=-1, keepdims=True), v_ref[0, key_rows(i), :])
        for st in range(2):
            acc = acc_ref[st]
            o_ref[st * half:(st + 1) * half, :] = (
                acc[:, :HEAD_DIM] / acc[:, HEAD_DIM:]).astype(BF16)

    @pl.when(i == 0)
    def _():
        scores(0, sa_ref, pa_ref)
        finish(sa_ref)

    @pl.when(i > 0)
    def _():
        scores(1, sa_ref, pa_ref)
        accumulate(0, sc_ref, pc_ref)

    def pair(jj, carry):
        c = 2 * jj + 1
        scores(c + 1, sb_ref, pb_ref)
        accumulate(c, sa_ref, pa_ref)
        scores(c + 2, sa_ref, pa_ref)
        accumulate(c + 1, sb_ref, pb_ref)
        return carry

    lax.fori_loop(0, lax.div(jnp.maximum(i - 1, 0), 2), pair, 0)

    @pl.when(lax.rem(i, 2) == 1)
    def _():
        finish(sa_ref)

    @pl.when(jnp.logical_and(lax.rem(i, 2) == 0, i > 0))
    def _():
        scores(i, sb_ref, pb_ref)
        accumulate(i - 1, sa_ref, pa_ref)
        finish(sb_ref)


def _flash(q, k, v, *, tq=1024):
    _, s, _ = q.shape
    score_buf = pltpu.VMEM((2, tq // 2, tq), F32)
    pmax_buf = pltpu.VMEM((2, tq // 2, LANES), F32)
    return pl.pallas_call(
        functools.partial(_flash_kernel, tq=tq),
        grid=(N_HEADS, s // tq),
        in_specs=[
            pl.BlockSpec((1, s, MLA_QK_PAD), lambda h, i: (h, 0, 0)),
            pl.BlockSpec((1, s, MLA_QK_PAD), lambda h, i: (h, 0, 0)),
            pl.BlockSpec((1, s, 2 * HEAD_DIM), lambda h, i: (h, 0, 0)),
        ],
        out_specs=pl.BlockSpec((tq, HEAD_DIM), lambda h, i: (i, h)),
        out_shape=jax.ShapeDtypeStruct((s, N_HEADS * HEAD_DIM), BF16),
        scratch_shapes=[
            pltpu.VMEM((2, tq // 2, 1), F32),
            pltpu.VMEM((2, tq // 2, 2 * HEAD_DIM), F32),
            score_buf, score_buf, score_buf, pmax_buf, pmax_buf, pmax_buf,
        ],
        compiler_params=_params("arbitrary", "arbitrary"),
        name="mla_flash",
    )(q, k, v)


def _dil_kernel(q_ref, kc_ref, kp_ref, vc_ref, vp_ref, o_ref, o_sc, l_sc):
    t = q_ref.shape[1]
    blk = DIL_SPAN
    n = pl.program_id(0)

    qi = lax.broadcasted_iota(jnp.int32, (blk, 2 * blk), 0)
    kj = lax.broadcasted_iota(jnp.int32, (blk, 2 * blk), 1)
    diff = kj - qi
    band = (diff >= 0) & (diff <= DIL_SPAN)
    band_first = band & (kj >= jnp.where(n == 0, blk, 0))

    ones = jnp.ones((2 * blk, HEAD_DIM), BF16)

    def rows(ref, start, d):
        idx = pl.ds(start, blk) if d == 1 else pl.ds(start, blk, stride=d)
        return ref[0, idx, :].astype(BF16)

    for idx, (window, d) in enumerate(DIL_PATTERNS):
        seg = blk * d
        for it in range(t // blk):
            r, b = it % d, it // d
            start = b * seg + r
            q = rows(q_ref, start, d)
            if b == 0:
                kp, vp = rows(kp_ref, t - seg + r, d), rows(vp_ref, t - seg + r, d)
            else:
                kp, vp = rows(kc_ref, start - seg, d), rows(vc_ref, start - seg, d)
            k = jnp.concatenate([kp, rows(kc_ref, start, d)], axis=0)
            v = jnp.concatenate([vp, rows(vc_ref, start, d)], axis=0)
            v = jnp.concatenate([v, ones], axis=1)
            s = lax.dot_general(q, k, (((1,), (1,)), ((), ())), preferred_element_type=F32)
            s = jnp.where(band_first if b == 0 else band, s, NEG)
            m = jnp.max(s, axis=-1, keepdims=True)
            pr = jnp.exp2(s - m)
            ol = jnp.dot(pr.astype(BF16), v, preferred_element_type=F32)
            l = ol[:, HEAD_DIM:]
            out_rows = pl.ds(start, blk) if d == 1 else pl.ds(start, blk, stride=d)
            o_sc[idx, out_rows, :] = ol[:, :HEAD_DIM] / l
            l_sc[idx, out_rows, :] = m + jnp.log2(l)

    lses = [l_sc[p] for p in range(len(DIL_PATTERNS))]
    mx = functools.reduce(jnp.maximum, lses)
    ws = [jnp.exp2(l - mx) for l in lses]
    num = sum(w * o_sc[p] for p, w in enumerate(ws))
    o_ref[...] = (num / sum(ws)).astype(BF16)


def _dilated(qkv):
    _, s, width = qkv.shape
    t = DIL_TILE
    npat = len(DIL_PATTERNS)

    def cur(which):
        return pl.BlockSpec((1, t, HEAD_DIM), lambda n, h: (which, n, h))

    def prv(which):
        return pl.BlockSpec((1, t, HEAD_DIM), lambda n, h: (which, jnp.maximum(n - 1, 0), h))

    return pl.pallas_call(
        _dil_kernel,
        grid=(s // t, N_HEADS),
        in_specs=[cur(0), cur(1), prv(1), cur(2), prv(2)],
        out_specs=pl.BlockSpec((t, HEAD_DIM), lambda n, h: (n, h)),
        out_shape=jax.ShapeDtypeStruct((s, width), BF16),
        scratch_shapes=[
            pltpu.VMEM((npat, t, HEAD_DIM), F32),
            pltpu.VMEM((npat, t, LANES), F32),
        ],
        compiler_params=_params("parallel", "parallel"),
        name="dilated",
    )(qkv, qkv, qkv, qkv, qkv)


def _oproj_kernel(x_ref, oa_ref, ob_ref, w_ref, g_ref, o_ref):
    half = oa_ref.shape[1]
    y = jnp.dot(oa_ref[...], w_ref[0, :half], preferred_element_type=F32)
    y = y + jnp.dot(ob_ref[...], w_ref[0, half:], preferred_element_type=F32)
    o_ref[...] = x_ref[...] + _rms(y, g_ref[0])


def _oproj(x, oa, ob, wo, g, layer, *, tm=512):
    s, d = x.shape
    width = oa.shape[1]
    return pl.pallas_call(
        _oproj_kernel,
        grid=(s // tm,),
        in_specs=[
            pl.BlockSpec((tm, d), lambda m: (m, 0)),
            pl.BlockSpec((tm, width), lambda m: (m, 0)),
            pl.BlockSpec((tm, width), lambda m: (m, 0)),
            pl.BlockSpec((1,) + wo.shape[1:], lambda m: (layer, 0, 0)),
            pl.BlockSpec((1, 1, d), lambda m: (layer, 0, 0)),
        ],
        out_specs=pl.BlockSpec((tm, d), lambda m: (m, 0)),
        out_shape=jax.ShapeDtypeStruct((s, d), F32),
        compiler_params=_params("parallel"),
        name="oproj",
    )(x, oa, ob, wo, g)


def _cast_kernel(w_ref, o_ref):
    o_ref[...] = w_ref[...].astype(BF16)


def _cast_layer(w, layer, *, steps=8):
    _, r, c = w.shape
    rows = r // steps
    return pl.pallas_call(
        _cast_kernel,
        grid=(steps,),
        in_specs=[pl.BlockSpec((1, rows, c), lambda i: (layer, i, 0))],
        out_specs=pl.BlockSpec((1, rows, c), lambda i: (0, i, 0)),
        out_shape=jax.ShapeDtypeStruct((1, r, c), BF16),
        compiler_params=_params("parallel"),
        name="cast_layer",
    )(w)


def _rope_tables(pos, dim):
    half = dim // 2
    inv = ROPE_THETA ** (-jnp.arange(0, dim, 2, dtype=F32) / dim)
    ang = pos.astype(F32)[:, None] * inv
    cos, sin = jnp.cos(ang), jnp.sin(ang)
    n = pos.shape[0]
    c = jnp.concatenate([cos, cos, jnp.ones((n, LANES - dim), F32)], axis=1)
    a = jnp.concatenate([-sin, jnp.zeros((n, LANES - half), F32)], axis=1)
    b = jnp.concatenate([jnp.zeros((n, half), F32), sin, jnp.zeros((n, LANES - dim), F32)], axis=1)
    return c, a, b


def kernel(x, positions, ffn1_pre_g, ffn1_post_g, ffn1_w_gate, ffn1_w_up, ffn1_w_down,
           mix_pre_g, mix_post_g, w_in, mla_q_norm_g, mla_w_uq, mla_kv_norm_g, mla_w_ukv,
           w_o, ffn2_pre_g, ffn2_post_g, ffn2_w_gate, ffn2_w_up, ffn2_w_down):
    batch, _, _ = x.shape
    depth = w_in.shape[0]
    bf = lambda w: w.astype(BF16)

    lat_cols = 2 * MLA_RANK
    w_in = bf(w_in)
    wc = w_in[:, :, :lat_cols]
    wkr = jnp.pad(w_in[:, :, lat_cols:lat_cols + MLA_ROPE], ((0, 0), (0, 0), (0, LANES - MLA_ROPE)))
    wd = w_in[:, :, lat_cols + MLA_ROPE:]
    wuq = bf(mla_w_uq).reshape(depth, MLA_RANK, N_HEADS, MLA_QK)
    wuq = jnp.pad(wuq, ((0, 0), (0, 0), (0, 0), (0, MLA_QK_PAD - MLA_QK))
                  ).reshape(depth, MLA_RANK, N_HEADS * MLA_QK_PAD)
    wukv = bf(mla_w_ukv)
    wo = bf(w_o)
    ffn_f32 = ((ffn1_w_gate, ffn1_w_up, ffn1_w_down), (ffn2_w_gate, ffn2_w_up, ffn2_w_down))
    ffn_gains = ((ffn1_pre_g, ffn1_post_g), (ffn2_pre_g, ffn2_post_g))
    ffn_bf16 = {(0, 0): tuple(_cast_layer(w, 0) for w in ffn_f32[0])}

    def ffn(xs, l, which):
        nxt = (l, 1) if which == 0 else (l + 1, 0)
        cast_next = None
        if nxt[0] < depth and nxt not in ffn_bf16:
            cast_next = ffn_f32[nxt[1]] + (nxt[0],)
        gpre, gpost = (g[l][None, None, :] for g in ffn_gains[which])
        xs, converted = _ffn(xs, gpre, gpost, ffn_bf16[(l, which)], 0, cast_next)
        if cast_next is not None:
            ffn_bf16[nxt] = converted
        return xs
    row = lambda g: g[:, None, :]
    mix_pre_g, mix_post_g, mla_q_norm_g, mla_kv_norm_g = map(
        row, (mix_pre_g, mix_post_g, mla_q_norm_g, mla_kv_norm_g))

    outs = []
    for bi in range(batch):
        xs = x[bi]
        tabs_a = _rope_tables(positions[bi], MLA_ROPE)
        cp, ap, bp = _rope_tables(positions[bi], PART_ROPE)
        qscale = math.log2(math.e) / math.sqrt(HEAD_DIM)
        tabs_p = (jnp.stack([cp * qscale, cp, jnp.ones_like(cp)]),
                  jnp.stack([ap * qscale, ap, jnp.zeros_like(ap)]),
                  jnp.stack([bp * qscale, bp, jnp.zeros_like(bp)]))
        for l in range(depth):
            xs = ffn(xs, l, 0)
            qkv = _dproj(xs, mix_pre_g, wd, tabs_p, l)
            qm, km, vm = _mproj(xs, mix_pre_g, wc, wkr, mla_q_norm_g, wuq,
                                mla_kv_norm_g, wukv, tabs_a, l)
            oa = _flash(qm, km, vm)
            ob = _dilated(qkv)
            xs = _oproj(xs, oa, ob, wo, mix_post_g, l)
            xs = ffn(xs, l, 1)
        outs.append(xs)
    return jnp.stack(outs, axis=0)
```

```python
import functools
import math

import jax
import jax.numpy as jnp
from jax import lax
from jax.experimental import pallas as pl
from jax.experimental.pallas import tpu as pltpu

F32 = jnp.float32
BF16 = jnp.bfloat16

HEAD_DIM = 128
N_HEADS = 8
MLA_RANK = 512
MLA_NOPE = 128
MLA_ROPE = 64
MLA_QK = MLA_NOPE + MLA_ROPE
MLA_QK_PAD = 256
PART_ROPE = 32
DIL_PATTERNS = ((128, 1), (512, 4), (2048, 16))
DIL_SPAN = 128
DIL_TILE = 2048
ROPE_THETA = 500000.0
RMS_EPS = 1e-6
NEG = -1e30
LANES = 128
VMEM_LIMIT = 58 * 1024 * 1024
FFN_NORM_ROWS = 256
FFN_DOT_ROWS = 512
DPROJ_DOT_ROWS = 128


def _rms(xf, g):
    ms = jnp.mean(xf * xf, axis=-1, keepdims=True)
    return xf * lax.rsqrt(ms + RMS_EPS) * g


def _rope(y, c, a, b, half):
    return y * c + pltpu.roll(y, LANES - half, axis=1) * a + pltpu.roll(y, half, axis=1) * b


def _params(*sem):
    return pltpu.CompilerParams(dimension_semantics=sem, vmem_limit_bytes=VMEM_LIMIT)


def _ffn_kernel(*refs, cast_next):
    x_ref, gpre_ref, gpost_ref, wg_ref, wu_ref, wd_ref = refs[:6]
    if cast_next:
        o_ref, nb_ref, r_ref = refs[9], refs[13], refs[14]
        for src_ref, dst_ref in zip(refs[6:9], refs[10:13]):
            dst_ref[...] = src_ref[...].astype(BF16)
    else:
        o_ref, nb_ref, r_ref = refs[6:]
    j = pl.program_id(1)

    tm = x_ref.shape[0]

    def row_chunks(size):
        return [slice(r, r + size) for r in range(0, tm, size)]

    last = pl.num_programs(1) - 1

    def chunk_sum(first):
        for rows in row_chunks(FFN_DOT_ROWS):
            nb = nb_ref[rows]
            g = jnp.dot(nb, wg_ref[0], preferred_element_type=F32)
            u = jnp.dot(nb, wu_ref[0], preferred_element_type=F32)
            h = (g * jax.nn.sigmoid(g) * u).astype(BF16)
            part = jnp.dot(h, wd_ref[0], preferred_element_type=F32)
            if first:
                o_ref[rows] = part
            else:
                o_ref[rows] += part

    @pl.when(j == 0)
    def _():
        for rows in row_chunks(FFN_NORM_ROWS):
            nb_ref[rows] = _rms(x_ref[rows], gpre_ref[0]).astype(BF16)
        chunk_sum(True)

    @pl.when(jnp.logical_and(j > 0, j < last))
    def _():
        chunk_sum(False)

    @pl.when(j == last)
    def _():
        chunk_sum(False)
        for rows in row_chunks(FFN_NORM_ROWS):
            y = o_ref[rows]
            r_ref[rows] = 0.5 * lax.rsqrt(jnp.mean(y * y, axis=-1, keepdims=True) + RMS_EPS)
        for rows in row_chunks(FFN_NORM_ROWS):
            o_ref[rows] = x_ref[rows] + o_ref[rows] * r_ref[rows] * gpost_ref[0]


def _ffn(x, gpre, gpost, weights, layer, cast_next=None, *, tm=1024, tn=512):
    s, d = x.shape
    wg, wu, wd = weights
    dff = wg.shape[-1]
    nm = s // tm
    dm = d // nm
    assert dff // tn >= 2, "first and last D_FF steps must be distinct grid steps"
    in_specs = [
        pl.BlockSpec((tm, d), lambda m, j: (m, 0)),
        pl.BlockSpec((1, 1, d), lambda m, j: (layer, 0, 0)),
        pl.BlockSpec((1, 1, d), lambda m, j: (layer, 0, 0)),
        pl.BlockSpec((1, d, tn), lambda m, j: (layer, 0, j)),
        pl.BlockSpec((1, d, tn), lambda m, j: (layer, 0, j)),
        pl.BlockSpec((1, tn, d), lambda m, j: (layer, j, 0)),
    ]
    out_specs = [pl.BlockSpec((tm, d), lambda m, j: (m, 0))]
    out_shape = [jax.ShapeDtypeStruct((s, d), F32)]
    args = [x, gpre, gpost, wg, wu, wd]
    if cast_next is not None:
        *nxt, nl = cast_next
        in_specs += [
            pl.BlockSpec((1, dm, tn), lambda m, j: (nl, m, j)),
            pl.BlockSpec((1, dm, tn), lambda m, j: (nl, m, j)),
            pl.BlockSpec((1, tn, dm), lambda m, j: (nl, j, m)),
        ]
        out_specs += [
            pl.BlockSpec((1, dm, tn), lambda m, j: (0, m, j)),
            pl.BlockSpec((1, dm, tn), lambda m, j: (0, m, j)),
            pl.BlockSpec((1, tn, dm), lambda m, j: (0, j, m)),
        ]
        out_shape += [jax.ShapeDtypeStruct((1,) + w.shape[1:], BF16) for w in nxt]
        args += nxt
    out = pl.pallas_call(
        functools.partial(_ffn_kernel, cast_next=cast_next is not None),
        grid=(nm, dff // tn),
        in_specs=in_specs,
        out_specs=out_specs,
        out_shape=out_shape,
        scratch_shapes=[pltpu.VMEM((tm, d), BF16), pltpu.VMEM((tm, 1), F32)],
        compiler_params=_params("parallel", "arbitrary"),
        name="ffn",
    )(*args)
    return out[0], tuple(out[1:])


def _dproj_kernel(x_ref, g_ref, w_ref, c_ref, a_ref, b_ref, o_ref, nb_ref):
    j = pl.program_id(1)

    def project():
        for r in range(0, x_ref.shape[0], DPROJ_DOT_ROWS):
            rows = slice(r, r + DPROJ_DOT_ROWS)
            y = jnp.dot(nb_ref[rows], w_ref[0], preferred_element_type=F32)
            c, a, b = c_ref[0, rows], a_ref[0, rows], b_ref[0, rows]
            for h in range(N_HEADS):
                sl = slice(h * HEAD_DIM, (h + 1) * HEAD_DIM)
                o_ref[0, rows, sl] = _rope(y[:, sl], c, a, b, PART_ROPE // 2)

    @pl.when(j == 0)
    def _():
        for r in range(0, x_ref.shape[0], FFN_NORM_ROWS):
            rows = slice(r, r + FFN_NORM_ROWS)
            nb_ref[rows] = _rms(x_ref[rows], g_ref[0]).astype(BF16)
        project()

    @pl.when(j > 0)
    def _():
        project()


def _dproj(x, g, wd, tabs, layer, *, tm=1024):
    s, d = x.shape
    width = N_HEADS * HEAD_DIM
    tab_spec = pl.BlockSpec((1, tm, LANES), lambda m, j: (j, m, 0))
    return pl.pallas_call(
        _dproj_kernel,
        grid=(s // tm, 3),
        in_specs=[
            pl.BlockSpec((tm, d), lambda m, j: (m, 0)),
            pl.BlockSpec((1, 1, d), lambda m, j: (layer, 0, 0)),
            pl.BlockSpec((1, d, width), lambda m, j: (layer, 0, j)),
            tab_spec, tab_spec, tab_spec,
        ],
        out_specs=pl.BlockSpec((1, tm, width), lambda m, j: (j, m, 0)),
        out_shape=jax.ShapeDtypeStruct((3, s, width), F32),
        scratch_shapes=[pltpu.VMEM((tm, d), BF16)],
        compiler_params=_params("parallel", "arbitrary"),
        name="dproj",
    )(x, g, wd, *tabs)


def _mproj_kernel(x_ref, g_ref, wc_ref, wkr_ref, gq_ref, wuq_ref, gkv_ref, wukv_ref,
                  c_ref, a_ref, b_ref, q_ref, k_ref, v_ref):
    nb = _rms(x_ref[...], g_ref[0]).astype(BF16)
    lat = jnp.dot(nb, wc_ref[0], preferred_element_type=F32)
    cq = _rms(lat[:, :MLA_RANK], gq_ref[0]).astype(BF16)
    ckv = _rms(lat[:, MLA_RANK:], gkv_ref[0]).astype(BF16)
    c, a, b = c_ref[...], a_ref[...], b_ref[...]
    half = MLA_ROPE // 2
    kr = jnp.dot(nb, wkr_ref[0], preferred_element_type=F32)
    kr = _rope(kr, c, a, b, half).astype(BF16)
    q = jnp.dot(cq, wuq_ref[0], preferred_element_type=F32)
    kv = jnp.dot(ckv, wukv_ref[0], preferred_element_type=F32)
    scale = math.log2(math.e) / math.sqrt(MLA_QK)
    ones = jnp.ones((x_ref.shape[0], HEAD_DIM), BF16)
    for h in range(N_HEADS):
        lo = h * MLA_QK_PAD
        q_ref[h, :, :MLA_NOPE] = (q[:, lo:lo + MLA_NOPE] * scale).astype(BF16)
        qr = _rope(q[:, lo + MLA_NOPE:lo + MLA_QK_PAD], c, a, b, half)
        q_ref[h, :, MLA_NOPE:] = (qr * scale).astype(BF16)
        k_ref[h, :, :MLA_NOPE] = kv[:, lo:lo + MLA_NOPE].astype(BF16)
        k_ref[h, :, MLA_NOPE:] = kr
        v_ref[h, :, :HEAD_DIM] = kv[:, lo + MLA_NOPE:lo + MLA_QK_PAD].astype(BF16)
        v_ref[h, :, HEAD_DIM:] = ones


def _mproj(x, g, wc, wkr, gq, wuq, gkv, wukv, tabs, layer, *, tm=512):
    s, d = x.shape
    tab_spec = pl.BlockSpec((tm, LANES), lambda m: (m, 0))

    def wspec(w):
        return pl.BlockSpec((1,) + w.shape[1:], lambda m: (layer, 0, 0))

    def gspec(gv):
        return pl.BlockSpec((1, 1, gv.shape[2]), lambda m: (layer, 0, 0))

    return pl.pallas_call(
        _mproj_kernel,
        grid=(s // tm,),
        in_specs=[
            pl.BlockSpec((tm, d), lambda m: (m, 0)),
            gspec(g), wspec(wc), wspec(wkr), gspec(gq), wspec(wuq), gspec(gkv), wspec(wukv),
            tab_spec, tab_spec, tab_spec,
        ],
        out_specs=[
            pl.BlockSpec((N_HEADS, tm, MLA_QK_PAD), lambda m: (0, m, 0)),
            pl.BlockSpec((N_HEADS, tm, MLA_QK_PAD), lambda m: (0, m, 0)),
            pl.BlockSpec((N_HEADS, tm, 2 * HEAD_DIM), lambda m: (0, m, 0)),
        ],
        out_shape=[
            jax.ShapeDtypeStruct((N_HEADS, s, MLA_QK_PAD), BF16),
            jax.ShapeDtypeStruct((N_HEADS, s, MLA_QK_PAD), BF16),
            jax.ShapeDtypeStruct((N_HEADS, s, 2 * HEAD_DIM), BF16),
        ],
        compiler_params=_params("parallel"),
        name="mproj",
    )(x, g, wc, wkr, gq, wuq, gkv, wukv, *tabs)


def _flash_kernel(q_ref, k_ref, v_ref, o_ref, m_ref, acc_ref,
                  sa_ref, sb_ref, sc_ref, pa_ref, pb_ref, pc_ref, *, tq):
    half = tq // 2
    i = pl.program_id(1)
    last = pl.num_programs(1) - 1
    m_ref[...] = jnp.full_like(m_ref, NEG)
    acc_ref[...] = jnp.zeros_like(acc_ref)

    def key_rows(c, n=tq):
        return pl.ds(pl.multiple_of(c * tq, tq), n)

    def scores(c, s_ref, p_ref, tile=i):
        kc = k_ref[0, key_rows(c), :]
        for st in range(2):
            q = q_ref[0, pl.ds(pl.multiple_of(tile * tq + st * half, half), half), :]
            s = lax.dot_general(q, kc, (((1,), (1,)), ((), ())), preferred_element_type=F32)
            s_ref[st] = s
            pm = s[:, :LANES]
            for t in range(1, tq // LANES):
                pm = jnp.maximum(pm, s[:, t * LANES:(t + 1) * LANES])
            p_ref[st] = pm

    def update(st, s, row_max, vc):
        m_prev = m_ref[st]
        m_new = jnp.maximum(m_prev, row_max)
        alpha = jnp.exp2(m_prev - m_new)
        pr = jnp.exp2(s - m_new).astype(BF16)
        acc_ref[st] = alpha * acc_ref[st] + jnp.dot(pr, vc, preferred_element_type=F32)
        m_ref[st] = m_new

    def accumulate(c, s_ref, p_ref):
        vc = v_ref[0, key_rows(c), :]
        for st in range(2):
            update(st, s_ref[st], jnp.max(p_ref[st], axis=-1, keepdims=True), vc)

    def finish(s_ref):
        scores(0, sc_ref, pc_ref, tile=jnp.minimum(i + 1, last))
        s0 = s_ref[0, :, :half]
        rows = lax.broadcasted_iota(jnp.int32, s0.shape, 0)
        cols = lax.broadcasted_iota(jnp.int32, s0.shape, 1)
        s0 = jnp.where(cols <= rows, s0, NEG)
        update(0, s0, jnp.max(s0, axis=-1, keepdims=True), v_ref[0, key_rows(i, half), :])
        s1 = s_ref[1]
        rows = lax.broadcasted_iota(jnp.int32, s1.shape, 0)
        cols = lax.broadcasted_iota(jnp.int32, s1.shape, 1)
        s1 = jnp.where(cols <= rows + half, s1, NEG)
        update(1, s1, jnp.max(s1, axis=-1, keepdims=True), v_ref[0, key_rows(i), :])
        for st in range(2):
            acc = acc_ref[st]
            o_ref[st * half:(st + 1) * half, :] = (
                acc[:, :HEAD_DIM] / acc[:, HEAD_DIM:]).astype(BF16)

    @pl.when(i == 0)
    def _():
        scores(0, sa_ref, pa_ref)
        finish(sa_ref)

    @pl.when(i > 0)
    def _():
        scores(1, sa_ref, pa_ref)
        accumulate(0, sc_ref, pc_ref)

    def pair(jj, carry):
        c = 2 * jj + 1
        scores(c + 1, sb_ref, pb_ref)
        accumulate(c, sa_ref, pa_ref)
        scores(c + 2, sa_ref, pa_ref)
        accumulate(c + 1, sb_ref, pb_ref)
        return carry

    lax.fori_loop(0, lax.div(jnp.maximum(i - 1, 0), 2), pair, 0)

    @pl.when(lax.rem(i, 2) == 1)
    def _():
        finish(sa_ref)

    @pl.when(jnp.logical_and(lax.rem(i, 2) == 0, i > 0))
    def _():
        scores(i, sb_ref, pb_ref)
        accumulate(i - 1, sa_ref, pa_ref)
        finish(sb_ref)


def _flash(q, k, v, *, tq=1024):
    _, s, _ = q.shape
    score_buf = pltpu.VMEM((2, tq // 2, tq), F32)
    pmax_buf = pltpu.VMEM((2, tq // 2, LANES), F32)
    return pl.pallas_call(
        functools.partial(_flash_kernel, tq=tq),
        grid=(N_HEADS, s // tq),
        in_specs=[
            pl.BlockSpec((1, s, MLA_QK_PAD), lambda h, i: (h, 0, 0)),
            pl.BlockSpec((1, s, MLA_QK_PAD), lambda h, i: (h, 0, 0)),
            pl.BlockSpec((1, s, 2 * HEAD_DIM), lambda h, i: (h, 0, 0)),
        ],
        out_specs=pl.BlockSpec((tq, HEAD_DIM), lambda h, i: (i, h)),
        out_shape=jax.ShapeDtypeStruct((s, N_HEADS * HEAD_DIM), BF16),
        scratch_shapes=[
            pltpu.VMEM((2, tq // 2, 1), F32),
            pltpu.VMEM((2, tq // 2, 2 * HEAD_DIM), F32),
            score_buf, score_buf, score_buf, pmax_buf, pmax_buf, pmax_buf,
        ],
        compiler_params=_params("arbitrary", "arbitrary"),
        name="mla_flash",
    )(q, k, v)


def _dil_kernel(q_ref, kc_ref, kp_ref, vc_ref, vp_ref, o_ref, o_sc, l_sc):
    t = q_ref.shape[1]
    blk = DIL_SPAN
    n = pl.program_id(0)

    qi = lax.broadcasted_iota(jnp.int32, (blk, 2 * blk), 0)
    kj = lax.broadcasted_iota(jnp.int32, (blk, 2 * blk), 1)
    diff = kj - qi
    band = (diff >= 0) & (diff <= DIL_SPAN)
    band_first = band & (kj >= jnp.where(n == 0, blk, 0))

    ones = jnp.ones((2 * blk, HEAD_DIM), BF16)

    def rows(ref, start, d):
        idx = pl.ds(start, blk) if d == 1 else pl.ds(start, blk, stride=d)
        return ref[0, idx, :].astype(BF16)

    for idx, (window, d) in enumerate(DIL_PATTERNS):
        seg = blk * d
        for it in range(t // blk):
            r, b = it % d, it // d
            start = b * seg + r
            q = rows(q_ref, start, d)
            if b == 0:
                kp, vp = rows(kp_ref, t - seg + r, d), rows(vp_ref, t - seg + r, d)
            else:
                kp, vp = rows(kc_ref, start - seg, d), rows(vc_ref, start - seg, d)
            k = jnp.concatenate([kp, rows(kc_ref, start, d)], axis=0)
            v = jnp.concatenate([vp, rows(vc_ref, start, d)], axis=0)
            v = jnp.concatenate([v, ones], axis=1)
            s = lax.dot_general(q, k, (((1,), (1,)), ((), ())), preferred_element_type=F32)
            s = jnp.where(band_first if b == 0 else band, s, NEG)
            m = jnp.max(s, axis=-1, keepdims=True)
            pr = jnp.exp2(s - m)
            ol = jnp.dot(pr.astype(BF16), v, preferred_element_type=F32)
            l = ol[:, HEAD_DIM:]
            out_rows = pl.ds(start, blk) if d == 1 else pl.ds(start, blk, stride=d)
            o_sc[idx, out_rows, :] = ol[:, :HEAD_DIM] / l
            l_sc[idx, out_rows, :] = m + jnp.log2(l)

    lses = [l_sc[p] for p in range(len(DIL_PATTERNS))]
    mx = functools.reduce(jnp.maximum, lses)
    ws = [jnp.exp2(l - mx) for l in lses]
    num = sum(w * o_sc[p] for p, w in enumerate(ws))
    o_ref[...] = (num / sum(ws)).astype(BF16)


def _dilated(qkv):
    _, s, width = qkv.shape
    t = DIL_TILE
    npat = len(DIL_PATTERNS)

    def cur(which):
        return pl.BlockSpec((1, t, HEAD_DIM), lambda n, h: (which, n, h))

    def prv(which):
        return pl.BlockSpec((1, t, HEAD_DIM), lambda n, h: (which, jnp.maximum(n - 1, 0), h))

    return pl.pallas_call(
        _dil_kernel,
        grid=(s // t, N_HEADS),
        in_specs=[cur(0), cur(1), prv(1), cur(2), prv(2)],
        out_specs=pl.BlockSpec((t, HEAD_DIM), lambda n, h: (n, h)),
        out_shape=jax.ShapeDtypeStruct((s, width), BF16),
        scratch_shapes=[
            pltpu.VMEM((npat, t, HEAD_DIM), F32),
            pltpu.VMEM((npat, t, LANES), F32),
        ],
        compiler_params=_params("parallel", "parallel"),
        name="dilated",
    )(qkv, qkv, qkv, qkv, qkv)


def _oproj_kernel(x_ref, oa_ref, ob_ref, w_ref, g_ref, o_ref):
    half = oa_ref.shape[1]
    y = jnp.dot(oa_ref[...], w_ref[0, :half], preferred_element_type=F32)
    y = y + jnp.dot(ob_ref[...], w_ref[0, half:], preferred_element_type=F32)
    o_ref[...] = x_ref[...] + _rms(y, g_ref[0])


def _oproj(x, oa, ob, wo, g, layer, *, tm=512):
    s, d = x.shape
    width = oa.shape[1]
    return pl.pallas_call(
        _oproj_kernel,
        grid=(s // tm,),
        in_specs=[
            pl.BlockSpec((tm, d), lambda m: (m, 0)),
            pl.BlockSpec((tm, width), lambda m: (m, 0)),
            pl.BlockSpec((tm, width), lambda m: (m, 0)),
            pl.BlockSpec((1,) + wo.shape[1:], lambda m: (layer, 0, 0)),
            pl.BlockSpec((1, 1, d), lambda m: (layer, 0, 0)),
        ],
        out_specs=pl.BlockSpec((tm, d), lambda m: (m, 0)),
        out_shape=jax.ShapeDtypeStruct((s, d), F32),
        compiler_params=_params("parallel"),
        name="oproj",
    )(x, oa, ob, wo, g)


def _cast_kernel(w_ref, o_ref):
    o_ref[...] = w_ref[...].astype(BF16)


def _cast_layer(w, layer, *, steps=8):
    _, r, c = w.shape
    rows = r // steps
    return pl.pallas_call(
        _cast_kernel,
        grid=(steps,),
        in_specs=[pl.BlockSpec((1, rows, c), lambda i: (layer, i, 0))],
        out_specs=pl.BlockSpec((1, rows, c), lambda i: (0, i, 0)),
        out_shape=jax.ShapeDtypeStruct((1, r, c), BF16),
        compiler_params=_params("parallel"),
        name="cast_layer",
    )(w)


def _rope_tables(pos, dim):
    half = dim // 2
    inv = ROPE_THETA ** (-jnp.arange(0, dim, 2, dtype=F32) / dim)
    ang = pos.astype(F32)[:, None] * inv
    cos, sin = jnp.cos(ang), jnp.sin(ang)
    n = pos.shape[0]
    c = jnp.concatenate([cos, cos, jnp.ones((n, LANES - dim), F32)], axis=1)
    a = jnp.concatenate([-sin, jnp.zeros((n, LANES - half), F32)], axis=1)
    b = jnp.concatenate([jnp.zeros((n, half), F32), sin, jnp.zeros((n, LANES - dim), F32)], axis=1)
    return c, a, b


def kernel(x, positions, ffn1_pre_g, ffn1_post_g, ffn1_w_gate, ffn1_w_up, ffn1_w_down,
           mix_pre_g, mix_post_g, w_in, mla_q_norm_g, mla_w_uq, mla_kv_norm_g, mla_w_ukv,
           w_o, ffn2_pre_g, ffn2_post_g, ffn2_w_gate, ffn2_w_up, ffn2_w_down):
    batch, _, _ = x.shape
    depth = w_in.shape[0]
    bf = lambda w: w.astype(BF16)

    lat_cols = 2 * MLA_RANK
    w_in = bf(w_in)
    wc = w_in[:, :, :lat_cols]
    wkr = jnp.pad(w_in[:, :, lat_cols:lat_cols + MLA_ROPE], ((0, 0), (0, 0), (0, LANES - MLA_ROPE)))
    wd = w_in[:, :, lat_cols + MLA_ROPE:]
    wuq = bf(mla_w_uq).reshape(depth, MLA_RANK, N_HEADS, MLA_QK)
    wuq = jnp.pad(wuq, ((0, 0), (0, 0), (0, 0), (0, MLA_QK_PAD - MLA_QK))
                  ).reshape(depth, MLA_RANK, N_HEADS * MLA_QK_PAD)
    wukv = bf(mla_w_ukv)
    wo = bf(w_o)
    ffn_f32 = ((ffn1_w_gate, ffn1_w_up, ffn1_w_down), (ffn2_w_gate, ffn2_w_up, ffn2_w_down))
    ffn_gains = ((ffn1_pre_g, ffn1_post_g), (ffn2_pre_g, ffn2_post_g))
    ffn_bf16 = {(0, 0): tuple(_cast_layer(w, 0) for w in ffn_f32[0])}

    def ffn(xs, l, which):
        nxt = (l, 1) if which == 0 else (l + 1, 0)
        cast_next = None
        if nxt[0] < depth and nxt not in ffn_bf16:
            cast_next = ffn_f32[nxt[1]] + (nxt[0],)
        gpre, gpost = (g[l][None, None, :] for g in ffn_gains[which])
        xs, converted = _ffn(xs, gpre, gpost, ffn_bf16[(l, which)], 0, cast_next)
        if cast_next is not None:
            ffn_bf16[nxt] = converted
        return xs
    row = lambda g: g[:, None, :]
    mix_pre_g, mix_post_g, mla_q_norm_g, mla_kv_norm_g = map(
        row, (mix_pre_g, mix_post_g, mla_q_norm_g, mla_kv_norm_g))

    outs = []
    for bi in range(batch):
        xs = x[bi]
        tabs_a = _rope_tables(positions[bi], MLA_ROPE)
        cp, ap, bp = _rope_tables(positions[bi], PART_ROPE)
        qscale = math.log2(math.e) / math.sqrt(HEAD_DIM)
        tabs_p = (jnp.stack([cp * qscale, cp, jnp.ones_like(cp)]),
                  jnp.stack([ap * qscale, ap, jnp.zeros_like(ap)]),
                  jnp.stack([bp * qscale, bp, jnp.zeros_like(bp)]))
        for l in range(depth):
            xs = ffn(xs, l, 0)
            qkv = _dproj(xs, mix_pre_g, wd, tabs_p, l)
            qm, km, vm = _mproj(xs, mix_pre_g, wc, wkr, mla_q_norm_g, wuq,
                                mla_kv_norm_g, wukv, tabs_a, l)
            oa = _flash(qm, km, vm)
            ob = _dilated(qkv)
            xs = _oproj(xs, oa, ob, wo, mix_post_g, l)
            xs = ffn(xs, l, 1)
        outs.append(xs)
    return jnp.stack(outs, axis=0)
```

```python
import functools
import math

import jax
import jax.numpy as jnp
from jax import lax
from jax.experimental import pallas as pl
from jax.experimental.pallas import tpu as pltpu

F32 = jnp.float32
BF16 = jnp.bfloat16

HEAD_DIM = 128
N_HEADS = 8
MLA_RANK = 512
MLA_NOPE = 128
MLA_ROPE = 64
MLA_QK = MLA_NOPE + MLA_ROPE
MLA_QK_PAD = 256
PART_ROPE = 32
DIL_PATTERNS = ((128, 1), (512, 4), (2048, 16))
DIL_SPAN = 128
DIL_TILE = 2048
ROPE_THETA = 500000.0
RMS_EPS = 1e-6
NEG = -1e30
LANES = 128
VMEM_LIMIT = 58 * 1024 * 1024
FFN_NORM_ROWS = 256
FFN_DOT_ROWS = 512
DPROJ_DOT_ROWS = 128


def _rms(xf, g):
    ms = jnp.mean(xf * xf, axis=-1, keepdims=True)
    return xf * lax.rsqrt(ms + RMS_EPS) * g


def _rope(y, c, a, b, half):
    return y * c + pltpu.roll(y, LANES - half, axis=1) * a + pltpu.roll(y, half, axis=1) * b


def _params(*sem):
    return pltpu.CompilerParams(dimension_semantics=sem, vmem_limit_bytes=VMEM_LIMIT)


def _ffn_kernel(*refs, cast_next):
    x_ref, gpre_ref, gpost_ref, wg_ref, wu_ref, wd_ref = refs[:6]
    if cast_next:
        o_ref, nb_ref, r_ref = refs[9], refs[13], refs[14]
        for src_ref, dst_ref in zip(refs[6:9], refs[10:13]):
            dst_ref[...] = src_ref[...].astype(BF16)
    else:
        o_ref, nb_ref, r_ref = refs[6:]
    j = pl.program_id(1)

    tm = x_ref.shape[0]

    def row_chunks(size):
        return [slice(r, r + size) for r in range(0, tm, size)]

    last = pl.num_programs(1) - 1

    def chunk_sum(first):
        for rows in row_chunks(FFN_DOT_ROWS):
            nb = nb_ref[rows]
            g = jnp.dot(nb, wg_ref[0], preferred_element_type=F32)
            u = jnp.dot(nb, wu_ref[0], preferred_element_type=F32)
            h = (g * jax.nn.sigmoid(g) * u).astype(BF16)
            part = jnp.dot(h, wd_ref[0], preferred_element_type=F32)
            if first:
                o_ref[rows] = part
            else:
                o_ref[rows] += part

    @pl.when(j == 0)
    def _():
        for rows in row_chunks(FFN_NORM_ROWS):
            nb_ref[rows] = _rms(x_ref[rows], gpre_ref[0]).astype(BF16)
        chunk_sum(True)

    @pl.when(jnp.logical_and(j > 0, j < last))
    def _():
        chunk_sum(False)

    @pl.when(j == last)
    def _():
        chunk_sum(False)
        for rows in row_chunks(FFN_NORM_ROWS):
            y = o_ref[rows]
            r_ref[rows] = 0.5 * lax.rsqrt(jnp.mean(y * y, axis=-1, keepdims=True) + RMS_EPS)
        for rows in row_chunks(FFN_NORM_ROWS):
            o_ref[rows] = x_ref[rows] + o_ref[rows] * r_ref[rows] * gpost_ref[0]


def _ffn(x, gpre, gpost, weights, layer, cast_next=None, *, tm=1024, tn=512):
    s, d = x.shape
    wg, wu, wd = weights
    dff = wg.shape[-1]
    nm = s // tm
    dm = d // nm
    assert dff // tn >= 2, "first and last D_FF steps must be distinct grid steps"
    in_specs = [
        pl.BlockSpec((tm, d), lambda m, j: (m, 0)),
        pl.BlockSpec((1, 1, d), lambda m, j: (layer, 0, 0)),
        pl.BlockSpec((1, 1, d), lambda m, j: (layer, 0, 0)),
        pl.BlockSpec((1, d, tn), lambda m, j: (layer, 0, j)),
        pl.BlockSpec((1, d, tn), lambda m, j: (layer, 0, j)),
        pl.BlockSpec((1, tn, d), lambda m, j: (layer, j, 0)),
    ]
    out_specs = [pl.BlockSpec((tm, d), lambda m, j: (m, 0))]
    out_shape = [jax.ShapeDtypeStruct((s, d), F32)]
    args = [x, gpre, gpost, wg, wu, wd]
    if cast_next is not None:
        *nxt, nl = cast_next
        in_specs += [
            pl.BlockSpec((1, dm, tn), lambda m, j: (nl, m, j)),
            pl.BlockSpec((1, dm, tn), lambda m, j: (nl, m, j)),
            pl.BlockSpec((1, tn, dm), lambda m, j: (nl, j, m)),
        ]
        out_specs += [
            pl.BlockSpec((1, dm, tn), lambda m, j: (0, m, j)),
            pl.BlockSpec((1, dm, tn), lambda m, j: (0, m, j)),
            pl.BlockSpec((1, tn, dm), lambda m, j: (0, j, m)),
        ]
        out_shape += [jax.ShapeDtypeStruct((1,) + w.shape[1:], BF16) for w in nxt]
        args += nxt
    out = pl.pallas_call(
        functools.partial(_ffn_kernel, cast_next=cast_next is not None),
        grid=(nm, dff // tn),
        in_specs=in_specs,
        out_specs=out_specs,
        out_shape=out_shape,
        scratch_shapes=[pltpu.VMEM((tm, d), BF16), pltpu.VMEM((tm, 1), F32)],
        compiler_params=_params("parallel", "arbitrary"),
        name="ffn",
    )(*args)
    return out[0], tuple(out[1:])


def _dproj_kernel(x_ref, g_ref, w_ref, wx_ref, c_ref, a_ref, b_ref, o_ref, nb_ref):
    j = pl.program_id(1)
    shift = MLA_ROPE
    piece = 2 * HEAD_DIM

    def project():
        w_cat = jnp.concatenate([w_ref[0], wx_ref[0]], axis=1)
        for lo in range(0, N_HEADS * HEAD_DIM, piece):
            wp = w_cat[:, shift + lo:shift + lo + piece]
            for r in range(0, x_ref.shape[0], DPROJ_DOT_ROWS):
                rows = slice(r, r + DPROJ_DOT_ROWS)
                y = jnp.dot(nb_ref[rows], wp, preferred_element_type=F32)
                c, a, b = c_ref[0, rows], a_ref[0, rows], b_ref[0, rows]
                for h in range(piece // HEAD_DIM):
                    sl = slice(h * HEAD_DIM, (h + 1) * HEAD_DIM)
                    osl = slice(lo + h * HEAD_DIM, lo + (h + 1) * HEAD_DIM)
                    o_ref[0, rows, osl] = _rope(y[:, sl], c, a, b, PART_ROPE // 2)

    @pl.when(j == 0)
    def _():
        for r in range(0, x_ref.shape[0], FFN_NORM_ROWS):
            rows = slice(r, r + FFN_NORM_ROWS)
            nb_ref[rows] = _rms(x_ref[rows], g_ref[0]).astype(BF16)
        project()

    @pl.when(j > 0)
    def _():
        project()


def _dproj(x, g, w_in, tabs, layer, *, tm=1024):
    s, d = x.shape
    width = N_HEADS * HEAD_DIM
    first = 2 * MLA_RANK // width
    tab_spec = pl.BlockSpec((1, tm, LANES), lambda m, j: (j, m, 0))
    return pl.pallas_call(
        _dproj_kernel,
        grid=(s // tm, 3),
        in_specs=[
            pl.BlockSpec((tm, d), lambda m, j: (m, 0)),
            pl.BlockSpec((1, 1, d), lambda m, j: (layer, 0, 0)),
            pl.BlockSpec((1, d, width), lambda m, j: (layer, 0, first + j)),
            pl.BlockSpec((1, d, LANES), lambda m, j: (layer, 0, (first + j + 1) * (width // LANES))),
            tab_spec, tab_spec, tab_spec,
        ],
        out_specs=pl.BlockSpec((1, tm, width), lambda m, j: (j, m, 0)),
        out_shape=jax.ShapeDtypeStruct((3, s, width), F32),
        scratch_shapes=[pltpu.VMEM((tm, d), BF16)],
        compiler_params=_params("parallel", "arbitrary"),
        name="dproj",
    )(x, g, w_in, w_in, *tabs)


def _mproj_kernel(x_ref, g_ref, wc_ref, wkr_ref, gq_ref, wuq_ref, gkv_ref, wukv_ref,
                  c_ref, a_ref, b_ref, q_ref, k_ref, v_ref):
    nb = _rms(x_ref[...], g_ref[0]).astype(BF16)
    lat = jnp.dot(nb, wc_ref[0], preferred_element_type=F32)
    cq = _rms(lat[:, :MLA_RANK], gq_ref[0]).astype(BF16)
    ckv = _rms(lat[:, MLA_RANK:], gkv_ref[0]).astype(BF16)
    c, a, b = c_ref[...], a_ref[...], b_ref[...]
    half = MLA_ROPE // 2
    kr = jnp.dot(nb, wkr_ref[0], preferred_element_type=F32)
    lane = lax.broadcasted_iota(jnp.int32, kr.shape, 1)
    kr = jnp.where(lane < MLA_ROPE, _rope(kr, c, a, b, half), 0.0).astype(BF16)
    q = jnp.dot(cq, wuq_ref[0], preferred_element_type=F32)
    kv = jnp.dot(ckv, wukv_ref[0], preferred_element_type=F32)
    scale = math.log2(math.e) / math.sqrt(MLA_QK)
    ones = jnp.ones((x_ref.shape[0], HEAD_DIM), BF16)
    for h in range(N_HEADS):
        lo = h * MLA_QK_PAD
        q_ref[h, :, :MLA_NOPE] = (q[:, lo:lo + MLA_NOPE] * scale).astype(BF16)
        qr = _rope(q[:, lo + MLA_NOPE:lo + MLA_QK_PAD], c, a, b, half)
        q_ref[h, :, MLA_NOPE:] = (qr * scale).astype(BF16)
        k_ref[h, :, :MLA_NOPE] = kv[:, lo:lo + MLA_NOPE].astype(BF16)
        k_ref[h, :, MLA_NOPE:] = kr
        v_ref[h, :, :HEAD_DIM] = kv[:, lo + MLA_NOPE:lo + MLA_QK_PAD].astype(BF16)
        v_ref[h, :, HEAD_DIM:] = ones


def _mproj(x, g, wc, wkr, gq, wuq, gkv, wukv, tabs, layer, *, tm=512):
    s, d = x.shape
    tab_spec = pl.BlockSpec((tm, LANES), lambda m: (m, 0))

    def wspec(w):
        return pl.BlockSpec((1,) + w.shape[1:], lambda m: (layer, 0, 0))

    def gspec(gv):
        return pl.BlockSpec((1, 1, gv.shape[2]), lambda m: (layer, 0, 0))

    return pl.pallas_call(
        _mproj_kernel,
        grid=(s // tm,),
        in_specs=[
            pl.BlockSpec((tm, d), lambda m: (m, 0)),
            gspec(g),
            pl.BlockSpec((1, d, 2 * MLA_RANK), lambda m: (layer, 0, 0)),
            pl.BlockSpec((1, d, LANES), lambda m: (layer, 0, 2 * MLA_RANK // LANES)),
            gspec(gq), wspec(wuq), gspec(gkv), wspec(wukv),
            tab_spec, tab_spec, tab_spec,
        ],
        out_specs=[
            pl.BlockSpec((N_HEADS, tm, MLA_QK_PAD), lambda m: (0, m, 0)),
            pl.BlockSpec((N_HEADS, tm, MLA_QK_PAD), lambda m: (0, m, 0)),
            pl.BlockSpec((N_HEADS, tm, 2 * HEAD_DIM), lambda m: (0, m, 0)),
        ],
        out_shape=[
            jax.ShapeDtypeStruct((N_HEADS, s, MLA_QK_PAD), BF16),
            jax.ShapeDtypeStruct((N_HEADS, s, MLA_QK_PAD), BF16),
            jax.ShapeDtypeStruct((N_HEADS, s, 2 * HEAD_DIM), BF16),
        ],
        compiler_params=_params("parallel"),
        name="mproj",
    )(x, g, wc, wkr, gq, wuq, gkv, wukv, *tabs)


def _flash_kernel(q_ref, k_ref, v_ref, o_ref, m_ref, acc_ref,
                  sa_ref, sb_ref, sc_ref, pa_ref, pb_ref, pc_ref, *, tq):
    half = tq // 2
    i = pl.program_id(1)
    last = pl.num_programs(1) - 1
    m_ref[...] = jnp.full_like(m_ref, NEG)
    acc_ref[...] = jnp.zeros_like(acc_ref)

    def key_rows(c, n=tq):
        return pl.ds(pl.multiple_of(c * tq, tq), n)

    def scores(c, s_ref, p_ref, tile=i):
        kc = k_ref[0, key_rows(c), :]
        for st in range(2):
            q = q_ref[0, pl.ds(pl.multiple_of(tile * tq + st * half, half), half), :]
            s = lax.dot_general(q, kc, (((1,), (1,)), ((), ())), preferred_element_type=F32)
            s_ref[st] = s
            pm = s[:, :LANES]
            for t in range(1, tq // LANES):
                pm = jnp.maximum(pm, s[:, t * LANES:(t + 1) * LANES])
            p_ref[st] = pm

    def update(st, s, row_max, vc):
        m_prev = m_ref[st]
        m_new = jnp.maximum(m_prev, row_max)
        alpha = jnp.exp2(m_prev - m_new)
        pr = jnp.exp2(s - m_new).astype(BF16)
        acc_ref[st] = alpha * acc_ref[st] + jnp.dot(pr, vc, preferred_element_type=F32)
        m_ref[st] = m_new

    def accumulate(c, s_ref, p_ref):
        vc = v_ref[0, key_rows(c), :]
        for st in range(2):
            update(st, s_ref[st], jnp.max(p_ref[st], axis=-1, keepdims=True), vc)

    def finish(s_ref):
        scores(0, sc_ref, pc_ref, tile=jnp.minimum(i + 1, last))
        s0 = s_ref[0, :, :half]
        rows = lax.broadcasted_iota(jnp.int32, s0.shape, 0)
        cols = lax.broadcasted_iota(jnp.int32, s0.shape, 1)
        s0 = jnp.where(cols <= rows, s0, NEG)
        update(0, s0, jnp.max(s0, axis=-1, keepdims=True), v_ref[0, key_rows(i, half), :])
        s1 = s_ref[1]
        rows = lax.broadcasted_iota(jnp.int32, s1.shape, 0)
        cols = lax.broadcasted_iota(jnp.int32, s1.shape, 1)
        s1 = jnp.where(cols <= rows + half, s1, NEG)
        update(1, s1, jnp.max(s1, axis=-1, keepdims=True), v_ref[0, key_rows(i), :])
        for st in range(2):
            acc = acc_ref[st]
            o_ref[st * half:(st + 1) * half, :] = (
                acc[:, :HEAD_DIM] / acc[:, HEAD_DIM:]).astype(BF16)

    @pl.when(i == 0)
    def _():
        scores(0, sa_ref, pa_ref)
        finish(sa_ref)

    @pl.when(i > 0)
    def _():
        scores(1, sa_ref, pa_ref)
        accumulate(0, sc_ref, pc_ref)

    def pair(jj, carry):
        c = 2 * jj + 1
        scores(c + 1, sb_ref, pb_ref)
        accumulate(c, sa_ref, pa_ref)
        scores(c + 2, sa_ref, pa_ref)
        accumulate(c + 1, sb_ref, pb_ref)
        return carry

    lax.fori_loop(0, lax.div(jnp.maximum(i - 1, 0), 2), pair, 0)

    @pl.when(lax.rem(i, 2) == 1)
    def _():
        finish(sa_ref)

    @pl.when(jnp.logical_and(lax.rem(i, 2) == 0, i > 0))
    def _():
        scores(i, sb_ref, pb_ref)
        accumulate(i - 1, sa_ref, pa_ref)
        finish(sb_ref)


def _flash(q, k, v, *, tq=1024):
    _, s, _ = q.shape
    score_buf = pltpu.VMEM((2, tq // 2, tq), F32)
    pmax_buf = pltpu.VMEM((2, tq // 2, LANES), F32)
    return pl.pallas_call(
        functools.partial(_flash_kernel, tq=tq),
        grid=(N_HEADS, s // tq),
        in_specs=[
            pl.BlockSpec((1, s, MLA_QK_PAD), lambda h, i: (h, 0, 0)),
            pl.BlockSpec((1, s, MLA_QK_PAD), lambda h, i: (h, 0, 0)),
            pl.BlockSpec((1, s, 2 * HEAD_DIM), lambda h, i: (h, 0, 0)),
        ],
        out_specs=pl.BlockSpec((tq, HEAD_DIM), lambda h, i: (i, h)),
        out_shape=jax.ShapeDtypeStruct((s, N_HEADS * HEAD_DIM), BF16),
        scratch_shapes=[
            pltpu.VMEM((2, tq // 2, 1), F32),
            pltpu.VMEM((2, tq // 2, 2 * HEAD_DIM), F32),
            score_buf, score_buf, score_buf, pmax_buf, pmax_buf, pmax_buf,
        ],
        compiler_params=_params("arbitrary", "arbitrary"),
        name="mla_flash",
    )(q, k, v)


def _dil_kernel(q_ref, kc_ref, kp_ref, vc_ref, vp_ref, o_ref, o_sc, l_sc):
    t = q_ref.shape[1]
    blk = DIL_SPAN
    n = pl.program_id(0)

    qi = lax.broadcasted_iota(jnp.int32, (blk, 2 * blk), 0)
    kj = lax.broadcasted_iota(jnp.int32, (blk, 2 * blk), 1)
    diff = kj - qi
    band = (diff >= 0) & (diff <= DIL_SPAN)
    band_first = band & (kj >= jnp.where(n == 0, blk, 0))

    ones = jnp.ones((2 * blk, HEAD_DIM), BF16)

    def rows(ref, start, d):
        idx = pl.ds(start, blk) if d == 1 else pl.ds(start, blk, stride=d)
        return ref[0, idx, :].astype(BF16)

    for idx, (window, d) in enumerate(DIL_PATTERNS):
        seg = blk * d
        for it in range(t // blk):
            r, b = it % d, it // d
            start = b * seg + r
            q = rows(q_ref, start, d)
            if b == 0:
                kp, vp = rows(kp_ref, t - seg + r, d), rows(vp_ref, t - seg + r, d)
            else:
                kp, vp = rows(kc_ref, start - seg, d), rows(vc_ref, start - seg, d)
            k = jnp.concatenate([kp, rows(kc_ref, start, d)], axis=0)
            v = jnp.concatenate([vp, rows(vc_ref, start, d)], axis=0)
            v = jnp.concatenate([v, ones], axis=1)
            s = lax.dot_general(q, k, (((1,), (1,)), ((), ())), preferred_element_type=F32)
            s = jnp.where(band_first if b == 0 else band, s, NEG)
            m = jnp.max(s, axis=-1, keepdims=True)
            pr = jnp.exp2(s - m)
            ol = jnp.dot(pr.astype(BF16), v, preferred_element_type=F32)
            l = ol[:, HEAD_DIM:]
            out_rows = pl.ds(start, blk) if d == 1 else pl.ds(start, blk, stride=d)
            o_sc[idx, out_rows, :] = ol[:, :HEAD_DIM] / l
            l_sc[idx, out_rows, :] = m + jnp.log2(l)

    lses = [l_sc[p] for p in range(len(DIL_PATTERNS))]
    mx = functools.reduce(jnp.maximum, lses)
    ws = [jnp.exp2(l - mx) for l in lses]
    num = sum(w * o_sc[p] for p, w in enumerate(ws))
    o_ref[...] = (num / sum(ws)).astype(BF16)


def _dilated(qkv):
    _, s, width = qkv.shape
    t = DIL_TILE
    npat = len(DIL_PATTERNS)

    def cur(which):
        return pl.BlockSpec((1, t, HEAD_DIM), lambda n, h: (which, n, h))

    def prv(which):
        return pl.BlockSpec((1, t, HEAD_DIM), lambda n, h: (which, jnp.maximum(n - 1, 0), h))

    return pl.pallas_call(
        _dil_kernel,
        grid=(s // t, N_HEADS),
        in_specs=[cur(0), cur(1), prv(1), cur(2), prv(2)],
        out_specs=pl.BlockSpec((t, HEAD_DIM), lambda n, h: (n, h)),
        out_shape=jax.ShapeDtypeStruct((s, width), BF16),
        scratch_shapes=[
            pltpu.VMEM((npat, t, HEAD_DIM), F32),
            pltpu.VMEM((npat, t, LANES), F32),
        ],
        compiler_params=_params("parallel", "parallel"),
        name="dilated",
    )(qkv, qkv, qkv, qkv, qkv)


def _oproj_kernel(x_ref, oa_ref, ob_ref, w_ref, g_ref, o_ref):
    half = oa_ref.shape[1]
    y = jnp.dot(oa_ref[...], w_ref[0, :half], preferred_element_type=F32)
    y = y + jnp.dot(ob_ref[...], w_ref[0, half:], preferred_element_type=F32)
    o_ref[...] = x_ref[...] + _rms(y, g_ref[0])


def _oproj(x, oa, ob, wo, g, layer, *, tm=512):
    s, d = x.shape
    width = oa.shape[1]
    return pl.pallas_call(
        _oproj_kernel,
        grid=(s // tm,),
        in_specs=[
            pl.BlockSpec((tm, d), lambda m: (m, 0)),
            pl.BlockSpec((tm, width), lambda m: (m, 0)),
            pl.BlockSpec((tm, width), lambda m: (m, 0)),
            pl.BlockSpec((1,) + wo.shape[1:], lambda m: (layer, 0, 0)),
            pl.BlockSpec((1, 1, d), lambda m: (layer, 0, 0)),
        ],
        out_specs=pl.BlockSpec((tm, d), lambda m: (m, 0)),
        out_shape=jax.ShapeDtypeStruct((s, d), F32),
        compiler_params=_params("parallel"),
        name="oproj",
    )(x, oa, ob, wo, g)


def _cast_kernel(w_ref, o_ref):
    o_ref[...] = w_ref[...].astype(BF16)


def _cast_layer(w, layer, *, steps=8):
    _, r, c = w.shape
    rows = r // steps
    return pl.pallas_call(
        _cast_kernel,
        grid=(steps,),
        in_specs=[pl.BlockSpec((1, rows, c), lambda i: (layer, i, 0))],
        out_specs=pl.BlockSpec((1, rows, c), lambda i: (0, i, 0)),
        out_shape=jax.ShapeDtypeStruct((1, r, c), BF16),
        compiler_params=_params("parallel"),
        name="cast_layer",
    )(w)


def _rope_tables(pos, dim):
    half = dim // 2
    inv = ROPE_THETA ** (-jnp.arange(0, dim, 2, dtype=F32) / dim)
    ang = pos.astype(F32)[:, None] * inv
    cos, sin = jnp.cos(ang), jnp.sin(ang)
    n = pos.shape[0]
    c = jnp.concatenate([cos, cos, jnp.ones((n, LANES - dim), F32)], axis=1)
    a = jnp.concatenate([-sin, jnp.zeros((n, LANES - half), F32)], axis=1)
    b = jnp.concatenate([jnp.zeros((n, half), F32), sin, jnp.zeros((n, LANES - dim), F32)], axis=1)
    return c, a, b


def kernel(x, positions, ffn1_pre_g, ffn1_post_g, ffn1_w_gate, ffn1_w_up, ffn1_w_down,
           mix_pre_g, mix_post_g, w_in, mla_q_norm_g, mla_w_uq, mla_kv_norm_g, mla_w_ukv,
           w_o, ffn2_pre_g, ffn2_post_g, ffn2_w_gate, ffn2_w_up, ffn2_w_down):
    batch, _, _ = x.shape
    depth = w_in.shape[0]
    bf = lambda w: w.astype(BF16)

    w_in = bf(w_in)
    wc = wkr = wd = w_in
    wuq = bf(mla_w_uq).reshape(depth, MLA_RANK, N_HEADS, MLA_QK)
    wuq = jnp.pad(wuq, ((0, 0), (0, 0), (0, 0), (0, MLA_QK_PAD - MLA_QK))
                  ).reshape(depth, MLA_RANK, N_HEADS * MLA_QK_PAD)
    wukv = bf(mla_w_ukv)
    wo = bf(w_o)
    ffn_f32 = ((ffn1_w_gate, ffn1_w_up, ffn1_w_down), (ffn2_w_gate, ffn2_w_up, ffn2_w_down))
    ffn_gains = ((ffn1_pre_g, ffn1_post_g), (ffn2_pre_g, ffn2_post_g))
    ffn_bf16 = {(0, 0): tuple(_cast_layer(w, 0) for w in ffn_f32[0])}

    def ffn(xs, l, which):
        nxt = (l, 1) if which == 0 else (l + 1, 0)
        cast_next = None
        if nxt[0] < depth and nxt not in ffn_bf16:
            cast_next = ffn_f32[nxt[1]] + (nxt[0],)
        gpre, gpost = (g[l][None, None, :] for g in ffn_gains[which])
        xs, converted = _ffn(xs, gpre, gpost, ffn_bf16[(l, which)], 0, cast_next)
        if cast_next is not None:
            ffn_bf16[nxt] = converted
        return xs
    row = lambda g: g[:, None, :]
    mix_pre_g, mix_post_g, mla_q_norm_g, mla_kv_norm_g = map(
        row, (mix_pre_g, mix_post_g, mla_q_norm_g, mla_kv_norm_g))

    outs = []
    for bi in range(batch):
        xs = x[bi]
        tabs_a = _rope_tables(positions[bi], MLA_ROPE)
        cp, ap, bp = _rope_tables(positions[bi], PART_ROPE)
        qscale = math.log2(math.e) / math.sqrt(HEAD_DIM)
        tabs_p = (jnp.stack([cp * qscale, cp, jnp.ones_like(cp)]),
                  jnp.stack([ap * qscale, ap, jnp.zeros_like(ap)]),
                  jnp.stack([bp * qscale, bp, jnp.zeros_like(bp)]))
        for l in range(depth):
            xs = ffn(xs, l, 0)
            qkv = _dproj(xs, mix_pre_g, wd, tabs_p, l)
            qm, km, vm = _mproj(xs, mix_pre_g, wc, wkr, mla_q_norm_g, wuq,
                                mla_kv_norm_g, wukv, tabs_a, l)
            oa = _flash(qm, km, vm)
            ob = _dilated(qkv)
            xs = _oproj(xs, oa, ob, wo, mix_post_g, l)
            xs = ffn(xs, l, 1)
        outs.append(xs)
    return jnp.stack(outs, axis=0)
```

```python
import functools
import math

import jax
import jax.numpy as jnp
from jax import lax
from jax.experimental import pallas as pl
from jax.experimental.pallas import tpu as pltpu

F32 = jnp.float32
BF16 = jnp.bfloat16

HEAD_DIM = 128
N_HEADS = 8
MLA_RANK = 512
MLA_NOPE = 128
MLA_ROPE = 64
MLA_QK = MLA_NOPE + MLA_ROPE
MLA_QK_PAD = 256
PART_ROPE = 32
DIL_PATTERNS = ((128, 1), (512, 4), (2048, 16))
DIL_SPAN = 128
DIL_TILE = 2048
ROPE_THETA = 500000.0
RMS_EPS = 1e-6
NEG = -1e30
LANES = 128
VMEM_LIMIT = 58 * 1024 * 1024
FFN_NORM_ROWS = 256
FFN_DOT_ROWS = 512
DPROJ_DOT_ROWS = 128


def _rms(xf, g):
    ms = jnp.mean(xf * xf, axis=-1, keepdims=True)
    return xf * lax.rsqrt(ms + RMS_EPS) * g


def _rope(y, c, a, b, half):
    return y * c + pltpu.roll(y, LANES - half, axis=1) * a + pltpu.roll(y, half, axis=1) * b


def _params(*sem):
    return pltpu.CompilerParams(dimension_semantics=sem, vmem_limit_bytes=VMEM_LIMIT)


def _ffn_kernel(*refs, cast_next):
    x_ref, gpre_ref, gpost_ref, wg_ref, wu_ref, wd_ref = refs[:6]
    if cast_next:
        o_ref, nb_ref, r_ref = refs[9], refs[13], refs[14]
        for src_ref, dst_ref in zip(refs[6:9], refs[10:13]):
            dst_ref[...] = src_ref[...].astype(BF16)
    else:
        o_ref, nb_ref, r_ref = refs[6:]
    j = pl.program_id(1)

    tm = x_ref.shape[0]

    def row_chunks(size):
        return [slice(r, r + size) for r in range(0, tm, size)]

    last = pl.num_programs(1) - 1

    def chunk_sum(first):
        for rows in row_chunks(FFN_DOT_ROWS):
            nb = nb_ref[rows]
            g = jnp.dot(nb, wg_ref[0], preferred_element_type=F32)
            u = jnp.dot(nb, wu_ref[0], preferred_element_type=F32)
            h = (g * jax.nn.sigmoid(g) * u).astype(BF16)
            part = jnp.dot(h, wd_ref[0], preferred_element_type=F32)
            if first:
                o_ref[rows] = part
            else:
                o_ref[rows] += part

    @pl.when(j == 0)
    def _():
        for rows in row_chunks(FFN_NORM_ROWS):
            nb_ref[rows] = _rms(x_ref[rows], gpre_ref[0]).astype(BF16)
        chunk_sum(True)

    @pl.when(jnp.logical_and(j > 0, j < last))
    def _():
        chunk_sum(False)

    @pl.when(j == last)
    def _():
        chunk_sum(False)
        for rows in row_chunks(FFN_NORM_ROWS):
            y = o_ref[rows]
            r_ref[rows] = 0.5 * lax.rsqrt(jnp.mean(y * y, axis=-1, keepdims=True) + RMS_EPS)
        for rows in row_chunks(FFN_NORM_ROWS):
            o_ref[rows] = x_ref[rows] + o_ref[rows] * r_ref[rows] * gpost_ref[0]


def _ffn(x, gpre, gpost, weights, layer, cast_next=None, *, tm=1024, tn=512):
    s, d = x.shape
    wg, wu, wd = weights
    dff = wg.shape[-1]
    nm = s // tm
    dm = d // nm
    assert dff // tn >= 2, "first and last D_FF steps must be distinct grid steps"
    in_specs = [
        pl.BlockSpec((tm, d), lambda m, j: (m, 0)),
        pl.BlockSpec((1, 1, d), lambda m, j: (layer, 0, 0)),
        pl.BlockSpec((1, 1, d), lambda m, j: (layer, 0, 0)),
        pl.BlockSpec((1, d, tn), lambda m, j: (layer, 0, j)),
        pl.BlockSpec((1, d, tn), lambda m, j: (layer, 0, j)),
        pl.BlockSpec((1, tn, d), lambda m, j: (layer, j, 0)),
    ]
    out_specs = [pl.BlockSpec((tm, d), lambda m, j: (m, 0))]
    out_shape = [jax.ShapeDtypeStruct((s, d), F32)]
    args = [x, gpre, gpost, wg, wu, wd]
    if cast_next is not None:
        *nxt, nl = cast_next
        in_specs += [
            pl.BlockSpec((1, dm, tn), lambda m, j: (nl, m, j)),
            pl.BlockSpec((1, dm, tn), lambda m, j: (nl, m, j)),
            pl.BlockSpec((1, tn, dm), lambda m, j: (nl, j, m)),
        ]
        out_specs += [
            pl.BlockSpec((1, dm, tn), lambda m, j: (0, m, j)),
            pl.BlockSpec((1, dm, tn), lambda m, j: (0, m, j)),
            pl.BlockSpec((1, tn, dm), lambda m, j: (0, j, m)),
        ]
        out_shape += [jax.ShapeDtypeStruct((1,) + w.shape[1:], BF16) for w in nxt]
        args += nxt
    out = pl.pallas_call(
        functools.partial(_ffn_kernel, cast_next=cast_next is not None),
        grid=(nm, dff // tn),
        in_specs=in_specs,
        out_specs=out_specs,
        out_shape=out_shape,
        scratch_shapes=[pltpu.VMEM((tm, d), BF16), pltpu.VMEM((tm, 1), F32)],
        compiler_params=_params("parallel", "arbitrary"),
        name="ffn",
    )(*args)
    return out[0], tuple(out[1:])


def _dproj_kernel(x_ref, g_ref, w_ref, wx_ref, c_ref, a_ref, b_ref, o_ref, nb_ref):
    j = pl.program_id(1)
    shift = MLA_ROPE
    piece = 2 * HEAD_DIM

    def project():
        w_cat = jnp.concatenate([w_ref[0], wx_ref[0]], axis=1)
        for lo in range(0, N_HEADS * HEAD_DIM, piece):
            wp = w_cat[:, shift + lo:shift + lo + piece]
            for r in range(0, x_ref.shape[0], DPROJ_DOT_ROWS):
                rows = slice(r, r + DPROJ_DOT_ROWS)
                y = jnp.dot(nb_ref[rows], wp, preferred_element_type=F32)
                c, a, b = c_ref[0, rows], a_ref[0, rows], b_ref[0, rows]
                for h in range(piece // HEAD_DIM):
                    sl = slice(h * HEAD_DIM, (h + 1) * HEAD_DIM)
                    osl = slice(lo + h * HEAD_DIM, lo + (h + 1) * HEAD_DIM)
                    o_ref[0, rows, osl] = _rope(y[:, sl], c, a, b, PART_ROPE // 2)

    @pl.when(j == 0)
    def _():
        for r in range(0, x_ref.shape[0], FFN_NORM_ROWS):
            rows = slice(r, r + FFN_NORM_ROWS)
            nb_ref[rows] = _rms(x_ref[rows], g_ref[0]).astype(BF16)
        project()

    @pl.when(j > 0)
    def _():
        project()


def _dproj(x, g, w_in, tabs, layer, *, tm=1024):
    s, d = x.shape
    width = N_HEADS * HEAD_DIM
    first = 2 * MLA_RANK // width
    tab_spec = pl.BlockSpec((1, tm, LANES), lambda m, j: (j, m, 0))
    return pl.pallas_call(
        _dproj_kernel,
        grid=(s // tm, 3),
        in_specs=[
            pl.BlockSpec((tm, d), lambda m, j: (m, 0)),
            pl.BlockSpec((1, 1, d), lambda m, j: (layer, 0, 0)),
            pl.BlockSpec((1, d, width), lambda m, j: (layer, 0, first + j)),
            pl.BlockSpec((1, d, LANES), lambda m, j: (layer, 0, (first + j + 1) * (width // LANES))),
            tab_spec, tab_spec, tab_spec,
        ],
        out_specs=pl.BlockSpec((1, tm, width), lambda m, j: (j, m, 0)),
        out_shape=jax.ShapeDtypeStruct((3, s, width), F32),
        scratch_shapes=[pltpu.VMEM((tm, d), BF16)],
        compiler_params=_params("parallel", "arbitrary"),
        name="dproj",
    )(x, g, w_in, w_in, *tabs)


def _mproj_kernel(x_ref, g_ref, wc_ref, wkr_ref, gq_ref, wuq_ref, gkv_ref, wukv_ref,
                  c_ref, a_ref, b_ref, q_ref, k_ref, v_ref):
    nb = _rms(x_ref[...], g_ref[0]).astype(BF16)
    lat = jnp.dot(nb, wc_ref[0], preferred_element_type=F32)
    cq = _rms(lat[:, :MLA_RANK], gq_ref[0]).astype(BF16)
    ckv = _rms(lat[:, MLA_RANK:], gkv_ref[0]).astype(BF16)
    c, a, b = c_ref[...], a_ref[...], b_ref[...]
    half = MLA_ROPE // 2
    kr = jnp.dot(nb, wkr_ref[0], preferred_element_type=F32)
    lane = lax.broadcasted_iota(jnp.int32, kr.shape, 1)
    kr = jnp.where(lane < MLA_ROPE, _rope(kr, c, a, b, half), 0.0).astype(BF16)
    q = jnp.dot(cq, wuq_ref[0], preferred_element_type=F32)
    kv = jnp.dot(ckv, wukv_ref[0], preferred_element_type=F32)
    scale = math.log2(math.e) / math.sqrt(MLA_QK)
    ones = jnp.ones((x_ref.shape[0], HEAD_DIM), BF16)
    qpad = jnp.zeros((x_ref.shape[0], MLA_QK_PAD - MLA_QK), F32)
    for h in range(N_HEADS):
        lo = h * MLA_QK_PAD
        qlo = h * MLA_QK
        q_ref[h, :, :MLA_NOPE] = (q[:, qlo:qlo + MLA_NOPE] * scale).astype(BF16)
        qr = jnp.concatenate([q[:, qlo + MLA_NOPE:qlo + MLA_QK], qpad], axis=1)
        qr = _rope(qr, c, a, b, half)
        q_ref[h, :, MLA_NOPE:] = (qr * scale).astype(BF16)
        k_ref[h, :, :MLA_NOPE] = kv[:, lo:lo + MLA_NOPE].astype(BF16)
        k_ref[h, :, MLA_NOPE:] = kr
        v_ref[h, :, :HEAD_DIM] = kv[:, lo + MLA_NOPE:lo + MLA_QK_PAD].astype(BF16)
        v_ref[h, :, HEAD_DIM:] = ones


def _mproj(x, g, wc, wkr, gq, wuq, gkv, wukv, tabs, layer, *, tm=512):
    s, d = x.shape
    tab_spec = pl.BlockSpec((tm, LANES), lambda m: (m, 0))

    def wspec(w):
        return pl.BlockSpec((1,) + w.shape[1:], lambda m: (layer, 0, 0))

    def gspec(gv):
        return pl.BlockSpec((1, 1, gv.shape[2]), lambda m: (layer, 0, 0))

    return pl.pallas_call(
        _mproj_kernel,
        grid=(s // tm,),
        in_specs=[
            pl.BlockSpec((tm, d), lambda m: (m, 0)),
            gspec(g),
            pl.BlockSpec((1, d, 2 * MLA_RANK), lambda m: (layer, 0, 0)),
            pl.BlockSpec((1, d, LANES), lambda m: (layer, 0, 2 * MLA_RANK // LANES)),
            gspec(gq), wspec(wuq), gspec(gkv), wspec(wukv),
            tab_spec, tab_spec, tab_spec,
        ],
        out_specs=[
            pl.BlockSpec((N_HEADS, tm, MLA_QK_PAD), lambda m: (0, m, 0)),
            pl.BlockSpec((N_HEADS, tm, MLA_QK_PAD), lambda m: (0, m, 0)),
            pl.BlockSpec((N_HEADS, tm, 2 * HEAD_DIM), lambda m: (0, m, 0)),
        ],
        out_shape=[
            jax.ShapeDtypeStruct((N_HEADS, s, MLA_QK_PAD), BF16),
            jax.ShapeDtypeStruct((N_HEADS, s, MLA_QK_PAD), BF16),
            jax.ShapeDtypeStruct((N_HEADS, s, 2 * HEAD_DIM), BF16),
        ],
        compiler_params=_params("parallel"),
        name="mproj",
    )(x, g, wc, wkr, gq, wuq, gkv, wukv, *tabs)


def _flash_kernel(q_ref, k_ref, v_ref, o_ref, m_ref, acc_ref,
                  sa_ref, sb_ref, sc_ref, pa_ref, pb_ref, pc_ref, *, tq):
    half = tq // 2
    i = pl.program_id(1)
    last = pl.num_programs(1) - 1
    m_ref[...] = jnp.full_like(m_ref, NEG)
    acc_ref[...] = jnp.zeros_like(acc_ref)

    def key_rows(c, n=tq):
        return pl.ds(pl.multiple_of(c * tq, tq), n)

    def scores(c, s_ref, p_ref, tile=i):
        kc = k_ref[0, key_rows(c), :]
        for st in range(2):
            q = q_ref[0, pl.ds(pl.multiple_of(tile * tq + st * half, half), half), :]
            s = lax.dot_general(q, kc, (((1,), (1,)), ((), ())), preferred_element_type=F32)
            s_ref[st] = s
            pm = s[:, :LANES]
            for t in range(1, tq // LANES):
                pm = jnp.maximum(pm, s[:, t * LANES:(t + 1) * LANES])
            p_ref[st] = pm

    def update(st, s, row_max, vc):
        m_prev = m_ref[st]
        m_new = jnp.maximum(m_prev, row_max)
        alpha = jnp.exp2(m_prev - m_new)
        pr = jnp.exp2(s - m_new).astype(BF16)
        acc_ref[st] = alpha * acc_ref[st] + jnp.dot(pr, vc, preferred_element_type=F32)
        m_ref[st] = m_new

    def accumulate(c, s_ref, p_ref):
        vc = v_ref[0, key_rows(c), :]
        for st in range(2):
            update(st, s_ref[st], jnp.max(p_ref[st], axis=-1, keepdims=True), vc)

    def finish(s_ref):
        scores(0, sc_ref, pc_ref, tile=jnp.minimum(i + 1, last))
        s0 = s_ref[0, :, :half]
        rows = lax.broadcasted_iota(jnp.int32, s0.shape, 0)
        cols = lax.broadcasted_iota(jnp.int32, s0.shape, 1)
        s0 = jnp.where(cols <= rows, s0, NEG)
        update(0, s0, jnp.max(s0, axis=-1, keepdims=True), v_ref[0, key_rows(i, half), :])
        s1 = s_ref[1]
        rows = lax.broadcasted_iota(jnp.int32, s1.shape, 0)
        cols = lax.broadcasted_iota(jnp.int32, s1.shape, 1)
        s1 = jnp.where(cols <= rows + half, s1, NEG)
        update(1, s1, jnp.max(s1, axis=-1, keepdims=True), v_ref[0, key_rows(i), :])
        for st in range(2):
            acc = acc_ref[st]
            o_ref[st * half:(st + 1) * half, :] = (
                acc[:, :HEAD_DIM] / acc[:, HEAD_DIM:]).astype(BF16)

    @pl.when(i == 0)
    def _():
        scores(0, sa_ref, pa_ref)
        finish(sa_ref)

    @pl.when(i > 0)
    def _():
        scores(1, sa_ref, pa_ref)
        accumulate(0, sc_ref, pc_ref)

    def pair(jj, carry):
        c = 2 * jj + 1
        scores(c + 1, sb_ref, pb_ref)
        accumulate(c, sa_ref, pa_ref)
        scores(c + 2, sa_ref, pa_ref)
        accumulate(c + 1, sb_ref, pb_ref)
        return carry

    lax.fori_loop(0, lax.div(jnp.maximum(i - 1, 0), 2), pair, 0)

    @pl.when(lax.rem(i, 2) == 1)
    def _():
        finish(sa_ref)

    @pl.when(jnp.logical_and(lax.rem(i, 2) == 0, i > 0))
    def _():
        scores(i, sb_ref, pb_ref)
        accumulate(i - 1, sa_ref, pa_ref)
        finish(sb_ref)


def _flash(q, k, v, *, tq=1024):
    _, s, _ = q.shape
    score_buf = pltpu.VMEM((2, tq // 2, tq), F32)
    pmax_buf = pltpu.VMEM((2, tq // 2, LANES), F32)
    return pl.pallas_call(
        functools.partial(_flash_kernel, tq=tq),
        grid=(N_HEADS, s // tq),
        in_specs=[
            pl.BlockSpec((1, s, MLA_QK_PAD), lambda h, i: (h, 0, 0)),
            pl.BlockSpec((1, s, MLA_QK_PAD), lambda h, i: (h, 0, 0)),
            pl.BlockSpec((1, s, 2 * HEAD_DIM), lambda h, i: (h, 0, 0)),
        ],
        out_specs=pl.BlockSpec((tq, HEAD_DIM), lambda h, i: (i, h)),
        out_shape=jax.ShapeDtypeStruct((s, N_HEADS * HEAD_DIM), BF16),
        scratch_shapes=[
            pltpu.VMEM((2, tq // 2, 1), F32),
            pltpu.VMEM((2, tq // 2, 2 * HEAD_DIM), F32),
            score_buf, score_buf, score_buf, pmax_buf, pmax_buf, pmax_buf,
        ],
        compiler_params=_params("arbitrary", "arbitrary"),
        name="mla_flash",
    )(q, k, v)


def _dil_kernel(q_ref, kc_ref, kp_ref, vc_ref, vp_ref, o_ref, o_sc, l_sc):
    t = q_ref.shape[1]
    blk = DIL_SPAN
    n = pl.program_id(0)

    qi = lax.broadcasted_iota(jnp.int32, (blk, 2 * blk), 0)
    kj = lax.broadcasted_iota(jnp.int32, (blk, 2 * blk), 1)
    diff = kj - qi
    band = (diff >= 0) & (diff <= DIL_SPAN)
    band_first = band & (kj >= jnp.where(n == 0, blk, 0))

    ones = jnp.ones((2 * blk, HEAD_DIM), BF16)

    def rows(ref, start, d):
        idx = pl.ds(start, blk) if d == 1 else pl.ds(start, blk, stride=d)
        return ref[0, idx, :].astype(BF16)

    for idx, (window, d) in enumerate(DIL_PATTERNS):
        seg = blk * d
        for it in range(t // blk):
            r, b = it % d, it // d
            start = b * seg + r
            q = rows(q_ref, start, d)
            if b == 0:
                kp, vp = rows(kp_ref, t - seg + r, d), rows(vp_ref, t - seg + r, d)
            else:
                kp, vp = rows(kc_ref, start - seg, d), rows(vc_ref, start - seg, d)
            k = jnp.concatenate([kp, rows(kc_ref, start, d)], axis=0)
            v = jnp.concatenate([vp, rows(vc_ref, start, d)], axis=0)
            v = jnp.concatenate([v, ones], axis=1)
            s = lax.dot_general(q, k, (((1,), (1,)), ((), ())), preferred_element_type=F32)
            s = jnp.where(band_first if b == 0 else band, s, NEG)
            m = jnp.max(s, axis=-1, keepdims=True)
            pr = jnp.exp2(s - m)
            ol = jnp.dot(pr.astype(BF16), v, preferred_element_type=F32)
            l = ol[:, HEAD_DIM:]
            out_rows = pl.ds(start, blk) if d == 1 else pl.ds(start, blk, stride=d)
            o_sc[idx, out_rows, :] = ol[:, :HEAD_DIM] / l
            l_sc[idx, out_rows, :] = m + jnp.log2(l)

    lses = [l_sc[p] for p in range(len(DIL_PATTERNS))]
    mx = functools.reduce(jnp.maximum, lses)
    ws = [jnp.exp2(l - mx) for l in lses]
    num = sum(w * o_sc[p] for p, w in enumerate(ws))
    o_ref[...] = (num / sum(ws)).astype(BF16)


def _dilated(qkv):
    _, s, width = qkv.shape
    t = DIL_TILE
    npat = len(DIL_PATTERNS)

    def cur(which):
        return pl.BlockSpec((1, t, HEAD_DIM), lambda n, h: (which, n, h))

    def prv(which):
        return pl.BlockSpec((1, t, HEAD_DIM), lambda n, h: (which, jnp.maximum(n - 1, 0), h))

    return pl.pallas_call(
        _dil_kernel,
        grid=(s // t, N_HEADS),
        in_specs=[cur(0), cur(1), prv(1), cur(2), prv(2)],
        out_specs=pl.BlockSpec((t, HEAD_DIM), lambda n, h: (n, h)),
        out_shape=jax.ShapeDtypeStruct((s, width), BF16),
        scratch_shapes=[
            pltpu.VMEM((npat, t, HEAD_DIM), F32),
            pltpu.VMEM((npat, t, LANES), F32),
        ],
        compiler_params=_params("parallel", "parallel"),
        name="dilated",
    )(qkv, qkv, qkv, qkv, qkv)


def _oproj_kernel(x_ref, oa_ref, ob_ref, w_ref, g_ref, o_ref):
    half = oa_ref.shape[1]
    y = jnp.dot(oa_ref[...], w_ref[0, :half], preferred_element_type=F32)
    y = y + jnp.dot(ob_ref[...], w_ref[0, half:], preferred_element_type=F32)
    o_ref[...] = x_ref[...] + _rms(y, g_ref[0])


def _oproj(x, oa, ob, wo, g, layer, *, tm=512):
    s, d = x.shape
    width = oa.shape[1]
    return pl.pallas_call(
        _oproj_kernel,
        grid=(s // tm,),
        in_specs=[
            pl.BlockSpec((tm, d), lambda m: (m, 0)),
            pl.BlockSpec((tm, width), lambda m: (m, 0)),
            pl.BlockSpec((tm, width), lambda m: (m, 0)),
            pl.BlockSpec((1,) + wo.shape[1:], lambda m: (layer, 0, 0)),
            pl.BlockSpec((1, 1, d), lambda m: (layer, 0, 0)),
        ],
        out_specs=pl.BlockSpec((tm, d), lambda m: (m, 0)),
        out_shape=jax.ShapeDtypeStruct((s, d), F32),
        compiler_params=_params("parallel"),
        name="oproj",
    )(x, oa, ob, wo, g)


def _cast_kernel(w_ref, o_ref):
    o_ref[...] = w_ref[...].astype(BF16)


def _cast_layer(w, layer, *, steps=8):
    _, r, c = w.shape
    rows = r // steps
    return pl.pallas_call(
        _cast_kernel,
        grid=(steps,),
        in_specs=[pl.BlockSpec((1, rows, c), lambda i: (layer, i, 0))],
        out_specs=pl.BlockSpec((1, rows, c), lambda i: (0, i, 0)),
        out_shape=jax.ShapeDtypeStruct((1, r, c), BF16),
        compiler_params=_params("parallel"),
        name="cast_layer",
    )(w)


def _rope_angles(pos, dim):
    inv = ROPE_THETA ** (-jnp.arange(0, dim, 2, dtype=F32) / dim)
    ang = pos.astype(F32)[:, None] * inv
    return jnp.cos(ang), jnp.sin(ang)


def _rope_tables(cos, sin):
    n, half = cos.shape
    dim = 2 * half
    c = jnp.concatenate([cos, cos, jnp.ones((n, LANES - dim), F32)], axis=1)
    a = jnp.concatenate([-sin, jnp.zeros((n, LANES - half), F32)], axis=1)
    b = jnp.concatenate([jnp.zeros((n, half), F32), sin, jnp.zeros((n, LANES - dim), F32)], axis=1)
    return c, a, b


def kernel(x, positions, ffn1_pre_g, ffn1_post_g, ffn1_w_gate, ffn1_w_up, ffn1_w_down,
           mix_pre_g, mix_post_g, w_in, mla_q_norm_g, mla_w_uq, mla_kv_norm_g, mla_w_ukv,
           w_o, ffn2_pre_g, ffn2_post_g, ffn2_w_gate, ffn2_w_up, ffn2_w_down):
    batch, _, _ = x.shape
    depth = w_in.shape[0]
    bf = lambda w: w.astype(BF16)

    w_in = bf(w_in)
    wc = wkr = wd = w_in
    wuq = bf(mla_w_uq)
    wukv = bf(mla_w_ukv)
    wo = bf(w_o)
    ffn_f32 = ((ffn1_w_gate, ffn1_w_up, ffn1_w_down), (ffn2_w_gate, ffn2_w_up, ffn2_w_down))
    ffn_gains = ((ffn1_pre_g, ffn1_post_g), (ffn2_pre_g, ffn2_post_g))
    ffn_bf16 = {(0, 0): tuple(_cast_layer(w, 0) for w in ffn_f32[0])}

    def ffn(xs, l, which):
        nxt = (l, 1) if which == 0 else (l + 1, 0)
        cast_next = None
        if nxt[0] < depth and nxt not in ffn_bf16:
            cast_next = ffn_f32[nxt[1]] + (nxt[0],)
        gpre, gpost = (g[l][None, None, :] for g in ffn_gains[which])
        xs, converted = _ffn(xs, gpre, gpost, ffn_bf16[(l, which)], 0, cast_next)
        if cast_next is not None:
            ffn_bf16[nxt] = converted
        return xs
    row = lambda g: g[:, None, :]
    mix_pre_g, mix_post_g, mla_q_norm_g, mla_kv_norm_g = map(
        row, (mix_pre_g, mix_post_g, mla_q_norm_g, mla_kv_norm_g))

    outs = []
    for bi in range(batch):
        xs = x[bi]
        assert MLA_ROPE == 2 * PART_ROPE
        cos_a, sin_a = _rope_angles(positions[bi], MLA_ROPE)
        tabs_a = _rope_tables(cos_a, sin_a)
        cp, ap, bp = _rope_tables(cos_a[:, ::2], sin_a[:, ::2])
        qscale = math.log2(math.e) / math.sqrt(HEAD_DIM)
        tabs_p = (jnp.stack([cp * qscale, cp, jnp.ones_like(cp)]),
                  jnp.stack([ap * qscale, ap, jnp.zeros_like(ap)]),
                  jnp.stack([bp * qscale, bp, jnp.zeros_like(bp)]))
        for l in range(depth):
            xs = ffn(xs, l, 0)
            qkv = _dproj(xs, mix_pre_g, wd, tabs_p, l)
            qm, km, vm = _mproj(xs, mix_pre_g, wc, wkr, mla_q_norm_g, wuq,
                                mla_kv_norm_g, wukv, tabs_a, l)
            oa = _flash(qm, km, vm)
            ob = _dilated(qkv)
            xs = _oproj(xs, oa, ob, wo, mix_post_g, l)
            xs = ffn(xs, l, 1)
        outs.append(xs)
    return jnp.stack(outs, axis=0)
```

```python
import functools
import math

import jax
import jax.numpy as jnp
from jax import lax
from jax.experimental import pallas as pl
from jax.experimental.pallas import tpu as pltpu

F32 = jnp.float32
BF16 = jnp.bfloat16

HEAD_DIM = 128
N_HEADS = 8
MLA_RANK = 512
MLA_NOPE = 128
MLA_ROPE = 64
MLA_QK = MLA_NOPE + MLA_ROPE
MLA_QK_PAD = 256
PART_ROPE = 32
DIL_PATTERNS = ((128, 1), (512, 4), (2048, 16))
DIL_SPAN = 128
DIL_TILE = 2048
ROPE_THETA = 500000.0
RMS_EPS = 1e-6
NEG = -1e30
LANES = 128
VMEM_LIMIT = 58 * 1024 * 1024
FFN_NORM_ROWS = 256
FFN_DOT_ROWS = 512
DPROJ_DOT_ROWS = 128


def _rms(xf, g):
    ms = jnp.mean(xf * xf, axis=-1, keepdims=True)
    return xf * lax.rsqrt(ms + RMS_EPS) * g


def _rope(y, c, a, b, half):
    return y * c + pltpu.roll(y, LANES - half, axis=1) * a + pltpu.roll(y, half, axis=1) * b


def _params(*sem):
    return pltpu.CompilerParams(dimension_semantics=sem, vmem_limit_bytes=VMEM_LIMIT)


def _ffn_kernel(*refs, cast_next):
    x_ref, gpre_ref, gpost_ref, wg_ref, wu_ref, wd_ref = refs[:6]
    if cast_next:
        o_ref, nb_ref, r_ref = refs[9], refs[13], refs[14]
        for src_ref, dst_ref in zip(refs[6:9], refs[10:13]):
            dst_ref[...] = src_ref[...].astype(BF16)
    else:
        o_ref, nb_ref, r_ref = refs[6:]
    j = pl.program_id(1)

    tm = x_ref.shape[0]

    def row_chunks(size):
        return [slice(r, r + size) for r in range(0, tm, size)]

    last = pl.num_programs(1) - 1

    def chunk_sum(first):
        for rows in row_chunks(FFN_DOT_ROWS):
            nb = nb_ref[rows]
            g = jnp.dot(nb, wg_ref[0], preferred_element_type=F32)
            u = jnp.dot(nb, wu_ref[0], preferred_element_type=F32)
            h = (g * jax.nn.sigmoid(g) * u).astype(BF16)
            part = jnp.dot(h, wd_ref[0], preferred_element_type=F32)
            if first:
                o_ref[rows] = part
            else:
                o_ref[rows] += part

    @pl.when(j == 0)
    def _():
        for rows in row_chunks(FFN_NORM_ROWS):
            nb_ref[rows] = _rms(x_ref[rows], gpre_ref[0]).astype(BF16)
        chunk_sum(True)

    @pl.when(jnp.logical_and(j > 0, j < last))
    def _():
        chunk_sum(False)

    @pl.when(j == last)
    def _():
        chunk_sum(False)
        for rows in row_chunks(FFN_NORM_ROWS):
            y = o_ref[rows]
            r_ref[rows] = 0.5 * lax.rsqrt(jnp.mean(y * y, axis=-1, keepdims=True) + RMS_EPS)
        for rows in row_chunks(FFN_NORM_ROWS):
            o_ref[rows] = x_ref[rows] + o_ref[rows] * r_ref[rows] * gpost_ref[0]


def _ffn(x, gpre, gpost, weights, layer, cast_next=None, *, tm=1024, tn=512):
    s, d = x.shape
    wg, wu, wd = weights
    dff = wg.shape[-1]
    nm = s // tm
    dm = d // nm
    assert dff // tn >= 2, "first and last D_FF steps must be distinct grid steps"
    in_specs = [
        pl.BlockSpec((tm, d), lambda m, j: (m, 0)),
        pl.BlockSpec((1, 1, d), lambda m, j: (layer, 0, 0)),
        pl.BlockSpec((1, 1, d), lambda m, j: (layer, 0, 0)),
        pl.BlockSpec((1, d, tn), lambda m, j: (layer, 0, j)),
        pl.BlockSpec((1, d, tn), lambda m, j: (layer, 0, j)),
        pl.BlockSpec((1, tn, d), lambda m, j: (layer, j, 0)),
    ]
    out_specs = [pl.BlockSpec((tm, d), lambda m, j: (m, 0))]
    out_shape = [jax.ShapeDtypeStruct((s, d), F32)]
    args = [x, gpre, gpost, wg, wu, wd]
    if cast_next is not None:
        *nxt, nl = cast_next
        in_specs += [
            pl.BlockSpec((1, dm, tn), lambda m, j: (nl, m, j)),
            pl.BlockSpec((1, dm, tn), lambda m, j: (nl, m, j)),
            pl.BlockSpec((1, tn, dm), lambda m, j: (nl, j, m)),
        ]
        out_specs += [
            pl.BlockSpec((1, dm, tn), lambda m, j: (0, m, j)),
            pl.BlockSpec((1, dm, tn), lambda m, j: (0, m, j)),
            pl.BlockSpec((1, tn, dm), lambda m, j: (0, j, m)),
        ]
        out_shape += [jax.ShapeDtypeStruct((1,) + w.shape[1:], BF16) for w in nxt]
        args += nxt
    out = pl.pallas_call(
        functools.partial(_ffn_kernel, cast_next=cast_next is not None),
        grid=(nm, dff // tn),
        in_specs=in_specs,
        out_specs=out_specs,
        out_shape=out_shape,
        scratch_shapes=[pltpu.VMEM((tm, d), BF16), pltpu.VMEM((tm, 1), F32)],
        compiler_params=_params("parallel", "arbitrary"),
        name="ffn",
    )(*args)
    return out[0], tuple(out[1:])


def _dproj_kernel(x_ref, g_ref, w_ref, wx_ref, c_ref, a_ref, b_ref, o_ref, nb_ref):
    j = pl.program_id(1)
    shift = MLA_ROPE
    piece = 2 * HEAD_DIM

    def project():
        w_cat = jnp.concatenate([w_ref[0], wx_ref[0]], axis=1)
        for lo in range(0, N_HEADS * HEAD_DIM, piece):
            wp = w_cat[:, shift + lo:shift + lo + piece]
            for r in range(0, x_ref.shape[0], DPROJ_DOT_ROWS):
                rows = slice(r, r + DPROJ_DOT_ROWS)
                y = jnp.dot(nb_ref[rows], wp, preferred_element_type=F32)
                c, a, b = c_ref[0, rows], a_ref[0, rows], b_ref[0, rows]
                for h in range(piece // HEAD_DIM):
                    sl = slice(h * HEAD_DIM, (h + 1) * HEAD_DIM)
                    osl = slice(lo + h * HEAD_DIM, lo + (h + 1) * HEAD_DIM)
                    o_ref[0, rows, osl] = _rope(y[:, sl], c, a, b, PART_ROPE // 2)

    @pl.when(j == 0)
    def _():
        for r in range(0, x_ref.shape[0], FFN_NORM_ROWS):
            rows = slice(r, r + FFN_NORM_ROWS)
            nb_ref[rows] = _rms(x_ref[rows], g_ref[0]).astype(BF16)
        project()

    @pl.when(j > 0)
    def _():
        project()


def _dproj(x, g, w_in, tabs, layer, *, tm=1024):
    s, d = x.shape
    width = N_HEADS * HEAD_DIM
    first = 2 * MLA_RANK // width
    tab_spec = pl.BlockSpec((1, tm, LANES), lambda m, j: (j, m, 0))
    return pl.pallas_call(
        _dproj_kernel,
        grid=(s // tm, 3),
        in_specs=[
            pl.BlockSpec((tm, d), lambda m, j: (m, 0)),
            pl.BlockSpec((1, 1, d), lambda m, j: (layer, 0, 0)),
            pl.BlockSpec((1, d, width), lambda m, j: (layer, 0, first + j)),
            pl.BlockSpec((1, d, LANES), lambda m, j: (layer, 0, (first + j + 1) * (width // LANES))),
            tab_spec, tab_spec, tab_spec,
        ],
        out_specs=pl.BlockSpec((1, tm, width), lambda m, j: (j, m, 0)),
        out_shape=jax.ShapeDtypeStruct((3, s, width), F32),
        scratch_shapes=[pltpu.VMEM((tm, d), BF16)],
        compiler_params=_params("parallel", "arbitrary"),
        name="dproj",
    )(x, g, w_in, w_in, *tabs)


def _mproj_kernel(x_ref, g_ref, wc_ref, wkr_ref, gq_ref, wuq_ref, gkv_ref, wukv_ref,
                  c_ref, a_ref, b_ref, q_ref, k_ref, v_ref):
    nb = _rms(x_ref[...], g_ref[0]).astype(BF16)
    lat = jnp.dot(nb, wc_ref[0], preferred_element_type=F32)
    cq = _rms(lat[:, :MLA_RANK], gq_ref[0]).astype(BF16)
    ckv = _rms(lat[:, MLA_RANK:], gkv_ref[0]).astype(BF16)
    c, a, b = c_ref[...], a_ref[...], b_ref[...]
    half = MLA_ROPE // 2
    kr = jnp.dot(nb, wkr_ref[0], preferred_element_type=F32)
    lane = lax.broadcasted_iota(jnp.int32, kr.shape, 1)
    kr = jnp.where(lane < MLA_ROPE, _rope(kr, c, a, b, half), 0.0).astype(BF16)
    q = jnp.dot(cq, wuq_ref[0], preferred_element_type=F32)
    kv = jnp.dot(ckv, wukv_ref[0], preferred_element_type=F32)
    scale = math.log2(math.e) / math.sqrt(MLA_QK)
    ones = jnp.ones((x_ref.shape[0], HEAD_DIM), BF16)
    qpad = jnp.zeros((x_ref.shape[0], MLA_QK_PAD - MLA_QK), F32)
    for h in range(N_HEADS):
        lo = h * MLA_QK_PAD
        qlo = h * MLA_QK
        q_ref[h, :, :MLA_NOPE] = (q[:, qlo:qlo + MLA_NOPE] * scale).astype(BF16)
        qr = jnp.concatenate([q[:, qlo + MLA_NOPE:qlo + MLA_QK], qpad], axis=1)
        qr = _rope(qr, c, a, b, half)
        q_ref[h, :, MLA_NOPE:] = (qr * scale).astype(BF16)
        k_ref[h, :, :MLA_NOPE] = kv[:, lo:lo + MLA_NOPE].astype(BF16)
        k_ref[h, :, MLA_NOPE:] = kr
        v_ref[h, :, :HEAD_DIM] = kv[:, lo + MLA_NOPE:lo + MLA_QK_PAD].astype(BF16)
        v_ref[h, :, HEAD_DIM:] = ones


def _mproj(x, g, wc, wkr, gq, wuq, gkv, wukv, tabs, layer, *, tm=512):
    s, d = x.shape
    tab_spec = pl.BlockSpec((tm, LANES), lambda m: (m, 0))

    def wspec(w):
        return pl.BlockSpec((1,) + w.shape[1:], lambda m: (layer, 0, 0))

    def gspec(gv):
        return pl.BlockSpec((1, 1, gv.shape[2]), lambda m: (layer, 0, 0))

    return pl.pallas_call(
        _mproj_kernel,
        grid=(s // tm,),
        in_specs=[
            pl.BlockSpec((tm, d), lambda m: (m, 0)),
            gspec(g),
            pl.BlockSpec((1, d, 2 * MLA_RANK), lambda m: (layer, 0, 0)),
            pl.BlockSpec((1, d, LANES), lambda m: (layer, 0, 2 * MLA_RANK // LANES)),
            gspec(gq), wspec(wuq), gspec(gkv), wspec(wukv),
            tab_spec, tab_spec, tab_spec,
        ],
        out_specs=[
            pl.BlockSpec((N_HEADS, tm, MLA_QK_PAD), lambda m: (0, m, 0)),
            pl.BlockSpec((N_HEADS, tm, MLA_QK_PAD), lambda m: (0, m, 0)),
            pl.BlockSpec((N_HEADS, tm, 2 * HEAD_DIM), lambda m: (0, m, 0)),
        ],
        out_shape=[
            jax.ShapeDtypeStruct((N_HEADS, s, MLA_QK_PAD), BF16),
            jax.ShapeDtypeStruct((N_HEADS, s, MLA_QK_PAD), BF16),
            jax.ShapeDtypeStruct((N_HEADS, s, 2 * HEAD_DIM), BF16),
        ],
        compiler_params=_params("parallel"),
        name="mproj",
    )(x, g, wc, wkr, gq, wuq, gkv, wukv, *tabs)


def _flash_kernel(q_ref, k_ref, v_ref, o_ref, m_ref, acc_ref,
                  sa_ref, sb_ref, sc_ref, pa_ref, pb_ref, pc_ref, *, tq):
    half = tq // 2
    i = pl.program_id(1)
    last = pl.num_programs(1) - 1
    m_ref[...] = jnp.full_like(m_ref, NEG)
    acc_ref[...] = jnp.zeros_like(acc_ref)

    def key_rows(c, n=tq):
        return pl.ds(pl.multiple_of(c * tq, tq), n)

    def scores(c, s_ref, p_ref, tile=i):
        kc = k_ref[0, key_rows(c), :]
        for st in range(2):
            q = q_ref[0, pl.ds(pl.multiple_of(tile * tq + st * half, half), half), :]
            s = lax.dot_general(q, kc, (((1,), (1,)), ((), ())), preferred_element_type=F32)
            s_ref[st] = s
            pm = s[:, :LANES]
            for t in range(1, tq // LANES):
                pm = jnp.maximum(pm, s[:, t * LANES:(t + 1) * LANES])
            p_ref[st] = pm

    def update(st, s, row_max, vc):
        m_prev = m_ref[st]
        m_new = jnp.maximum(m_prev, row_max)
        alpha = jnp.exp2(m_prev - m_new)
        pr = jnp.exp2(s - m_new).astype(BF16)
        acc_ref[st] = alpha * acc_ref[st] + jnp.dot(pr, vc, preferred_element_type=F32)
        m_ref[st] = m_new

    def accumulate(c, s_ref, p_ref):
        vc = v_ref[0, key_rows(c), :]
        for st in range(2):
            update(st, s_ref[st], jnp.max(p_ref[st], axis=-1, keepdims=True), vc)

    def finish(s_ref):
        scores(0, sc_ref, pc_ref, tile=jnp.minimum(i + 1, last))
        s0 = s_ref[0, :, :half]
        rows = lax.broadcasted_iota(jnp.int32, s0.shape, 0)
        cols = lax.broadcasted_iota(jnp.int32, s0.shape, 1)
        s0 = jnp.where(cols <= rows, s0, NEG)
        update(0, s0, jnp.max(s0, axis=-1, keepdims=True), v_ref[0, key_rows(i, half), :])
        s1 = s_ref[1]
        rows = lax.broadcasted_iota(jnp.int32, s1.shape, 0)
        cols = lax.broadcasted_iota(jnp.int32, s1.shape, 1)
        s1 = jnp.where(cols <= rows + half, s1, NEG)
        update(1, s1, jnp.max(s1, axis=-1, keepdims=True), v_ref[0, key_rows(i), :])
        for st in range(2):
            acc = acc_ref[st]
            o_ref[st * half:(st + 1) * half, :] = (
                acc[:, :HEAD_DIM] / acc[:, HEAD_DIM:]).astype(BF16)

    @pl.when(i == 0)
    def _():
        scores(0, sa_ref, pa_ref)
        finish(sa_ref)

    @pl.when(i > 0)
    def _():
        scores(1, sa_ref, pa_ref)
        accumulate(0, sc_ref, pc_ref)

    def pair(jj, carry):
        c = 2 * jj + 1
        scores(c + 1, sb_ref, pb_ref)
        accumulate(c, sa_ref, pa_ref)
        scores(c + 2, sa_ref, pa_ref)
        accumulate(c + 1, sb_ref, pb_ref)
        return carry

    lax.fori_loop(0, lax.div(jnp.maximum(i - 1, 0), 2), pair, 0)

    @pl.when(lax.rem(i, 2) == 1)
    def _():
        finish(sa_ref)

    @pl.when(jnp.logical_and(lax.rem(i, 2) == 0, i > 0))
    def _():
        scores(i, sb_ref, pb_ref)
        accumulate(i - 1, sa_ref, pa_ref)
        finish(sb_ref)


def _flash(q, k, v, *, tq=1024):
    _, s, _ = q.shape
    score_buf = pltpu.VMEM((2, tq // 2, tq), F32)
    pmax_buf = pltpu.VMEM((2, tq // 2, LANES), F32)
    return pl.pallas_call(
        functools.partial(_flash_kernel, tq=tq),
        grid=(N_HEADS, s // tq),
        in_specs=[
            pl.BlockSpec((1, s, MLA_QK_PAD), lambda h, i: (h, 0, 0)),
            pl.BlockSpec((1, s, MLA_QK_PAD), lambda h, i: (h, 0, 0)),
            pl.BlockSpec((1, s, 2 * HEAD_DIM), lambda h, i: (h, 0, 0)),
        ],
        out_specs=pl.BlockSpec((tq, HEAD_DIM), lambda h, i: (i, h)),
        out_shape=jax.ShapeDtypeStruct((s, N_HEADS * HEAD_DIM), BF16),
        scratch_shapes=[
            pltpu.VMEM((2, tq // 2, 1), F32),
            pltpu.VMEM((2, tq // 2, 2 * HEAD_DIM), F32),
            score_buf, score_buf, score_buf, pmax_buf, pmax_buf, pmax_buf,
        ],
        compiler_params=_params("arbitrary", "arbitrary"),
        name="mla_flash",
    )(q, k, v)


def _dil_kernel(q_ref, kc_ref, kp_ref, vc_ref, vp_ref, o_ref, o_sc, l_sc):
    t = q_ref.shape[1]
    blk = DIL_SPAN
    n = pl.program_id(0)

    qi = lax.broadcasted_iota(jnp.int32, (blk, 2 * blk), 0)
    kj = lax.broadcasted_iota(jnp.int32, (blk, 2 * blk), 1)
    diff = kj - qi
    band = (diff >= 0) & (diff <= DIL_SPAN)
    band_first = band & (kj >= jnp.where(n == 0, blk, 0))

    ones = jnp.ones((2 * blk, HEAD_DIM), BF16)

    def rows(ref, start, d):
        idx = pl.ds(start, blk) if d == 1 else pl.ds(start, blk, stride=d)
        return ref[0, idx, :].astype(BF16)

    for idx, (window, d) in enumerate(DIL_PATTERNS):
        seg = blk * d
        for it in range(t // blk):
            r, b = it % d, it // d
            start = b * seg + r
            q = rows(q_ref, start, d)
            if b == 0:
                kp, vp = rows(kp_ref, t - seg + r, d), rows(vp_ref, t - seg + r, d)
            else:
                kp, vp = rows(kc_ref, start - seg, d), rows(vc_ref, start - seg, d)
            k = jnp.concatenate([kp, rows(kc_ref, start, d)], axis=0)
            v = jnp.concatenate([vp, rows(vc_ref, start, d)], axis=0)
            v = jnp.concatenate([v, ones], axis=1)
            s = lax.dot_general(q, k, (((1,), (1,)), ((), ())), preferred_element_type=F32)
            s = jnp.where(band_first if b == 0 else band, s, NEG)
            m = jnp.max(s, axis=-1, keepdims=True)
            pr = jnp.exp2(s - m)
            ol = jnp.dot(pr.astype(BF16), v, preferred_element_type=F32)
            l = ol[:, HEAD_DIM:]
            out_rows = pl.ds(start, blk) if d == 1 else pl.ds(start, blk, stride=d)
            o_sc[idx, out_rows, :] = ol[:, :HEAD_DIM] / l
            l_sc[idx, out_rows, :] = m + jnp.log2(l)

    lses = [l_sc[p] for p in range(len(DIL_PATTERNS))]
    mx = functools.reduce(jnp.maximum, lses)
    ws = [jnp.exp2(l - mx) for l in lses]
    num = sum(w * o_sc[p] for p, w in enumerate(ws))
    o_ref[...] = (num / sum(ws)).astype(BF16)


def _dilated(qkv):
    _, s, width = qkv.shape
    t = DIL_TILE
    npat = len(DIL_PATTERNS)

    def cur(which):
        return pl.BlockSpec((1, t, HEAD_DIM), lambda n, h: (which, n, h))

    def prv(which):
        return pl.BlockSpec((1, t, HEAD_DIM), lambda n, h: (which, jnp.maximum(n - 1, 0), h))

    return pl.pallas_call(
        _dil_kernel,
        grid=(s // t, N_HEADS),
        in_specs=[cur(0), cur(1), prv(1), cur(2), prv(2)],
        out_specs=pl.BlockSpec((t, HEAD_DIM), lambda n, h: (n, h)),
        out_shape=jax.ShapeDtypeStruct((s, width), BF16),
        scratch_shapes=[
            pltpu.VMEM((npat, t, HEAD_DIM), F32),
            pltpu.VMEM((npat, t, LANES), F32),
        ],
        compiler_params=_params("parallel", "parallel"),
        name="dilated",
    )(qkv, qkv, qkv, qkv, qkv)


def _oproj_kernel(x_ref, oa_ref, ob_ref, w_ref, g_ref, o_ref):
    half = oa_ref.shape[1]
    y = jnp.dot(oa_ref[...], w_ref[0, :half], preferred_element_type=F32)
    y = y + jnp.dot(ob_ref[...], w_ref[0, half:], preferred_element_type=F32)
    o_ref[...] = x_ref[...] + _rms(y, g_ref[0])


def _oproj(x, oa, ob, wo, g, layer, *, tm=512):
    s, d = x.shape
    width = oa.shape[1]
    return pl.pallas_call(
        _oproj_kernel,
        grid=(s // tm,),
        in_specs=[
            pl.BlockSpec((tm, d), lambda m: (m, 0)),
            pl.BlockSpec((tm, width), lambda m: (m, 0)),
            pl.BlockSpec((tm, width), lambda m: (m, 0)),
            pl.BlockSpec((1,) + wo.shape[1:], lambda m: (layer, 0, 0)),
            pl.BlockSpec((1, 1, d), lambda m: (layer, 0, 0)),
        ],
        out_specs=pl.BlockSpec((tm, d), lambda m: (m, 0)),
        out_shape=jax.ShapeDtypeStruct((s, d), F32),
        compiler_params=_params("parallel"),
        name="oproj",
    )(x, oa, ob, wo, g)


def _cast_kernel(w_ref, o_ref):
    o_ref[...] = w_ref[...].astype(BF16)


def _cast_layer(w, layer, *, steps=8):
    _, r, c = w.shape
    rows = r // steps
    return pl.pallas_call(
        _cast_kernel,
        grid=(steps,),
        in_specs=[pl.BlockSpec((1, rows, c), lambda i: (layer, i, 0))],
        out_specs=pl.BlockSpec((1, rows, c), lambda i: (0, i, 0)),
        out_shape=jax.ShapeDtypeStruct((1, r, c), BF16),
        compiler_params=_params("parallel"),
        name="cast_layer",
    )(w)


def _rope_angles(pos, dim):
    inv = ROPE_THETA ** (-jnp.arange(0, dim, 2, dtype=F32) / dim)
    ang = pos.astype(F32)[:, None] * inv
    return jnp.cos(ang), jnp.sin(ang)


def _rope_tables(cos, sin):
    n, half = cos.shape
    dim = 2 * half
    c = jnp.concatenate([cos, cos, jnp.ones((n, LANES - dim), F32)], axis=1)
    a = jnp.concatenate([-sin, jnp.zeros((n, LANES - half), F32)], axis=1)
    b = jnp.concatenate([jnp.zeros((n, half), F32), sin, jnp.zeros((n, LANES - dim), F32)], axis=1)
    return c, a, b


def kernel(x, positions, ffn1_pre_g, ffn1_post_g, ffn1_w_gate, ffn1_w_up, ffn1_w_down,
           mix_pre_g, mix_post_g, w_in, mla_q_norm_g, mla_w_uq, mla_kv_norm_g, mla_w_ukv,
           w_o, ffn2_pre_g, ffn2_post_g, ffn2_w_gate, ffn2_w_up, ffn2_w_down):
    batch, _, _ = x.shape
    depth = w_in.shape[0]
    bf = lambda w: w.astype(BF16)

    w_in = bf(w_in)
    wc = wkr = wd = w_in
    wuq = bf(mla_w_uq)
    wukv = bf(mla_w_ukv)
    wo = bf(w_o)
    ffn_f32 = ((ffn1_w_gate, ffn1_w_up, ffn1_w_down), (ffn2_w_gate, ffn2_w_up, ffn2_w_down))
    ffn_gains = ((ffn1_pre_g, ffn1_post_g), (ffn2_pre_g, ffn2_post_g))
    ffn_bf16 = {(0, 0): tuple(_cast_layer(w, 0) for w in ffn_f32[0])}

    def ffn(xs, l, which):
        nxt = (l, 1) if which == 0 else (l + 1, 0)
        cast_next = None
        if nxt[0] < depth and nxt not in ffn_bf16:
            cast_next = ffn_f32[nxt[1]] + (nxt[0],)
        gpre, gpost = (g[l][None, None, :] for g in ffn_gains[which])
        xs, converted = _ffn(xs, gpre, gpost, ffn_bf16[(l, which)], 0, cast_next)
        if cast_next is not None:
            ffn_bf16[nxt] = converted
        return xs
    row = lambda g: g[:, None, :]
    mix_pre_g, mix_post_g, mla_q_norm_g, mla_kv_norm_g = map(
        row, (mix_pre_g, mix_post_g, mla_q_norm_g, mla_kv_norm_g))

    outs = []
    for bi in range(batch):
        xs = x[bi]
        tabs_a = _rope_tables(*_rope_angles(positions[bi], MLA_ROPE))
        cp, ap, bp = _rope_tables(*_rope_angles(positions[bi], PART_ROPE))
        qscale = math.log2(math.e) / math.sqrt(HEAD_DIM)
        tabs_p = (jnp.stack([cp * qscale, cp, jnp.ones_like(cp)]),
                  jnp.stack([ap * qscale, ap, jnp.zeros_like(ap)]),
                  jnp.stack([bp * qscale, bp, jnp.zeros_like(bp)]))
        for l in range(depth):
            xs = ffn(xs, l, 0)
            qkv = _dproj(xs, mix_pre_g, wd, tabs_p, l)
            qm, km, vm = _mproj(xs, mix_pre_g, wc, wkr, mla_q_norm_g, wuq,
                                mla_kv_norm_g, wukv, tabs_a, l)
            oa = _flash(qm, km, vm)
            ob = _dilated(qkv)
            xs = _oproj(xs, oa, ob, wo, mix_post_g, l)
            xs = ffn(xs, l, 1)
        outs.append(xs)
    return jnp.stack(outs, axis=0)
```

```python
import functools
import math

import jax
import jax.numpy as jnp
from jax import lax
from jax.experimental import pallas as pl
from jax.experimental.pallas import tpu as pltpu

F32 = jnp.float32
BF16 = jnp.bfloat16

HEAD_DIM = 128
N_HEADS = 8
MLA_RANK = 512
MLA_NOPE = 128
MLA_ROPE = 64
MLA_QK = MLA_NOPE + MLA_ROPE
MLA_QK_PAD = 256
PART_ROPE = 32
DIL_PATTERNS = ((128, 1), (512, 4), (2048, 16))
DIL_SPAN = 128
DIL_TILE = 2048
ROPE_THETA = 500000.0
RMS_EPS = 1e-6
NEG = -1e30
LANES = 128
VMEM_LIMIT = 58 * 1024 * 1024
FFN_NORM_ROWS = 256
FFN_DOT_ROWS = 512
DPROJ_DOT_ROWS = 128
DIL_QSCALE = math.log2(math.e) / math.sqrt(HEAD_DIM)


def _rms(xf, g):
    ms = jnp.mean(xf * xf, axis=-1, keepdims=True)
    return xf * lax.rsqrt(ms + RMS_EPS) * g


def _rope(y, c, a, b, half):
    return y * c + pltpu.roll(y, LANES - half, axis=1) * a + pltpu.roll(y, half, axis=1) * b


def _params(*sem):
    return pltpu.CompilerParams(dimension_semantics=sem, vmem_limit_bytes=VMEM_LIMIT)


def _ffn_kernel(*refs, cast_next):
    x_ref, gpre_ref, gpost_ref, wg_ref, wu_ref, wd_ref = refs[:6]
    if cast_next:
        o_ref, nb_ref, r_ref = refs[9], refs[13], refs[14]
        for src_ref, dst_ref in zip(refs[6:9], refs[10:13]):
            dst_ref[...] = src_ref[...].astype(BF16)
    else:
        o_ref, nb_ref, r_ref = refs[6:]
    j = pl.program_id(1)

    tm = x_ref.shape[0]

    def row_chunks(size):
        return [slice(r, r + size) for r in range(0, tm, size)]

    last = pl.num_programs(1) - 1

    def chunk_sum(first):
        for rows in row_chunks(FFN_DOT_ROWS):
            nb = nb_ref[rows]
            g = jnp.dot(nb, wg_ref[0], preferred_element_type=F32)
            u = jnp.dot(nb, wu_ref[0], preferred_element_type=F32)
            h = (g * jax.nn.sigmoid(g) * u).astype(BF16)
            part = jnp.dot(h, wd_ref[0], preferred_element_type=F32)
            if first:
                o_ref[rows] = part
            else:
                o_ref[rows] += part

    @pl.when(j == 0)
    def _():
        for rows in row_chunks(FFN_NORM_ROWS):
            nb_ref[rows] = _rms(x_ref[rows], gpre_ref[0]).astype(BF16)
        chunk_sum(True)

    @pl.when(jnp.logical_and(j > 0, j < last))
    def _():
        chunk_sum(False)

    @pl.when(j == last)
    def _():
        chunk_sum(False)
        for rows in row_chunks(FFN_NORM_ROWS):
            y = o_ref[rows]
            r_ref[rows] = 0.5 * lax.rsqrt(jnp.mean(y * y, axis=-1, keepdims=True) + RMS_EPS)
        for rows in row_chunks(FFN_NORM_ROWS):
            o_ref[rows] = x_ref[rows] + o_ref[rows] * r_ref[rows] * gpost_ref[0]


def _ffn(x, gpre, gpost, weights, layer, cast_next=None, *, tm=1024, tn=512):
    s, d = x.shape
    wg, wu, wd = weights
    dff = wg.shape[-1]
    nm = s // tm
    dm = d // nm
    assert dff // tn >= 2, "first and last D_FF steps must be distinct grid steps"
    in_specs = [
        pl.BlockSpec((tm, d), lambda m, j: (m, 0)),
        pl.BlockSpec((1, 1, d), lambda m, j: (layer, 0, 0)),
        pl.BlockSpec((1, 1, d), lambda m, j: (layer, 0, 0)),
        pl.BlockSpec((1, d, tn), lambda m, j: (layer, 0, j)),
        pl.BlockSpec((1, d, tn), lambda m, j: (layer, 0, j)),
        pl.BlockSpec((1, tn, d), lambda m, j: (layer, j, 0)),
    ]
    out_specs = [pl.BlockSpec((tm, d), lambda m, j: (m, 0))]
    out_shape = [jax.ShapeDtypeStruct((s, d), F32)]
    args = [x, gpre, gpost, wg, wu, wd]
    if cast_next is not None:
        *nxt, nl = cast_next
        in_specs += [
            pl.BlockSpec((1, dm, tn), lambda m, j: (nl, m, j)),
            pl.BlockSpec((1, dm, tn), lambda m, j: (nl, m, j)),
            pl.BlockSpec((1, tn, dm), lambda m, j: (nl, j, m)),
        ]
        out_specs += [
            pl.BlockSpec((1, dm, tn), lambda m, j: (0, m, j)),
            pl.BlockSpec((1, dm, tn), lambda m, j: (0, m, j)),
            pl.BlockSpec((1, tn, dm), lambda m, j: (0, j, m)),
        ]
        out_shape += [jax.ShapeDtypeStruct((1,) + w.shape[1:], BF16) for w in nxt]
        args += nxt
    out = pl.pallas_call(
        functools.partial(_ffn_kernel, cast_next=cast_next is not None),
        grid=(nm, dff // tn),
        in_specs=in_specs,
        out_specs=out_specs,
        out_shape=out_shape,
        scratch_shapes=[pltpu.VMEM((tm, d), BF16), pltpu.VMEM((tm, 1), F32)],
        compiler_params=_params("parallel", "arbitrary"),
        name="ffn",
    )(*args)
    return out[0], tuple(out[1:])


def _dproj_kernel(x_ref, g_ref, w_ref, wx_ref, c_ref, a_ref, b_ref, o_ref, nb_ref, t_sc):
    j = pl.program_id(1)
    shift = MLA_ROPE
    piece = 2 * HEAD_DIM

    def project():
        is_v = j == 2
        scale = jnp.where(j == 0, DIL_QSCALE, 1.0).astype(F32)
        t_sc[0] = jnp.where(is_v, 1.0, c_ref[...] * scale)
        t_sc[1] = jnp.where(is_v, 0.0, a_ref[...] * scale)
        t_sc[2] = jnp.where(is_v, 0.0, b_ref[...] * scale)
        w_cat = jnp.concatenate([w_ref[0], wx_ref[0]], axis=1)
        for lo in range(0, N_HEADS * HEAD_DIM, piece):
            wp = w_cat[:, shift + lo:shift + lo + piece]
            for r in range(0, x_ref.shape[0], DPROJ_DOT_ROWS):
                rows = slice(r, r + DPROJ_DOT_ROWS)
                y = jnp.dot(nb_ref[rows], wp, preferred_element_type=F32)
                c, a, b = t_sc[0, rows], t_sc[1, rows], t_sc[2, rows]
                for h in range(piece // HEAD_DIM):
                    sl = slice(h * HEAD_DIM, (h + 1) * HEAD_DIM)
                    osl = slice(lo + h * HEAD_DIM, lo + (h + 1) * HEAD_DIM)
                    o_ref[0, rows, osl] = _rope(y[:, sl], c, a, b, PART_ROPE // 2)

    @pl.when(j == 0)
    def _():
        for r in range(0, x_ref.shape[0], FFN_NORM_ROWS):
            rows = slice(r, r + FFN_NORM_ROWS)
            nb_ref[rows] = _rms(x_ref[rows], g_ref[0]).astype(BF16)
        project()

    @pl.when(j > 0)
    def _():
        project()


def _dproj(x, g, w_in, tabs, layer, *, tm=1024):
    s, d = x.shape
    width = N_HEADS * HEAD_DIM
    first = 2 * MLA_RANK // width
    tab_spec = pl.BlockSpec((tm, LANES), lambda m, j: (m, 0))
    return pl.pallas_call(
        _dproj_kernel,
        grid=(s // tm, 3),
        in_specs=[
            pl.BlockSpec((tm, d), lambda m, j: (m, 0)),
            pl.BlockSpec((1, 1, d), lambda m, j: (layer, 0, 0)),
            pl.BlockSpec((1, d, width), lambda m, j: (layer, 0, first + j)),
            pl.BlockSpec((1, d, LANES), lambda m, j: (layer, 0, (first + j + 1) * (width // LANES))),
            tab_spec, tab_spec, tab_spec,
        ],
        out_specs=pl.BlockSpec((1, tm, width), lambda m, j: (j, m, 0)),
        out_shape=jax.ShapeDtypeStruct((3, s, width), F32),
        scratch_shapes=[pltpu.VMEM((tm, d), BF16), pltpu.VMEM((3, tm, LANES), F32)],
        compiler_params=_params("parallel", "arbitrary"),
        name="dproj",
    )(x, g, w_in, w_in, *tabs)


def _mproj_kernel(x_ref, g_ref, wc_ref, wkr_ref, gq_ref, wuq_ref, gkv_ref, wukv_ref,
                  c_ref, a_ref, b_ref, q_ref, k_ref, v_ref):
    nb = _rms(x_ref[...], g_ref[0]).astype(BF16)
    lat = jnp.dot(nb, wc_ref[0], preferred_element_type=F32)
    cq = _rms(lat[:, :MLA_RANK], gq_ref[0]).astype(BF16)
    ckv = _rms(lat[:, MLA_RANK:], gkv_ref[0]).astype(BF16)
    c, a, b = c_ref[...], a_ref[...], b_ref[...]
    half = MLA_ROPE // 2
    kr = jnp.dot(nb, wkr_ref[0], preferred_element_type=F32)
    lane = lax.broadcasted_iota(jnp.int32, kr.shape, 1)
    kr = jnp.where(lane < MLA_ROPE, _rope(kr, c, a, b, half), 0.0).astype(BF16)
    q = jnp.dot(cq, wuq_ref[0], preferred_element_type=F32)
    kv = jnp.dot(ckv, wukv_ref[0], preferred_element_type=F32)
    scale = math.log2(math.e) / math.sqrt(MLA_QK)
    ones = jnp.ones((x_ref.shape[0], HEAD_DIM), BF16)
    qpad = jnp.zeros((x_ref.shape[0], MLA_QK_PAD - MLA_QK), F32)
    for h in range(N_HEADS):
        lo = h * MLA_QK_PAD
        qlo = h * MLA_QK
        q_ref[h, :, :MLA_NOPE] = (q[:, qlo:qlo + MLA_NOPE] * scale).astype(BF16)
        qr = jnp.concatenate([q[:, qlo + MLA_NOPE:qlo + MLA_QK], qpad], axis=1)
        qr = _rope(qr, c, a, b, half)
        q_ref[h, :, MLA_NOPE:] = (qr * scale).astype(BF16)
        k_ref[h, :, :MLA_NOPE] = kv[:, lo:lo + MLA_NOPE].astype(BF16)
        k_ref[h, :, MLA_NOPE:] = kr
        v_ref[h, :, :HEAD_DIM] = kv[:, lo + MLA_NOPE:lo + MLA_QK_PAD].astype(BF16)
        v_ref[h, :, HEAD_DIM:] = ones


def _mproj(x, g, wc, wkr, gq, wuq, gkv, wukv, tabs, layer, *, tm=512):
    s, d = x.shape
    tab_spec = pl.BlockSpec((tm, LANES), lambda m: (m, 0))

    def wspec(w):
        return pl.BlockSpec((1,) + w.shape[1:], lambda m: (layer, 0, 0))

    def gspec(gv):
        return pl.BlockSpec((1, 1, gv.shape[2]), lambda m: (layer, 0, 0))

    return pl.pallas_call(
        _mproj_kernel,
        grid=(s // tm,),
        in_specs=[
            pl.BlockSpec((tm, d), lambda m: (m, 0)),
            gspec(g),
            pl.BlockSpec((1, d, 2 * MLA_RANK), lambda m: (layer, 0, 0)),
            pl.BlockSpec((1, d, LANES), lambda m: (layer, 0, 2 * MLA_RANK // LANES)),
            gspec(gq), wspec(wuq), gspec(gkv), wspec(wukv),
            tab_spec, tab_spec, tab_spec,
        ],
        out_specs=[
            pl.BlockSpec((N_HEADS, tm, MLA_QK_PAD), lambda m: (0, m, 0)),
            pl.BlockSpec((N_HEADS, tm, MLA_QK_PAD), lambda m: (0, m, 0)),
            pl.BlockSpec((N_HEADS, tm, 2 * HEAD_DIM), lambda m: (0, m, 0)),
        ],
        out_shape=[
            jax.ShapeDtypeStruct((N_HEADS, s, MLA_QK_PAD), BF16),
            jax.ShapeDtypeStruct((N_HEADS, s, MLA_QK_PAD), BF16),
            jax.ShapeDtypeStruct((N_HEADS, s, 2 * HEAD_DIM), BF16),
        ],
        compiler_params=_params("parallel"),
        name="mproj",
    )(x, g, wc, wkr, gq, wuq, gkv, wukv, *tabs)


def _flash_kernel(q_ref, k_ref, v_ref, o_ref, m_ref, acc_ref,
                  sa_ref, sb_ref, sc_ref, pa_ref, pb_ref, pc_ref, *, tq):
    half = tq // 2
    i = pl.program_id(1)
    last = pl.num_programs(1) - 1
    m_ref[...] = jnp.full_like(m_ref, NEG)
    acc_ref[...] = jnp.zeros_like(acc_ref)

    def key_rows(c, n=tq):
        return pl.ds(pl.multiple_of(c * tq, tq), n)

    def scores(c, s_ref, p_ref, tile=i):
        kc = k_ref[0, key_rows(c), :]
        for st in range(2):
            q = q_ref[0, pl.ds(pl.multiple_of(tile * tq + st * half, half), half), :]
            s = lax.dot_general(q, kc, (((1,), (1,)), ((), ())), preferred_element_type=F32)
            s_ref[st] = s
            pm = s[:, :LANES]
            for t in range(1, tq // LANES):
                pm = jnp.maximum(pm, s[:, t * LANES:(t + 1) * LANES])
            p_ref[st] = pm

    def update(st, s, row_max, vc):
        m_prev = m_ref[st]
        m_new = jnp.maximum(m_prev, row_max)
        alpha = jnp.exp2(m_prev - m_new)
        pr = jnp.exp2(s - m_new).astype(BF16)
        acc_ref[st] = alpha * acc_ref[st] + jnp.dot(pr, vc, preferred_element_type=F32)
        m_ref[st] = m_new

    def accumulate(c, s_ref, p_ref):
        vc = v_ref[0, key_rows(c), :]
        for st in range(2):
            update(st, s_ref[st], jnp.max(p_ref[st], axis=-1, keepdims=True), vc)

    def finish(s_ref):
        scores(0, sc_ref, pc_ref, tile=jnp.minimum(i + 1, last))
        s0 = s_ref[0, :, :half]
        rows = lax.broadcasted_iota(jnp.int32, s0.shape, 0)
        cols = lax.broadcasted_iota(jnp.int32, s0.shape, 1)
        s0 = jnp.where(cols <= rows, s0, NEG)
        update(0, s0, jnp.max(s0, axis=-1, keepdims=True), v_ref[0, key_rows(i, half), :])
        s1 = s_ref[1]
        rows = lax.broadcasted_iota(jnp.int32, s1.shape, 0)
        cols = lax.broadcasted_iota(jnp.int32, s1.shape, 1)
        s1 = jnp.where(cols <= rows + half, s1, NEG)
        update(1, s1, jnp.max(s1, axis=-1, keepdims=True), v_ref[0, key_rows(i), :])
        for st in range(2):
            acc = acc_ref[st]
            o_ref[st * half:(st + 1) * half, :] = (
                acc[:, :HEAD_DIM] / acc[:, HEAD_DIM:]).astype(BF16)

    @pl.when(i == 0)
    def _():
        scores(0, sa_ref, pa_ref)
        finish(sa_ref)

    @pl.when(i > 0)
    def _():
        scores(1, sa_ref, pa_ref)
        accumulate(0, sc_ref, pc_ref)

    def pair(jj, carry):
        c = 2 * jj + 1
        scores(c + 1, sb_ref, pb_ref)
        accumulate(c, sa_ref, pa_ref)
        scores(c + 2, sa_ref, pa_ref)
        accumulate(c + 1, sb_ref, pb_ref)
        return carry

    lax.fori_loop(0, lax.div(jnp.maximum(i - 1, 0), 2), pair, 0)

    @pl.when(lax.rem(i, 2) == 1)
    def _():
        finish(sa_ref)

    @pl.when(jnp.logical_and(lax.rem(i, 2) == 0, i > 0))
    def _():
        scores(i, sb_ref, pb_ref)
        accumulate(i - 1, sa_ref, pa_ref)
        finish(sb_ref)


def _flash(q, k, v, *, tq=1024):
    _, s, _ = q.shape
    score_buf = pltpu.VMEM((2, tq // 2, tq), F32)
    pmax_buf = pltpu.VMEM((2, tq // 2, LANES), F32)
    return pl.pallas_call(
        functools.partial(_flash_kernel, tq=tq),
        grid=(N_HEADS, s // tq),
        in_specs=[
            pl.BlockSpec((1, s, MLA_QK_PAD), lambda h, i: (h, 0, 0)),
            pl.BlockSpec((1, s, MLA_QK_PAD), lambda h, i: (h, 0, 0)),
            pl.BlockSpec((1, s, 2 * HEAD_DIM), lambda h, i: (h, 0, 0)),
        ],
        out_specs=pl.BlockSpec((tq, HEAD_DIM), lambda h, i: (i, h)),
        out_shape=jax.ShapeDtypeStruct((s, N_HEADS * HEAD_DIM), BF16),
        scratch_shapes=[
            pltpu.VMEM((2, tq // 2, 1), F32),
            pltpu.VMEM((2, tq // 2, 2 * HEAD_DIM), F32),
            score_buf, score_buf, score_buf, pmax_buf, pmax_buf, pmax_buf,
        ],
        compiler_params=_params("arbitrary", "arbitrary"),
        name="mla_flash",
    )(q, k, v)


def _dil_kernel(q_ref, kc_ref, kp_ref, vc_ref, vp_ref, o_ref, o_sc, l_sc):
    t = q_ref.shape[1]
    blk = DIL_SPAN
    n = pl.program_id(0)

    qi = lax.broadcasted_iota(jnp.int32, (blk, 2 * blk), 0)
    kj = lax.broadcasted_iota(jnp.int32, (blk, 2 * blk), 1)
    diff = kj - qi
    band = (diff >= 0) & (diff <= DIL_SPAN)
    band_first = band & (kj >= jnp.where(n == 0, blk, 0))

    ones = jnp.ones((2 * blk, HEAD_DIM), BF16)

    def rows(ref, start, d):
        idx = pl.ds(start, blk) if d == 1 else pl.ds(start, blk, stride=d)
        return ref[0, idx, :].astype(BF16)

    for idx, (window, d) in enumerate(DIL_PATTERNS):
        seg = blk * d
        for it in range(t // blk):
            r, b = it % d, it // d
            start = b * seg + r
            q = rows(q_ref, start, d)
            if b == 0:
                kp, vp = rows(kp_ref, t - seg + r, d), rows(vp_ref, t - seg + r, d)
            else:
                kp, vp = rows(kc_ref, start - seg, d), rows(vc_ref, start - seg, d)
            k = jnp.concatenate([kp, rows(kc_ref, start, d)], axis=0)
            v = jnp.concatenate([vp, rows(vc_ref, start, d)], axis=0)
            v = jnp.concatenate([v, ones], axis=1)
            s = lax.dot_general(q, k, (((1,), (1,)), ((), ())), preferred_element_type=F32)
            s = jnp.where(band_first if b == 0 else band, s, NEG)
            m = jnp.max(s, axis=-1, keepdims=True)
            pr = jnp.exp2(s - m)
            ol = jnp.dot(pr.astype(BF16), v, preferred_element_type=F32)
            l = ol[:, HEAD_DIM:]
            out_rows = pl.ds(start, blk) if d == 1 else pl.ds(start, blk, stride=d)
            o_sc[idx, out_rows, :] = ol[:, :HEAD_DIM] / l
            l_sc[idx, out_rows, :] = m + jnp.log2(l)

    lses = [l_sc[p] for p in range(len(DIL_PATTERNS))]
    mx = functools.reduce(jnp.maximum, lses)
    ws = [jnp.exp2(l - mx) for l in lses]
    num = sum(w * o_sc[p] for p, w in enumerate(ws))
    o_ref[...] = (num / sum(ws)).astype(BF16)


def _dilated(qkv):
    _, s, width = qkv.shape
    t = DIL_TILE
    npat = len(DIL_PATTERNS)

    def cur(which):
        return pl.BlockSpec((1, t, HEAD_DIM), lambda n, h: (which, n, h))

    def prv(which):
        return pl.BlockSpec((1, t, HEAD_DIM), lambda n, h: (which, jnp.maximum(n - 1, 0), h))

    return pl.pallas_call(
        _dil_kernel,
        grid=(s // t, N_HEADS),
        in_specs=[cur(0), cur(1), prv(1), cur(2), prv(2)],
        out_specs=pl.BlockSpec((t, HEAD_DIM), lambda n, h: (n, h)),
        out_shape=jax.ShapeDtypeStruct((s, width), BF16),
        scratch_shapes=[
            pltpu.VMEM((npat, t, HEAD_DIM), F32),
            pltpu.VMEM((npat, t, LANES), F32),
        ],
        compiler_params=_params("parallel", "parallel"),
        name="dilated",
    )(qkv, qkv, qkv, qkv, qkv)


def _oproj_kernel(x_ref, oa_ref, ob_ref, w_ref, g_ref, o_ref):
    half = oa_ref.shape[1]
    y = jnp.dot(oa_ref[...], w_ref[0, :half], preferred_element_type=F32)
    y = y + jnp.dot(ob_ref[...], w_ref[0, half:], preferred_element_type=F32)
    o_ref[...] = x_ref[...] + _rms(y, g_ref[0])


def _oproj(x, oa, ob, wo, g, layer, *, tm=512):
    s, d = x.shape
    width = oa.shape[1]
    return pl.pallas_call(
        _oproj_kernel,
        grid=(s // tm,),
        in_specs=[
            pl.BlockSpec((tm, d), lambda m: (m, 0)),
            pl.BlockSpec((tm, width), lambda m: (m, 0)),
            pl.BlockSpec((tm, width), lambda m: (m, 0)),
            pl.BlockSpec((1,) + wo.shape[1:], lambda m: (layer, 0, 0)),
            pl.BlockSpec((1, 1, d), lambda m: (layer, 0, 0)),
        ],
        out_specs=pl.BlockSpec((tm, d), lambda m: (m, 0)),
        out_shape=jax.ShapeDtypeStruct((s, d), F32),
        compiler_params=_params("parallel"),
        name="oproj",
    )(x, oa, ob, wo, g)


def _cast_kernel(w_ref, o_ref):
    o_ref[...] = w_ref[...].astype(BF16)


def _cast_layer(w, layer, *, steps=8):
    _, r, c = w.shape
    rows = r // steps
    return pl.pallas_call(
        _cast_kernel,
        grid=(steps,),
        in_specs=[pl.BlockSpec((1, rows, c), lambda i: (layer, i, 0))],
        out_specs=pl.BlockSpec((1, rows, c), lambda i: (0, i, 0)),
        out_shape=jax.ShapeDtypeStruct((1, r, c), BF16),
        compiler_params=_params("parallel"),
        name="cast_layer",
    )(w)


def _rope_angles(pos, dim):
    inv = ROPE_THETA ** (-jnp.arange(0, dim, 2, dtype=F32) / dim)
    ang = pos.astype(F32)[:, None] * inv
    return jnp.cos(ang), jnp.sin(ang)


def _rope_tables(cos, sin):
    n, half = cos.shape
    dim = 2 * half
    c = jnp.concatenate([cos, cos, jnp.ones((n, LANES - dim), F32)], axis=1)
    a = jnp.concatenate([-sin, jnp.zeros((n, LANES - half), F32)], axis=1)
    b = jnp.concatenate([jnp.zeros((n, half), F32), sin, jnp.zeros((n, LANES - dim), F32)], axis=1)
    return c, a, b


def kernel(x, positions, ffn1_pre_g, ffn1_post_g, ffn1_w_gate, ffn1_w_up, ffn1_w_down,
           mix_pre_g, mix_post_g, w_in, mla_q_norm_g, mla_w_uq, mla_kv_norm_g, mla_w_ukv,
           w_o, ffn2_pre_g, ffn2_post_g, ffn2_w_gate, ffn2_w_up, ffn2_w_down):
    batch, _, _ = x.shape
    depth = w_in.shape[0]
    bf = lambda w: w.astype(BF16)

    w_in = bf(w_in)
    wc = wkr = wd = w_in
    wuq = bf(mla_w_uq)
    wukv = bf(mla_w_ukv)
    wo = bf(w_o)
    ffn_f32 = ((ffn1_w_gate, ffn1_w_up, ffn1_w_down), (ffn2_w_gate, ffn2_w_up, ffn2_w_down))
    ffn_gains = ((ffn1_pre_g, ffn1_post_g), (ffn2_pre_g, ffn2_post_g))
    ffn_bf16 = {(0, 0): tuple(_cast_layer(w, 0) for w in ffn_f32[0])}

    def ffn(xs, l, which):
        nxt = (l, 1) if which == 0 else (l + 1, 0)
        cast_next = None
        if nxt[0] < depth and nxt not in ffn_bf16:
            cast_next = ffn_f32[nxt[1]] + (nxt[0],)
        gpre, gpost = (g[l][None, None, :] for g in ffn_gains[which])
        xs, converted = _ffn(xs, gpre, gpost, ffn_bf16[(l, which)], 0, cast_next)
        if cast_next is not None:
            ffn_bf16[nxt] = converted
        return xs
    row = lambda g: g[:, None, :]
    mix_pre_g, mix_post_g, mla_q_norm_g, mla_kv_norm_g = map(
        row, (mix_pre_g, mix_post_g, mla_q_norm_g, mla_kv_norm_g))

    outs = []
    for bi in range(batch):
        xs = x[bi]
        tabs_a = _rope_tables(*_rope_angles(positions[bi], MLA_ROPE))
        tabs_p = _rope_tables(*_rope_angles(positions[bi], PART_ROPE))
        for l in range(depth):
            xs = ffn(xs, l, 0)
            qkv = _dproj(xs, mix_pre_g, wd, tabs_p, l)
            qm, km, vm = _mproj(xs, mix_pre_g, wc, wkr, mla_q_norm_g, wuq,
                                mla_kv_norm_g, wukv, tabs_a, l)
            oa = _flash(qm, km, vm)
            ob = _dilated(qkv)
            xs = _oproj(xs, oa, ob, wo, mix_post_g, l)
            xs = ffn(xs, l, 1)
        outs.append(xs)
    return jnp.stack(outs, axis=0)
```

```python
import functools
import math

import jax
import jax.numpy as jnp
from jax import lax
from jax.experimental import pallas as pl
from jax.experimental.pallas import tpu as pltpu

F32 = jnp.float32
BF16 = jnp.bfloat16

HEAD_DIM = 128
N_HEADS = 8
MLA_RANK = 512
MLA_NOPE = 128
MLA_ROPE = 64
MLA_QK = MLA_NOPE + MLA_ROPE
MLA_QK_PAD = 256
PART_ROPE = 32
DIL_PATTERNS = ((128, 1), (512, 4), (2048, 16))
DIL_SPAN = 128
DIL_TILE = 2048
ROPE_THETA = 500000.0
RMS_EPS = 1e-6
NEG = -1e30
LANES = 128
VMEM_LIMIT = 58 * 1024 * 1024
FFN_NORM_ROWS = 256
FFN_DOT_ROWS = 512
DPROJ_DOT_ROWS = 128
DIL_QSCALE = math.log2(math.e) / math.sqrt(HEAD_DIM)


def _rms(xf, g):
    ms = jnp.mean(xf * xf, axis=-1, keepdims=True)
    return xf * lax.rsqrt(ms + RMS_EPS) * g


def _rope(y, c, a, b, half):
    return y * c + pltpu.roll(y, LANES - half, axis=1) * a + pltpu.roll(y, half, axis=1) * b


def _params(*sem):
    return pltpu.CompilerParams(dimension_semantics=sem, vmem_limit_bytes=VMEM_LIMIT)


def _ffn_kernel(*refs, cast_next):
    x_ref, gpre_ref, gpost_ref, wg_ref, wu_ref, wd_ref = refs[:6]
    if cast_next:
        o_ref, nb_ref, r_ref = refs[9], refs[13], refs[14]
        for src_ref, dst_ref in zip(refs[6:9], refs[10:13]):
            dst_ref[...] = src_ref[...].astype(BF16)
    else:
        o_ref, nb_ref, r_ref = refs[6:]
    j = pl.program_id(1)

    tm = x_ref.shape[0]

    def row_chunks(size):
        return [slice(r, r + size) for r in range(0, tm, size)]

    last = pl.num_programs(1) - 1

    def chunk_sum(first):
        for rows in row_chunks(FFN_DOT_ROWS):
            nb = nb_ref[rows]
            g = jnp.dot(nb, wg_ref[0], preferred_element_type=F32)
            u = jnp.dot(nb, wu_ref[0], preferred_element_type=F32)
            h = (g * jax.nn.sigmoid(g) * u).astype(BF16)
            part = jnp.dot(h, wd_ref[0], preferred_element_type=F32)
            if first:
                o_ref[rows] = part
            else:
                o_ref[rows] += part

    @pl.when(j == 0)
    def _():
        for rows in row_chunks(FFN_NORM_ROWS):
            nb_ref[rows] = _rms(x_ref[rows], gpre_ref[0]).astype(BF16)
        chunk_sum(True)

    @pl.when(jnp.logical_and(j > 0, j < last))
    def _():
        chunk_sum(False)

    @pl.when(j == last)
    def _():
        chunk_sum(False)
        for rows in row_chunks(FFN_NORM_ROWS):
            y = o_ref[rows]
            r_ref[rows] = 0.5 * lax.rsqrt(jnp.mean(y * y, axis=-1, keepdims=True) + RMS_EPS)
        for rows in row_chunks(FFN_NORM_ROWS):
            o_ref[rows] = x_ref[rows] + o_ref[rows] * r_ref[rows] * gpost_ref[0]


def _ffn(x, gpre, gpost, weights, layer, cast_next=None, *, tm=1024, tn=512):
    s, d = x.shape
    wg, wu, wd = weights
    dff = wg.shape[-1]
    nm = s // tm
    dm = d // nm
    assert dff // tn >= 2, "first and last D_FF steps must be distinct grid steps"
    in_specs = [
        pl.BlockSpec((tm, d), lambda m, j: (m, 0)),
        pl.BlockSpec((1, 1, d), lambda m, j: (layer, 0, 0)),
        pl.BlockSpec((1, 1, d), lambda m, j: (layer, 0, 0)),
        pl.BlockSpec((1, d, tn), lambda m, j: (layer, 0, j)),
        pl.BlockSpec((1, d, tn), lambda m, j: (layer, 0, j)),
        pl.BlockSpec((1, tn, d), lambda m, j: (layer, j, 0)),
    ]
    out_specs = [pl.BlockSpec((tm, d), lambda m, j: (m, 0))]
    out_shape = [jax.ShapeDtypeStruct((s, d), F32)]
    args = [x, gpre, gpost, wg, wu, wd]
    if cast_next is not None:
        *nxt, nl = cast_next
        in_specs += [
            pl.BlockSpec((1, dm, tn), lambda m, j: (nl, m, j)),
            pl.BlockSpec((1, dm, tn), lambda m, j: (nl, m, j)),
            pl.BlockSpec((1, tn, dm), lambda m, j: (nl, j, m)),
        ]
        out_specs += [
            pl.BlockSpec((1, dm, tn), lambda m, j: (0, m, j)),
            pl.BlockSpec((1, dm, tn), lambda m, j: (0, m, j)),
            pl.BlockSpec((1, tn, dm), lambda m, j: (0, j, m)),
        ]
        out_shape += [jax.ShapeDtypeStruct((1,) + w.shape[1:], BF16) for w in nxt]
        args += nxt
    out = pl.pallas_call(
        functools.partial(_ffn_kernel, cast_next=cast_next is not None),
        grid=(nm, dff // tn),
        in_specs=in_specs,
        out_specs=out_specs,
        out_shape=out_shape,
        scratch_shapes=[pltpu.VMEM((tm, d), BF16), pltpu.VMEM((tm, 1), F32)],
        compiler_params=_params("parallel", "arbitrary"),
        name="ffn",
    )(*args)
    return out[0], tuple(out[1:])


def _dproj_kernel(x_ref, g_ref, w_ref, wx_ref, c_ref, a_ref, b_ref, o_ref, nb_ref, t_sc):
    j = pl.program_id(1)
    shift = MLA_ROPE
    piece = 2 * HEAD_DIM

    def project():
        is_v = j == 2
        scale = jnp.where(j == 0, DIL_QSCALE, 1.0).astype(F32)
        t_sc[0] = jnp.where(is_v, 1.0, c_ref[...] * scale)
        t_sc[1] = jnp.where(is_v, 0.0, a_ref[...] * scale)
        t_sc[2] = jnp.where(is_v, 0.0, b_ref[...] * scale)
        w_cat = jnp.concatenate([w_ref[0], wx_ref[0]], axis=1)
        for lo in range(0, N_HEADS * HEAD_DIM, piece):
            wp = w_cat[:, shift + lo:shift + lo + piece]
            for r in range(0, x_ref.shape[0], DPROJ_DOT_ROWS):
                rows = slice(r, r + DPROJ_DOT_ROWS)
                y = jnp.dot(nb_ref[rows], wp, preferred_element_type=F32)
                c, a, b = t_sc[0, rows], t_sc[1, rows], t_sc[2, rows]
                for h in range(piece // HEAD_DIM):
                    sl = slice(h * HEAD_DIM, (h + 1) * HEAD_DIM)
                    osl = slice(lo + h * HEAD_DIM, lo + (h + 1) * HEAD_DIM)
                    o_ref[0, rows, osl] = _rope(y[:, sl], c, a, b, PART_ROPE // 2)

    @pl.when(j == 0)
    def _():
        for r in range(0, x_ref.shape[0], FFN_NORM_ROWS):
            rows = slice(r, r + FFN_NORM_ROWS)
            nb_ref[rows] = _rms(x_ref[rows], g_ref[0]).astype(BF16)
        project()

    @pl.when(j > 0)
    def _():
        project()


def _dproj(x, g, w_in, tabs, layer, *, tm=1024):
    s, d = x.shape
    width = N_HEADS * HEAD_DIM
    first = 2 * MLA_RANK // width
    tab_spec = pl.BlockSpec((tm, LANES), lambda m, j: (m, 0))
    return pl.pallas_call(
        _dproj_kernel,
        grid=(s // tm, 3),
        in_specs=[
            pl.BlockSpec((tm, d), lambda m, j: (m, 0)),
            pl.BlockSpec((1, 1, d), lambda m, j: (layer, 0, 0)),
            pl.BlockSpec((1, d, width), lambda m, j: (layer, 0, first + j)),
            pl.BlockSpec((1, d, LANES), lambda m, j: (layer, 0, (first + j + 1) * (width // LANES))),
            tab_spec, tab_spec, tab_spec,
        ],
        out_specs=pl.BlockSpec((1, tm, width), lambda m, j: (j, m, 0)),
        out_shape=jax.ShapeDtypeStruct((3, s, width), F32),
        scratch_shapes=[pltpu.VMEM((tm, d), BF16), pltpu.VMEM((3, tm, LANES), F32)],
        compiler_params=_params("parallel", "arbitrary"),
        name="dproj",
    )(x, g, w_in, w_in, *tabs)


def _mproj_kernel(x_ref, g_ref, wc_ref, wkr_ref, gq_ref, wuq_ref, gkv_ref, wukv_ref,
                  c_ref, a_ref, b_ref, q_ref, k_ref, v_ref):
    nb = _rms(x_ref[...], g_ref[0]).astype(BF16)
    lat = jnp.dot(nb, wc_ref[0], preferred_element_type=F32)
    cq = _rms(lat[:, :MLA_RANK], gq_ref[0]).astype(BF16)
    ckv = _rms(lat[:, MLA_RANK:], gkv_ref[0]).astype(BF16)
    c, a, b = c_ref[...], a_ref[...], b_ref[...]
    half = MLA_ROPE // 2
    kr = jnp.dot(nb, wkr_ref[0], preferred_element_type=F32)
    lane = lax.broadcasted_iota(jnp.int32, kr.shape, 1)
    kr = jnp.where(lane < MLA_ROPE, _rope(kr, c, a, b, half), 0.0).astype(BF16)
    q = jnp.dot(cq, wuq_ref[0], preferred_element_type=F32)
    kv = jnp.dot(ckv, wukv_ref[0], preferred_element_type=F32)
    scale = math.log2(math.e) / math.sqrt(MLA_QK)
    ones = jnp.ones((x_ref.shape[0], HEAD_DIM), BF16)
    qpad = jnp.zeros((x_ref.shape[0], MLA_QK_PAD - MLA_QK), F32)
    for h in range(N_HEADS):
        lo = h * MLA_QK_PAD
        qlo = h * MLA_QK
        q_ref[h, :, :MLA_NOPE] = (q[:, qlo:qlo + MLA_NOPE] * scale).astype(BF16)
        qr = jnp.concatenate([q[:, qlo + MLA_NOPE:qlo + MLA_QK], qpad], axis=1)
        qr = _rope(qr, c, a, b, half)
        q_ref[h, :, MLA_NOPE:] = (qr * scale).astype(BF16)
        k_ref[h, :, :MLA_NOPE] = kv[:, lo:lo + MLA_NOPE].astype(BF16)
        k_ref[h, :, MLA_NOPE:] = kr
        v_ref[h, :, :HEAD_DIM] = kv[:, lo + MLA_NOPE:lo + MLA_QK_PAD].astype(BF16)
        v_ref[h, :, HEAD_DIM:] = ones


def _mproj(x, g, wc, wkr, gq, wuq, gkv, wukv, tabs, layer, *, tm=512):
    s, d = x.shape
    tab_spec = pl.BlockSpec((tm, LANES), lambda m: (m, 0))

    def wspec(w):
        return pl.BlockSpec((1,) + w.shape[1:], lambda m: (layer, 0, 0))

    def gspec(gv):
        return pl.BlockSpec((1, 1, gv.shape[2]), lambda m: (layer, 0, 0))

    return pl.pallas_call(
        _mproj_kernel,
        grid=(s // tm,),
        in_specs=[
            pl.BlockSpec((tm, d), lambda m: (m, 0)),
            gspec(g),
            pl.BlockSpec((1, d, 2 * MLA_RANK), lambda m: (layer, 0, 0)),
            pl.BlockSpec((1, d, LANES), lambda m: (layer, 0, 2 * MLA_RANK // LANES)),
            gspec(gq), wspec(wuq), gspec(gkv), wspec(wukv),
            tab_spec, tab_spec, tab_spec,
        ],
        out_specs=[
            pl.BlockSpec((N_HEADS, tm, MLA_QK_PAD), lambda m: (0, m, 0)),
            pl.BlockSpec((N_HEADS, tm, MLA_QK_PAD), lambda m: (0, m, 0)),
            pl.BlockSpec((N_HEADS, tm, 2 * HEAD_DIM), lambda m: (0, m, 0)),
        ],
        out_shape=[
            jax.ShapeDtypeStruct((N_HEADS, s, MLA_QK_PAD), BF16),
            jax.ShapeDtypeStruct((N_HEADS, s, MLA_QK_PAD), BF16),
            jax.ShapeDtypeStruct((N_HEADS, s, 2 * HEAD_DIM), BF16),
        ],
        compiler_params=_params("parallel"),
        name="mproj",
    )(x, g, wc, wkr, gq, wuq, gkv, wukv, *tabs)


def _flash_kernel(q_ref, k_ref, v_ref, o_ref, m_ref, acc_ref,
                  sa_ref, sb_ref, sc_ref, pa_ref, pb_ref, pc_ref, *, tq):
    half = tq // 2
    i = pl.program_id(1)
    last = pl.num_programs(1) - 1
    m_ref[...] = jnp.full_like(m_ref, NEG)
    acc_ref[...] = jnp.zeros_like(acc_ref)

    def key_rows(c, n=tq):
        return pl.ds(pl.multiple_of(c * tq, tq), n)

    def scores(c, s_ref, p_ref, tile=i):
        kc = k_ref[0, key_rows(c), :]
        for st in range(2):
            q = q_ref[0, pl.ds(pl.multiple_of(tile * tq + st * half, half), half), :]
            s = lax.dot_general(q, kc, (((1,), (1,)), ((), ())), preferred_element_type=F32)
            s_ref[st] = s
            pm = s[:, :LANES]
            for t in range(1, tq // LANES):
                pm = jnp.maximum(pm, s[:, t * LANES:(t + 1) * LANES])
            p_ref[st] = pm

    def update(st, s, row_max, vc):
        m_prev = m_ref[st]
        m_new = jnp.maximum(m_prev, row_max)
        alpha = jnp.exp2(m_prev - m_new)
        pr = jnp.exp2(s - m_new).astype(BF16)
        acc_ref[st] = alpha * acc_ref[st] + jnp.dot(pr, vc, preferred_element_type=F32)
        m_ref[st] = m_new

    def accumulate(c, s_ref, p_ref):
        vc = v_ref[0, key_rows(c), :]
        for st in range(2):
            update(st, s_ref[st], jnp.max(p_ref[st], axis=-1, keepdims=True), vc)

    def finish(s_ref):
        scores(0, sc_ref, pc_ref, tile=jnp.minimum(i + 1, last))
        s0 = s_ref[0, :, :half]
        rows = lax.broadcasted_iota(jnp.int32, s0.shape, 0)
        cols = lax.broadcasted_iota(jnp.int32, s0.shape, 1)
        s0 = jnp.where(cols <= rows, s0, NEG)
        update(0, s0, jnp.max(s0, axis=-1, keepdims=True), v_ref[0, key_rows(i, half), :])
        s1 = s_ref[1]
        rows = lax.broadcasted_iota(jnp.int32, s1.shape, 0)
        cols = lax.broadcasted_iota(jnp.int32, s1.shape, 1)
        s1 = jnp.where(cols <= rows + half, s1, NEG)
        update(1, s1, jnp.max(s1, axis=-1, keepdims=True), v_ref[0, key_rows(i), :])
        for st in range(2):
            acc = acc_ref[st]
            o_ref[st * half:(st + 1) * half, :] = (
                acc[:, :HEAD_DIM] / acc[:, HEAD_DIM:]).astype(BF16)

    @pl.when(i == 0)
    def _():
        scores(0, sa_ref, pa_ref)
        finish(sa_ref)

    @pl.when(i > 0)
    def _():
        scores(1, sa_ref, pa_ref)
        accumulate(0, sc_ref, pc_ref)

    def pair(jj, carry):
        c = 2 * jj + 1
        scores(c + 1, sb_ref, pb_ref)
        accumulate(c, sa_ref, pa_ref)
        scores(c + 2, sa_ref, pa_ref)
        accumulate(c + 1, sb_ref, pb_ref)
        return carry

    lax.fori_loop(0, lax.div(jnp.maximum(i - 1, 0), 2), pair, 0)

    @pl.when(lax.rem(i, 2) == 1)
    def _():
        finish(sa_ref)

    @pl.when(jnp.logical_and(lax.rem(i, 2) == 0, i > 0))
    def _():
        scores(i, sb_ref, pb_ref)
        accumulate(i - 1, sa_ref, pa_ref)
        finish(sb_ref)


def _flash(q, k, v, *, tq=1024):
    _, s, _ = q.shape
    score_buf = pltpu.VMEM((2, tq // 2, tq), F32)
    pmax_buf = pltpu.VMEM((2, tq // 2, LANES), F32)
    return pl.pallas_call(
        functools.partial(_flash_kernel, tq=tq),
        grid=(N_HEADS, s // tq),
        in_specs=[
            pl.BlockSpec((1, s, MLA_QK_PAD), lambda h, i: (h, 0, 0)),
            pl.BlockSpec((1, s, MLA_QK_PAD), lambda h, i: (h, 0, 0)),
            pl.BlockSpec((1, s, 2 * HEAD_DIM), lambda h, i: (h, 0, 0)),
        ],
        out_specs=pl.BlockSpec((tq, HEAD_DIM), lambda h, i: (i, h)),
        out_shape=jax.ShapeDtypeStruct((s, N_HEADS * HEAD_DIM), BF16),
        scratch_shapes=[
            pltpu.VMEM((2, tq // 2, 1), F32),
            pltpu.VMEM((2, tq // 2, 2 * HEAD_DIM), F32),
            score_buf, score_buf, score_buf, pmax_buf, pmax_buf, pmax_buf,
        ],
        compiler_params=_params("arbitrary", "arbitrary"),
        name="mla_flash",
    )(q, k, v)


def _dil_kernel(q_ref, kc_ref, kp_ref, vc_ref, vp_ref, o_ref, o_sc, l_sc):
    t = q_ref.shape[1]
    blk = DIL_SPAN
    n = pl.program_id(0)

    qi = lax.broadcasted_iota(jnp.int32, (blk, 2 * blk), 0)
    kj = lax.broadcasted_iota(jnp.int32, (blk, 2 * blk), 1)
    diff = kj - qi
    band = (diff >= 0) & (diff <= DIL_SPAN)
    band_first = band & (kj >= jnp.where(n == 0, blk, 0))

    ones = jnp.ones((2 * blk, HEAD_DIM), BF16)

    def rows(ref, start, d):
        idx = pl.ds(start, blk) if d == 1 else pl.ds(start, blk, stride=d)
        return ref[0, idx, :].astype(BF16)

    for idx, (window, d) in enumerate(DIL_PATTERNS):
        seg = blk * d
        for it in range(t // blk):
            r, b = it % d, it // d
            start = b * seg + r
            q = rows(q_ref, start, d)
            if b == 0:
                kp, vp = rows(kp_ref, t - seg + r, d), rows(vp_ref, t - seg + r, d)
            else:
                kp, vp = rows(kc_ref, start - seg, d), rows(vc_ref, start - seg, d)
            k = jnp.concatenate([kp, rows(kc_ref, start, d)], axis=0)
            v = jnp.concatenate([vp, rows(vc_ref, start, d)], axis=0)
            v = jnp.concatenate([v, ones], axis=1)
            s = lax.dot_general(q, k, (((1,), (1,)), ((), ())), preferred_element_type=F32)
            s = jnp.where(band_first if b == 0 else band, s, NEG)
            m = jnp.max(s, axis=-1, keepdims=True)
            pr = jnp.exp2(s - m)
            ol = jnp.dot(pr.astype(BF16), v, preferred_element_type=F32)
            l = ol[:, HEAD_DIM:]
            out_rows = pl.ds(start, blk) if d == 1 else pl.ds(start, blk, stride=d)
            o_sc[idx, out_rows, :] = ol[:, :HEAD_DIM] / l
            l_sc[idx, out_rows, :] = m + jnp.log2(l)

    lses = [l_sc[p] for p in range(len(DIL_PATTERNS))]
    mx = functools.reduce(jnp.maximum, lses)
    ws = [jnp.exp2(l - mx) for l in lses]
    num = sum(w * o_sc[p] for p, w in enumerate(ws))
    o_ref[...] = (num / sum(ws)).astype(BF16)


def _dilated(qkv):
    _, s, width = qkv.shape
    t = DIL_TILE
    npat = len(DIL_PATTERNS)

    def cur(which):
        return pl.BlockSpec((1, t, HEAD_DIM), lambda n, h: (which, n, h))

    def prv(which):
        return pl.BlockSpec((1, t, HEAD_DIM), lambda n, h: (which, jnp.maximum(n - 1, 0), h))

    return pl.pallas_call(
        _dil_kernel,
        grid=(s // t, N_HEADS),
        in_specs=[cur(0), cur(1), prv(1), cur(2), prv(2)],
        out_specs=pl.BlockSpec((t, HEAD_DIM), lambda n, h: (n, h)),
        out_shape=jax.ShapeDtypeStruct((s, width), BF16),
        scratch_shapes=[
            pltpu.VMEM((npat, t, HEAD_DIM), F32),
            pltpu.VMEM((npat, t, LANES), F32),
        ],
        compiler_params=_params("parallel", "parallel"),
        name="dilated",
    )(qkv, qkv, qkv, qkv, qkv)


def _oproj_kernel(x_ref, oa_ref, ob_ref, w_ref, g_ref, o_ref):
    half = oa_ref.shape[1]
    y = jnp.dot(oa_ref[...], w_ref[0, :half], preferred_element_type=F32)
    y = y + jnp.dot(ob_ref[...], w_ref[0, half:], preferred_element_type=F32)
    o_ref[...] = x_ref[...] + _rms(y, g_ref[0])


def _oproj(x, oa, ob, wo, g, layer, *, tm=512):
    s, d = x.shape
    width = oa.shape[1]
    return pl.pallas_call(
        _oproj_kernel,
        grid=(s // tm,),
        in_specs=[
            pl.BlockSpec((tm, d), lambda m: (m, 0)),
            pl.BlockSpec((tm, width), lambda m: (m, 0)),
            pl.BlockSpec((tm, width), lambda m: (m, 0)),
            pl.BlockSpec((1,) + wo.shape[1:], lambda m: (layer, 0, 0)),
            pl.BlockSpec((1, 1, d), lambda m: (layer, 0, 0)),
        ],
        out_specs=pl.BlockSpec((tm, d), lambda m: (m, 0)),
        out_shape=jax.ShapeDtypeStruct((s, d), F32),
        compiler_params=_params("parallel"),
        name="oproj",
    )(x, oa, ob, wo, g)


def _cast_kernel(w_ref, o_ref):
    o_ref[...] = w_ref[...].astype(BF16)


def _cast_layer(w, layer, *, steps=8):
    _, r, c = w.shape
    rows = r // steps
    return pl.pallas_call(
        _cast_kernel,
        grid=(steps,),
        in_specs=[pl.BlockSpec((1, rows, c), lambda i: (layer, i, 0))],
        out_specs=pl.BlockSpec((1, rows, c), lambda i: (0, i, 0)),
        out_shape=jax.ShapeDtypeStruct((1, r, c), BF16),
        compiler_params=_params("parallel"),
        name="cast_layer",
    )(w)


def _rope_angles(pos, dim):
    inv = ROPE_THETA ** (-jnp.arange(0, dim, 2, dtype=F32) / dim)
    ang = pos.astype(F32)[:, None] * inv
    return jnp.cos(ang), jnp.sin(ang)


def _rope_tables(cos, sin):
    n, half = cos.shape
    dim = 2 * half
    c = jnp.concatenate([cos, cos, jnp.ones((n, LANES - dim), F32)], axis=1)
    a = jnp.concatenate([-sin, jnp.zeros((n, LANES - half), F32)], axis=1)
    b = jnp.concatenate([jnp.zeros((n, half), F32), sin, jnp.zeros((n, LANES - dim), F32)], axis=1)
    return c, a, b


def kernel(x, positions, ffn1_pre_g, ffn1_post_g, ffn1_w_gate, ffn1_w_up, ffn1_w_down,
           mix_pre_g, mix_post_g, w_in, mla_q_norm_g, mla_w_uq, mla_kv_norm_g, mla_w_ukv,
           w_o, ffn2_pre_g, ffn2_post_g, ffn2_w_gate, ffn2_w_up, ffn2_w_down):
    batch, _, _ = x.shape
    depth = w_in.shape[0]
    bf = lambda w: w.astype(BF16)

    w_in = jnp.pad(bf(w_in), ((0, 0), (0, 0), (0, -w_in.shape[2] % LANES)))
    wc = wkr = wd = w_in
    wuq = bf(mla_w_uq)
    wukv = bf(mla_w_ukv)
    wo = bf(w_o)
    ffn_f32 = ((ffn1_w_gate, ffn1_w_up, ffn1_w_down), (ffn2_w_gate, ffn2_w_up, ffn2_w_down))
    ffn_gains = ((ffn1_pre_g, ffn1_post_g), (ffn2_pre_g, ffn2_post_g))
    ffn_bf16 = {(0, 0): tuple(_cast_layer(w, 0) for w in ffn_f32[0])}

    def ffn(xs, l, which):
        nxt = (l, 1) if which == 0 else (l + 1, 0)
        cast_next = None
        if nxt[0] < depth and nxt not in ffn_bf16:
            cast_next = ffn_f32[nxt[1]] + (nxt[0],)
        gpre, gpost = (g[l][None, None, :] for g in ffn_gains[which])
        xs, converted = _ffn(xs, gpre, gpost, ffn_bf16[(l, which)], 0, cast_next)
        if cast_next is not None:
            ffn_bf16[nxt] = converted
        return xs
    row = lambda g: g[:, None, :]
    mix_pre_g, mix_post_g, mla_q_norm_g, mla_kv_norm_g = map(
        row, (mix_pre_g, mix_post_g, mla_q_norm_g, mla_kv_norm_g))

    outs = []
    for bi in range(batch):
        xs = x[bi]
        tabs_a = _rope_tables(*_rope_angles(positions[bi], MLA_ROPE))
        tabs_p = _rope_tables(*_rope_angles(positions[bi], PART_ROPE))
        for l in range(depth):
            xs = ffn(xs, l, 0)
            qkv = _dproj(xs, mix_pre_g, wd, tabs_p, l)
            qm, km, vm = _mproj(xs, mix_pre_g, wc, wkr, mla_q_norm_g, wuq,
                                mla_kv_norm_g, wukv, tabs_a, l)
            oa = _flash(qm, km, vm)
            ob = _dilated(qkv)
            xs = _oproj(xs, oa, ob, wo, mix_post_g, l)
            xs = ffn(xs, l, 1)
        outs.append(xs)
    return jnp.stack(outs, axis=0)
```

```python
import functools
import math

import jax
import jax.numpy as jnp
from jax import lax
from jax.experimental import pallas as pl
from jax.experimental.pallas import tpu as pltpu

F32 = jnp.float32
BF16 = jnp.bfloat16

HEAD_DIM = 128
N_HEADS = 8
MLA_RANK = 512
MLA_NOPE = 128
MLA_ROPE = 64
MLA_QK = MLA_NOPE + MLA_ROPE
MLA_QK_PAD = 256
PART_ROPE = 32
DIL_PATTERNS = ((128, 1), (512, 4), (2048, 16))
DIL_SPAN = 128
DIL_TILE = 2048
ROPE_THETA = 500000.0
RMS_EPS = 1e-6
NEG = -1e30
LANES = 128
VMEM_LIMIT = 58 * 1024 * 1024
FFN_NORM_ROWS = 256
FFN_DOT_ROWS = 512
DPROJ_DOT_ROWS = 128
OPROJ_DOT_ROWS = 512
DIL_QSCALE = math.log2(math.e) / math.sqrt(HEAD_DIM)


def _rms(xf, g):
    ms = jnp.mean(xf * xf, axis=-1, keepdims=True)
    return xf * lax.rsqrt(ms + RMS_EPS) * g


def _rope(y, c, a, b, half):
    return y * c + pltpu.roll(y, LANES - half, axis=1) * a + pltpu.roll(y, half, axis=1) * b


def _params(*sem):
    return pltpu.CompilerParams(dimension_semantics=sem, vmem_limit_bytes=VMEM_LIMIT)


def _ffn_kernel(*refs, cast_next):
    x_ref, gpre_ref, gpost_ref, wg_ref, wu_ref, wd_ref = refs[:6]
    if cast_next:
        o_ref, nb_ref, r_ref = refs[9], refs[13], refs[14]
        for src_ref, dst_ref in zip(refs[6:9], refs[10:13]):
            dst_ref[...] = src_ref[...].astype(BF16)
    else:
        o_ref, nb_ref, r_ref = refs[6:]
    j = pl.program_id(1)

    tm = x_ref.shape[0]

    def row_chunks(size):
        return [slice(r, r + size) for r in range(0, tm, size)]

    last = pl.num_programs(1) - 1

    def chunk_sum(first):
        for rows in row_chunks(FFN_DOT_ROWS):
            nb = nb_ref[rows]
            g = jnp.dot(nb, wg_ref[0], preferred_element_type=F32)
            u = jnp.dot(nb, wu_ref[0], preferred_element_type=F32)
            h = (g * jax.nn.sigmoid(g) * u).astype(BF16)
            part = jnp.dot(h, wd_ref[0], preferred_element_type=F32)
            if first:
                o_ref[rows] = part
            else:
                o_ref[rows] += part

    @pl.when(j == 0)
    def _():
        for rows in row_chunks(FFN_NORM_ROWS):
            nb_ref[rows] = _rms(x_ref[rows], gpre_ref[0]).astype(BF16)
        chunk_sum(True)

    @pl.when(jnp.logical_and(j > 0, j < last))
    def _():
        chunk_sum(False)

    @pl.when(j == last)
    def _():
        chunk_sum(False)
        for rows in row_chunks(FFN_NORM_ROWS):
            y = o_ref[rows]
            r_ref[rows] = 0.5 * lax.rsqrt(jnp.mean(y * y, axis=-1, keepdims=True) + RMS_EPS)
        for rows in row_chunks(FFN_NORM_ROWS):
            o_ref[rows] = x_ref[rows] + o_ref[rows] * r_ref[rows] * gpost_ref[0]


def _ffn(x, gpre, gpost, weights, layer, cast_next=None, *, tm=1024, tn=512):
    s, d = x.shape
    wg, wu, wd = weights
    dff = wg.shape[-1]
    nm = s // tm
    dm = d // nm
    assert dff // tn >= 2, "first and last D_FF steps must be distinct grid steps"
    in_specs = [
        pl.BlockSpec((tm, d), lambda m, j: (m, 0)),
        pl.BlockSpec((1, 1, d), lambda m, j: (layer, 0, 0)),
        pl.BlockSpec((1, 1, d), lambda m, j: (layer, 0, 0)),
        pl.BlockSpec((1, d, tn), lambda m, j: (layer, 0, j)),
        pl.BlockSpec((1, d, tn), lambda m, j: (layer, 0, j)),
        pl.BlockSpec((1, tn, d), lambda m, j: (layer, j, 0)),
    ]
    out_specs = [pl.BlockSpec((tm, d), lambda m, j: (m, 0))]
    out_shape = [jax.ShapeDtypeStruct((s, d), F32)]
    args = [x, gpre, gpost, wg, wu, wd]
    if cast_next is not None:
        *nxt, nl = cast_next
        in_specs += [
            pl.BlockSpec((1, dm, tn), lambda m, j: (nl, m, j)),
            pl.BlockSpec((1, dm, tn), lambda m, j: (nl, m, j)),
            pl.BlockSpec((1, tn, dm), lambda m, j: (nl, j, m)),
        ]
        out_specs += [
            pl.BlockSpec((1, dm, tn), lambda m, j: (0, m, j)),
            pl.BlockSpec((1, dm, tn), lambda m, j: (0, m, j)),
            pl.BlockSpec((1, tn, dm), lambda m, j: (0, j, m)),
        ]
        out_shape += [jax.ShapeDtypeStruct((1,) + w.shape[1:], BF16) for w in nxt]
        args += nxt
    out = pl.pallas_call(
        functools.partial(_ffn_kernel, cast_next=cast_next is not None),
        grid=(nm, dff // tn),
        in_specs=in_specs,
        out_specs=out_specs,
        out_shape=out_shape,
        scratch_shapes=[pltpu.VMEM((tm, d), BF16), pltpu.VMEM((tm, 1), F32)],
        compiler_params=_params("parallel", "arbitrary"),
        name="ffn",
    )(*args)
    return out[0], tuple(out[1:])


def _dproj_kernel(x_ref, g_ref, w_ref, wx_ref, c_ref, a_ref, b_ref, o_ref, nb_ref, t_sc):
    j = pl.program_id(1)
    shift = MLA_ROPE
    piece = 2 * HEAD_DIM

    def project():
        is_v = j == 2
        scale = jnp.where(j == 0, DIL_QSCALE, 1.0).astype(F32)
        t_sc[0] = jnp.where(is_v, 1.0, c_ref[...] * scale)
        t_sc[1] = jnp.where(is_v, 0.0, a_ref[...] * scale)
        t_sc[2] = jnp.where(is_v, 0.0, b_ref[...] * scale)
        w_cat = jnp.concatenate([w_ref[0], wx_ref[0]], axis=1)
        for lo in range(0, N_HEADS * HEAD_DIM, piece):
            wp = w_cat[:, shift + lo:shift + lo + piece]
            for r in range(0, x_ref.shape[0], DPROJ_DOT_ROWS):
                rows = slice(r, r + DPROJ_DOT_ROWS)
                y = jnp.dot(nb_ref[rows], wp, preferred_element_type=F32)
                c, a, b = t_sc[0, rows], t_sc[1, rows], t_sc[2, rows]
                for h in range(piece // HEAD_DIM):
                    sl = slice(h * HEAD_DIM, (h + 1) * HEAD_DIM)
                    osl = slice(lo + h * HEAD_DIM, lo + (h + 1) * HEAD_DIM)
                    o_ref[0, rows, osl] = _rope(y[:, sl], c, a, b, PART_ROPE // 2)

    @pl.when(j == 0)
    def _():
        for r in range(0, x_ref.shape[0], FFN_NORM_ROWS):
            rows = slice(r, r + FFN_NORM_ROWS)
            nb_ref[rows] = _rms(x_ref[rows], g_ref[0]).astype(BF16)
        project()

    @pl.when(j > 0)
    def _():
        project()


def _dproj(x, g, w_in, tabs, layer, *, tm=1024):
    s, d = x.shape
    width = N_HEADS * HEAD_DIM
    first = 2 * MLA_RANK // width
    tab_spec = pl.BlockSpec((tm, LANES), lambda m, j: (m, 0))
    return pl.pallas_call(
        _dproj_kernel,
        grid=(s // tm, 3),
        in_specs=[
            pl.BlockSpec((tm, d), lambda m, j: (m, 0)),
            pl.BlockSpec((1, 1, d), lambda m, j: (layer, 0, 0)),
            pl.BlockSpec((1, d, width), lambda m, j: (layer, 0, first + j)),
            pl.BlockSpec((1, d, LANES), lambda m, j: (layer, 0, (first + j + 1) * (width // LANES))),
            tab_spec, tab_spec, tab_spec,
        ],
        out_specs=pl.BlockSpec((1, tm, width), lambda m, j: (j, m, 0)),
        out_shape=jax.ShapeDtypeStruct((3, s, width), F32),
        scratch_shapes=[pltpu.VMEM((tm, d), BF16), pltpu.VMEM((3, tm, LANES), F32)],
        compiler_params=_params("parallel", "arbitrary"),
        name="dproj",
    )(x, g, w_in, w_in, *tabs)


def _mproj_kernel(x_ref, g_ref, wc_ref, wkr_ref, gq_ref, wuq_ref, gkv_ref, wukv_ref,
                  c_ref, a_ref, b_ref, q_ref, k_ref, v_ref):
    nb = _rms(x_ref[...], g_ref[0]).astype(BF16)
    lat = jnp.dot(nb, wc_ref[0], preferred_element_type=F32)
    cq = _rms(lat[:, :MLA_RANK], gq_ref[0]).astype(BF16)
    ckv = _rms(lat[:, MLA_RANK:], gkv_ref[0]).astype(BF16)
    c, a, b = c_ref[...], a_ref[...], b_ref[...]
    half = MLA_ROPE // 2
    kr = jnp.dot(nb, wkr_ref[0], preferred_element_type=F32)
    lane = lax.broadcasted_iota(jnp.int32, kr.shape, 1)
    kr = jnp.where(lane < MLA_ROPE, _rope(kr, c, a, b, half), 0.0).astype(BF16)
    q = jnp.dot(cq, wuq_ref[0], preferred_element_type=F32)
    kv = jnp.dot(ckv, wukv_ref[0], preferred_element_type=F32)
    scale = math.log2(math.e) / math.sqrt(MLA_QK)
    ones = jnp.ones((x_ref.shape[0], HEAD_DIM), BF16)
    qpad = jnp.zeros((x_ref.shape[0], MLA_QK_PAD - MLA_QK), F32)
    for h in range(N_HEADS):
        lo = h * MLA_QK_PAD
        qlo = h * MLA_QK
        q_ref[h, :, :MLA_NOPE] = (q[:, qlo:qlo + MLA_NOPE] * scale).astype(BF16)
        qr = jnp.concatenate([q[:, qlo + MLA_NOPE:qlo + MLA_QK], qpad], axis=1)
        qr = _rope(qr, c, a, b, half)
        q_ref[h, :, MLA_NOPE:] = (qr * scale).astype(BF16)
        k_ref[h, :, :MLA_NOPE] = kv[:, lo:lo + MLA_NOPE].astype(BF16)
        k_ref[h, :, MLA_NOPE:] = kr
        v_ref[h, :, :HEAD_DIM] = kv[:, lo + MLA_NOPE:lo + MLA_QK_PAD].astype(BF16)
        v_ref[h, :, HEAD_DIM:] = ones


def _mproj(x, g, wc, wkr, gq, wuq, gkv, wukv, tabs, layer, *, tm=512):
    s, d = x.shape
    tab_spec = pl.BlockSpec((tm, LANES), lambda m: (m, 0))

    def wspec(w):
        return pl.BlockSpec((1,) + w.shape[1:], lambda m: (layer, 0, 0))

    def gspec(gv):
        return pl.BlockSpec((1, 1, gv.shape[2]), lambda m: (layer, 0, 0))

    return pl.pallas_call(
        _mproj_kernel,
        grid=(s // tm,),
        in_specs=[
            pl.BlockSpec((tm, d), lambda m: (m, 0)),
            gspec(g),
            pl.BlockSpec((1, d, 2 * MLA_RANK), lambda m: (layer, 0, 0)),
            pl.BlockSpec((1, d, LANES), lambda m: (layer, 0, 2 * MLA_RANK // LANES)),
            gspec(gq), wspec(wuq), gspec(gkv), wspec(wukv),
            tab_spec, tab_spec, tab_spec,
        ],
        out_specs=[
            pl.BlockSpec((N_HEADS, tm, MLA_QK_PAD), lambda m: (0, m, 0)),
            pl.BlockSpec((N_HEADS, tm, MLA_QK_PAD), lambda m: (0, m, 0)),
            pl.BlockSpec((N_HEADS, tm, 2 * HEAD_DIM), lambda m: (0, m, 0)),
        ],
        out_shape=[
            jax.ShapeDtypeStruct((N_HEADS, s, MLA_QK_PAD), BF16),
            jax.ShapeDtypeStruct((N_HEADS, s, MLA_QK_PAD), BF16),
            jax.ShapeDtypeStruct((N_HEADS, s, 2 * HEAD_DIM), BF16),
        ],
        compiler_params=_params("parallel"),
        name="mproj",
    )(x, g, wc, wkr, gq, wuq, gkv, wukv, *tabs)


def _flash_kernel(q_ref, k_ref, v_ref, o_ref, m_ref, acc_ref,
                  sa_ref, sb_ref, sc_ref, pa_ref, pb_ref, pc_ref, *, tq):
    half = tq // 2
    i = pl.program_id(1)
    last = pl.num_programs(1) - 1
    m_ref[...] = jnp.full_like(m_ref, NEG)
    acc_ref[...] = jnp.zeros_like(acc_ref)

    def key_rows(c, n=tq):
        return pl.ds(pl.multiple_of(c * tq, tq), n)

    def scores(c, s_ref, p_ref, tile=i):
        kc = k_ref[0, key_rows(c), :]
        for st in range(2):
            q = q_ref[0, pl.ds(pl.multiple_of(tile * tq + st * half, half), half), :]
            s = lax.dot_general(q, kc, (((1,), (1,)), ((), ())), preferred_element_type=F32)
            s_ref[st] = s
            pm = s[:, :LANES]
            for t in range(1, tq // LANES):
                pm = jnp.maximum(pm, s[:, t * LANES:(t + 1) * LANES])
            p_ref[st] = pm

    def update(st, s, row_max, vc):
        m_prev = m_ref[st]
        m_new = jnp.maximum(m_prev, row_max)
        alpha = jnp.exp2(m_prev - m_new)
        pr = jnp.exp2(s - m_new).astype(BF16)
        acc_ref[st] = alpha * acc_ref[st] + jnp.dot(pr, vc, preferred_element_type=F32)
        m_ref[st] = m_new

    def accumulate(c, s_ref, p_ref):
        vc = v_ref[0, key_rows(c), :]
        for st in range(2):
            update(st, s_ref[st], jnp.max(p_ref[st], axis=-1, keepdims=True), vc)

    def finish(s_ref):
        scores(0, sc_ref, pc_ref, tile=jnp.minimum(i + 1, last))
        s0 = s_ref[0, :, :half]
        rows = lax.broadcasted_iota(jnp.int32, s0.shape, 0)
        cols = lax.broadcasted_iota(jnp.int32, s0.shape, 1)
        s0 = jnp.where(cols <= rows, s0, NEG)
        update(0, s0, jnp.max(s0, axis=-1, keepdims=True), v_ref[0, key_rows(i, half), :])
        s1 = s_ref[1]
        rows = lax.broadcasted_iota(jnp.int32, s1.shape, 0)
        cols = lax.broadcasted_iota(jnp.int32, s1.shape, 1)
        s1 = jnp.where(cols <= rows + half, s1, NEG)
        update(1, s1, jnp.max(s1, axis=-1, keepdims=True), v_ref[0, key_rows(i), :])
        for st in range(2):
            acc = acc_ref[st]
            o_ref[st * half:(st + 1) * half, :] = (
                acc[:, :HEAD_DIM] / acc[:, HEAD_DIM:]).astype(BF16)

    @pl.when(i == 0)
    def _():
        scores(0, sa_ref, pa_ref)
        finish(sa_ref)

    @pl.when(i > 0)
    def _():
        scores(1, sa_ref, pa_ref)
        accumulate(0, sc_ref, pc_ref)

    def pair(jj, carry):
        c = 2 * jj + 1
        scores(c + 1, sb_ref, pb_ref)
        accumulate(c, sa_ref, pa_ref)
        scores(c + 2, sa_ref, pa_ref)
        accumulate(c + 1, sb_ref, pb_ref)
        return carry

    lax.fori_loop(0, lax.div(jnp.maximum(i - 1, 0), 2), pair, 0)

    @pl.when(lax.rem(i, 2) == 1)
    def _():
        finish(sa_ref)

    @pl.when(jnp.logical_and(lax.rem(i, 2) == 0, i > 0))
    def _():
        scores(i, sb_ref, pb_ref)
        accumulate(i - 1, sa_ref, pa_ref)
        finish(sb_ref)


def _flash(q, k, v, *, tq=1024):
    _, s, _ = q.shape
    score_buf = pltpu.VMEM((2, tq // 2, tq), F32)
    pmax_buf = pltpu.VMEM((2, tq // 2, LANES), F32)
    return pl.pallas_call(
        functools.partial(_flash_kernel, tq=tq),
        grid=(N_HEADS, s // tq),
        in_specs=[
            pl.BlockSpec((1, s, MLA_QK_PAD), lambda h, i: (h, 0, 0)),
            pl.BlockSpec((1, s, MLA_QK_PAD), lambda h, i: (h, 0, 0)),
            pl.BlockSpec((1, s, 2 * HEAD_DIM), lambda h, i: (h, 0, 0)),
        ],
        out_specs=pl.BlockSpec((tq, HEAD_DIM), lambda h, i: (i, h)),
        out_shape=jax.ShapeDtypeStruct((s, N_HEADS * HEAD_DIM), BF16),
        scratch_shapes=[
            pltpu.VMEM((2, tq // 2, 1), F32),
            pltpu.VMEM((2, tq // 2, 2 * HEAD_DIM), F32),
            score_buf, score_buf, score_buf, pmax_buf, pmax_buf, pmax_buf,
        ],
        compiler_params=_params("arbitrary", "arbitrary"),
        name="mla_flash",
    )(q, k, v)


def _dil_kernel(q_ref, kc_ref, kp_ref, vc_ref, vp_ref, o_ref, o_sc, l_sc):
    t = q_ref.shape[1]
    blk = DIL_SPAN
    n = pl.program_id(0)

    qi = lax.broadcasted_iota(jnp.int32, (blk, 2 * blk), 0)
    kj = lax.broadcasted_iota(jnp.int32, (blk, 2 * blk), 1)
    diff = kj - qi
    band = (diff >= 0) & (diff <= DIL_SPAN)
    band_first = band & (kj >= jnp.where(n == 0, blk, 0))

    ones = jnp.ones((2 * blk, HEAD_DIM), BF16)

    def rows(ref, start, d):
        idx = pl.ds(start, blk) if d == 1 else pl.ds(start, blk, stride=d)
        return ref[0, idx, :].astype(BF16)

    for idx, (window, d) in enumerate(DIL_PATTERNS):
        seg = blk * d
        for it in range(t // blk):
            r, b = it % d, it // d
            start = b * seg + r
            q = rows(q_ref, start, d)
            if b == 0:
                kp, vp = rows(kp_ref, t - seg + r, d), rows(vp_ref, t - seg + r, d)
            else:
                kp, vp = rows(kc_ref, start - seg, d), rows(vc_ref, start - seg, d)
            k = jnp.concatenate([kp, rows(kc_ref, start, d)], axis=0)
            v = jnp.concatenate([vp, rows(vc_ref, start, d)], axis=0)
            v = jnp.concatenate([v, ones], axis=1)
            s = lax.dot_general(q, k, (((1,), (1,)), ((), ())), preferred_element_type=F32)
            s = jnp.where(band_first if b == 0 else band, s, NEG)
            m = jnp.max(s, axis=-1, keepdims=True)
            pr = jnp.exp2(s - m)
            ol = jnp.dot(pr.astype(BF16), v, preferred_element_type=F32)
            l = ol[:, HEAD_DIM:]
            out_rows = pl.ds(start, blk) if d == 1 else pl.ds(start, blk, stride=d)
            o_sc[idx, out_rows, :] = ol[:, :HEAD_DIM] / l
            l_sc[idx, out_rows, :] = m + jnp.log2(l)

    lses = [l_sc[p] for p in range(len(DIL_PATTERNS))]
    mx = functools.reduce(jnp.maximum, lses)
    ws = [jnp.exp2(l - mx) for l in lses]
    num = sum(w * o_sc[p] for p, w in enumerate(ws))
    o_ref[...] = (num / sum(ws)).astype(BF16)


def _dilated(qkv):
    _, s, width = qkv.shape
    t = DIL_TILE
    npat = len(DIL_PATTERNS)

    def cur(which):
        return pl.BlockSpec((1, t, HEAD_DIM), lambda n, h: (which, n, h))

    def prv(which):
        return pl.BlockSpec((1, t, HEAD_DIM), lambda n, h: (which, jnp.maximum(n - 1, 0), h))

    return pl.pallas_call(
        _dil_kernel,
        grid=(s // t, N_HEADS),
        in_specs=[cur(0), cur(1), prv(1), cur(2), prv(2)],
        out_specs=pl.BlockSpec((t, HEAD_DIM), lambda n, h: (n, h)),
        out_shape=jax.ShapeDtypeStruct((s, width), BF16),
        scratch_shapes=[
            pltpu.VMEM((npat, t, HEAD_DIM), F32),
            pltpu.VMEM((npat, t, LANES), F32),
        ],
        compiler_params=_params("parallel", "parallel"),
        name="dilated",
    )(qkv, qkv, qkv, qkv, qkv)


def _oproj_kernel(x_ref, oa_ref, ob_ref, w_ref, g_ref, o_ref):
    half = oa_ref.shape[1]
    for r in range(0, x_ref.shape[0], OPROJ_DOT_ROWS):
        rows = slice(r, r + OPROJ_DOT_ROWS)
        y = jnp.dot(oa_ref[rows], w_ref[0, :half], preferred_element_type=F32)
        y = y + jnp.dot(ob_ref[rows], w_ref[0, half:], preferred_element_type=F32)
        o_ref[rows] = x_ref[rows] + _rms(y, g_ref[0])


def _oproj(x, oa, ob, wo, g, layer, *, tm=1024):
    s, d = x.shape
    width = oa.shape[1]
    return pl.pallas_call(
        _oproj_kernel,
        grid=(s // tm,),
        in_specs=[
            pl.BlockSpec((tm, d), lambda m: (m, 0)),
            pl.BlockSpec((tm, width), lambda m: (m, 0)),
            pl.BlockSpec((tm, width), lambda m: (m, 0)),
            pl.BlockSpec((1,) + wo.shape[1:], lambda m: (layer, 0, 0), pipeline_mode=pl.Buffered(1)),
            pl.BlockSpec((1, 1, d), lambda m: (layer, 0, 0)),
        ],
        out_specs=pl.BlockSpec((tm, d), lambda m: (m, 0)),
        out_shape=jax.ShapeDtypeStruct((s, d), F32),
        compiler_params=_params("parallel"),
        name="oproj",
    )(x, oa, ob, wo, g)


def _cast_kernel(w_ref, o_ref):
    o_ref[...] = w_ref[...].astype(BF16)


def _cast_layer(w, layer, *, steps=8):
    _, r, c = w.shape
    rows = r // steps
    return pl.pallas_call(
        _cast_kernel,
        grid=(steps,),
        in_specs=[pl.BlockSpec((1, rows, c), lambda i: (layer, i, 0))],
        out_specs=pl.BlockSpec((1, rows, c), lambda i: (0, i, 0)),
        out_shape=jax.ShapeDtypeStruct((1, r, c), BF16),
        compiler_params=_params("parallel"),
        name="cast_layer",
    )(w)


def _rope_angles(pos, dim):
    inv = ROPE_THETA ** (-jnp.arange(0, dim, 2, dtype=F32) / dim)
    ang = pos.astype(F32)[:, None] * inv
    return jnp.cos(ang), jnp.sin(ang)


def _rope_tables(cos, sin):
    n, half = cos.shape
    dim = 2 * half
    c = jnp.concatenate([cos, cos, jnp.ones((n, LANES - dim), F32)], axis=1)
    a = jnp.concatenate([-sin, jnp.zeros((n, LANES - half), F32)], axis=1)
    b = jnp.concatenate([jnp.zeros((n, half), F32), sin, jnp.zeros((n, LANES - dim), F32)], axis=1)
    return c, a, b


def kernel(x, positions, ffn1_pre_g, ffn1_post_g, ffn1_w_gate, ffn1_w_up, ffn1_w_down,
           mix_pre_g, mix_post_g, w_in, mla_q_norm_g, mla_w_uq, mla_kv_norm_g, mla_w_ukv,
           w_o, ffn2_pre_g, ffn2_post_g, ffn2_w_gate, ffn2_w_up, ffn2_w_down):
    batch, _, _ = x.shape
    depth = w_in.shape[0]
    bf = lambda w: w.astype(BF16)

    w_in = bf(w_in)
    wc = wkr = wd = w_in
    wuq = bf(mla_w_uq)
    wukv = bf(mla_w_ukv)
    wo = bf(w_o)
    ffn_f32 = ((ffn1_w_gate, ffn1_w_up, ffn1_w_down), (ffn2_w_gate, ffn2_w_up, ffn2_w_down))
    ffn_gains = ((ffn1_pre_g, ffn1_post_g), (ffn2_pre_g, ffn2_post_g))
    ffn_bf16 = {(0, 0): tuple(_cast_layer(w, 0) for w in ffn_f32[0])}

    def ffn(xs, l, which):
        nxt = (l, 1) if which == 0 else (l + 1, 0)
        cast_next = None
        if nxt[0] < depth and nxt not in ffn_bf16:
            cast_next = ffn_f32[nxt[1]] + (nxt[0],)
        gpre, gpost = (g[l][None, None, :] for g in ffn_gains[which])
        xs, converted = _ffn(xs, gpre, gpost, ffn_bf16[(l, which)], 0, cast_next)
        if cast_next is not None:
            ffn_bf16[nxt] = converted
        return xs
    row = lambda g: g[:, None, :]
    mix_pre_g, mix_post_g, mla_q_norm_g, mla_kv_norm_g = map(
        row, (mix_pre_g, mix_post_g, mla_q_norm_g, mla_kv_norm_g))

    outs = []
    for bi in range(batch):
        xs = x[bi]
        tabs_a = _rope_tables(*_rope_angles(positions[bi], MLA_ROPE))
        tabs_p = _rope_tables(*_rope_angles(positions[bi], PART_ROPE))
        for l in range(depth):
            xs = ffn(xs, l, 0)
            qkv = _dproj(xs, mix_pre_g, wd, tabs_p, l)
            qm, km, vm = _mproj(xs, mix_pre_g, wc, wkr, mla_q_norm_g, wuq,
                                mla_kv_norm_g, wukv, tabs_a, l)
            oa = _flash(qm, km, vm)
            ob = _dilated(qkv)
            xs = _oproj(xs, oa, ob, wo, mix_post_g, l)
            xs = ffn(xs, l, 1)
        outs.append(xs)
    return jnp.stack(outs, axis=0)
```
